```python
import math
import jax, jax.numpy as jnp
from jax import lax
import numpy as np

D_MODEL = 2048
BATCH = 4
SEQ = 2048
DEPTH = 4
DEC_BATCH = 8
DEC_SEQ = 8
PAST_LEN = 16384
PAGE_SIZE = 128

N_MIXERS = 4
PLE_DIM = 256
D_FF = 5632
FFN_CONV = 3
A_CONV = 3
C_CONV = 31
DSA_WINDOWS = (128, 512, 2048)
DSA_DILATIONS = (1, 4, 16)
DSA_GROUPS = 3
DSA_HPG = 8
DSA_HEAD_DIM = 64
DSA_HEADS = DSA_GROUPS * DSA_HPG
FOX_HEADS = 16
FOX_HEAD_DIM = D_MODEL // FOX_HEADS
FOX_FORGET_BIAS = 7.0
Q_BLOCK = 128
EPS = 1e-6
NEG = -1e30

kernel_name = "hybrid_conv_dilated_conformer_fox_decoder_step"


def rmsnorm(x, g):
    xf = x.astype(jnp.float32)
    y = xf * lax.rsqrt(jnp.mean(xf * xf, axis=-1, keepdims=True) + EPS)
    return (y * g.astype(jnp.float32)).astype(x.dtype)


def layernorm(x, g, b):
    xf = x.astype(jnp.float32)
    mu = jnp.mean(xf, axis=-1, keepdims=True)
    xc = xf - mu
    y = xc * lax.rsqrt(jnp.mean(xc * xc, axis=-1, keepdims=True) + EPS)
    return (y * g.astype(jnp.float32) + b.astype(jnp.float32)).astype(x.dtype)


def causal_dwconv(u, hist, w):
    k = w.shape[0]
    ext = jnp.concatenate([hist.astype(u.dtype), u], axis=1)
    y = lax.conv_general_dilated(ext, w[:, None, :].astype(u.dtype), window_strides=(1,), padding='VALID',
                                 dimension_numbers=('NWC', 'WIO', 'NWC'), feature_group_count=u.shape[-1])
    return y, ext[:, ext.shape[1] - (k - 1):]


def short_gated_conv(h, hist, w_in, conv_w, w_out):
    gb, gc, v = jnp.split(h @ w_in, 3, axis=-1)
    c, new_hist = causal_dwconv(gc * v, hist, conv_w)
    return (gb * c) @ w_out, new_hist


def conformer_conv(h, hist, w_in, conv_w, ln_g, ln_b, w_out):
    a, g = jnp.split(h @ w_in, 2, axis=-1)
    c, new_hist = causal_dwconv(a * jax.nn.sigmoid(g), hist, conv_w)
    return jax.nn.silu(layernorm(c, ln_g, ln_b)) @ w_out, new_hist


def gated_conv_ffn(h, hist, w_up, conv_w, w_down):
    g, u = jnp.split(h @ w_up, 2, axis=-1)
    gc, new_hist = causal_dwconv(g, hist, conv_w)
    return (jax.nn.silu(gc) * u) @ w_down, new_hist


def per_layer_embed(x, p, g, w_gate, w_proj):
    return x + jax.nn.sigmoid(rmsnorm(x, g) @ w_gate) * (p @ w_proj)


def alibi_slopes(n):
    return 2.0 ** (-8.0 * jnp.arange(1, n + 1, dtype=jnp.float32) / n)


def dsa_project(h, w_qkv, qn, kn):
    b, t, _ = h.shape
    qkv = (h @ w_qkv).reshape(b, t, 3, DSA_HEADS, DSA_HEAD_DIM)
    q = rmsnorm(qkv[:, :, 0], qn)
    k = rmsnorm(qkv[:, :, 1], kn)
    v = qkv[:, :, 2]
    split = lambda a: [a[:, :, g * DSA_HPG:(g + 1) * DSA_HPG] for g in range(DSA_GROUPS)]
    return split(q), split(k), split(v)


def dilated_attend(q, k_rows, v_rows, q_row, q_pos, window, dil, slopes):
    offs = jnp.arange(0, window + 1, dil)
    kidx = q_row[:, None] - offs[None, :]
    valid = (kidx >= 0) & ((q_pos[:, None] - offs[None, :]) >= 0)
    kidx = jnp.maximum(kidx, 0)
    kg = k_rows[:, kidx]
    vg = v_rows[:, kidx]
    s = jnp.einsum('bqhd,bqkhd->bhqk', q, kg).astype(jnp.float32) * (DSA_HEAD_DIM ** -0.5)
    s = s - slopes[:, None, None] * offs.astype(jnp.float32)
    s = jnp.where(valid[None, None], s, NEG)
    m = jnp.max(s, axis=-1, keepdims=True)
    p = jnp.exp(s - m)
    l = jnp.sum(p, axis=-1, keepdims=True)
    o = jnp.einsum('bhqk,bqkhd->bqhd', (p / l).astype(v_rows.dtype), vg)
    lse = (m + jnp.log(l))[..., 0].transpose(0, 2, 1)
    return o, lse


def combine_groups(outs, lses):
    w = jax.nn.softmax(jnp.stack(lses, axis=0), axis=0)
    return jnp.einsum('gbqh,gbqhd->bqhd', w, jnp.stack(outs, axis=0).astype(jnp.float32))


def dsa_prompt(h, w_qkv, qn, kn, w_out):
    b, t, _ = h.shape
    qs, ks, vs = dsa_project(h, w_qkv, qn, kn)
    slopes = alibi_slopes(DSA_HEADS).reshape(DSA_GROUPS, DSA_HPG)

    def block(i):
        pos = i * Q_BLOCK + jnp.arange(Q_BLOCK)
        outs, lses = [], []
        for g in range(DSA_GROUPS):
            qb = lax.dynamic_slice_in_dim(qs[g], i * Q_BLOCK, Q_BLOCK, axis=1)
            o, lse = dilated_attend(qb, ks[g], vs[g], pos, pos, DSA_WINDOWS[g], DSA_DILATIONS[g], slopes[g])
            outs.append(o)
            lses.append(lse)
        return combine_groups(outs, lses)

    o = lax.map(block, jnp.arange(t // Q_BLOCK))
    o = o.transpose(1, 0, 2, 3, 4).reshape(b, t, DSA_HPG * DSA_HEAD_DIM).astype(h.dtype)
    bufs = [jnp.stack([ks[g], vs[g]], axis=2)[:, t - min(DSA_WINDOWS[g], t):] for g in range(DSA_GROUPS)]
    return o @ w_out, bufs


def dsa_sample(h, bufs, w_qkv, qn, kn, w_out):
    b, t, _ = h.shape
    qs, ks, vs = dsa_project(h, w_qkv, qn, kn)
    slopes = alibi_slopes(DSA_HEADS).reshape(DSA_GROUPS, DSA_HPG)
    q_pos = PAST_LEN + jnp.arange(t)
    outs, lses, new_bufs = [], [], []
    for g in range(DSA_GROUPS):
        buf = bufs[g].astype(h.dtype)
        L = buf.shape[1]
        rows = jnp.concatenate([buf, jnp.stack([ks[g], vs[g]], axis=2)], axis=1)
        o, lse = dilated_attend(qs[g], rows[:, :, 0], rows[:, :, 1], L + jnp.arange(t), q_pos,
                                DSA_WINDOWS[g], DSA_DILATIONS[g], slopes[g])
        outs.append(o)
        lses.append(lse)
        new_bufs.append(rows[:, t:])
    o = combine_groups(outs, lses).reshape(b, t, DSA_HPG * DSA_HEAD_DIM).astype(h.dtype)
    return o @ w_out, new_bufs


def fox_project(h, w_qkvf, b_f, qn, kn):
    b, t, _ = h.shape
    z = h @ w_qkvf
    qkv = z[..., :3 * D_MODEL].reshape(b, t, 3, FOX_HEADS, FOX_HEAD_DIM)
    logf = jax.nn.log_sigmoid(z[..., 3 * D_MODEL:].astype(jnp.float32) + b_f.astype(jnp.float32))
    q = rmsnorm(qkv[:, :, 0], qn)
    k = rmsnorm(qkv[:, :, 1], kn)
    return q, k, qkv[:, :, 2], logf


def fox_prompt(h, w_qkvf, b_f, qn, kn, w_out):
    b, t, _ = h.shape
    q, k, v, logf = fox_project(h, w_qkvf, b_f, qn, kn)
    F = jnp.cumsum(logf, axis=1).transpose(0, 2, 1)
    kpos = jnp.arange(t)
    scale = FOX_HEAD_DIM ** -0.5

    def block(i):
        start = i * Q_BLOCK
        qb = lax.dynamic_slice_in_dim(q, start, Q_BLOCK, axis=1)
        fq = lax.dynamic_slice_in_dim(F, start, Q_BLOCK, axis=2)
        qpos = start + jnp.arange(Q_BLOCK)
        s = jnp.einsum('bqhd,bkhd->bhqk', qb, k).astype(jnp.float32) * scale + fq[..., None] - F[:, :, None, :]
        s = jnp.where(kpos[None, :] <= qpos[:, None], s, NEG)
        p = jax.nn.softmax(s, axis=-1)
        return jnp.einsum('bhqk,bkhd->bqhd', p.astype(v.dtype), v)

    o = lax.map(block, jnp.arange(t // Q_BLOCK)).transpose(1, 0, 2, 3, 4).reshape(b, t, D_MODEL)
    return o @ w_out, jnp.stack([k, v], axis=2), logf.astype(h.dtype)


def fox_sample(h, kv_pool, logf_pool, page_table, w_qkvf, b_f, qn, kn, w_out):
    b, t, _ = h.shape
    q, k, v, logf = fox_project(h, w_qkvf, b_f, qn, kn)
    n_pages = page_table.shape[1]
    page = kv_pool.shape[1]
    scale = FOX_HEAD_DIM ** -0.5
    fq = jnp.cumsum(logf, axis=1).transpose(0, 2, 1)
    s = jnp.einsum('bqhd,bkhd->bhqk', q, k).astype(jnp.float32) * scale + fq[..., None] - fq[:, :, None, :]
    s = jnp.where(jnp.arange(t)[None, :] <= jnp.arange(t)[:, None], s, NEG)
    m = jnp.max(s, axis=-1)
    p = jnp.exp(s - m[..., None])
    l = jnp.sum(p, axis=-1)
    acc = jnp.einsum('bhqk,bkhd->bhqd', p, v.astype(jnp.float32))
    logf_past = logf_pool[page_table].astype(jnp.float32).reshape(b, n_pages * page, FOX_HEADS)
    r = lax.cumsum(logf_past, axis=1, reverse=True) - logf_past
    r_pages = r.reshape(b, n_pages, page, FOX_HEADS).transpose(1, 0, 3, 2)

    def step(carry, xs):
        m, l, acc = carry
        pt, rp = xs
        kv = kv_pool[pt]
        s = jnp.einsum('bqhd,bkhd->bhqk', q, kv[:, :, 0].astype(q.dtype)).astype(jnp.float32) * scale
        s = s + fq[..., None] + rp[:, :, None, :]
        m_new = jnp.maximum(m, jnp.max(s, axis=-1))
        c = jnp.exp(m - m_new)
        p = jnp.exp(s - m_new[..., None])
        acc = acc * c[..., None] + jnp.einsum('bhqk,bkhd->bhqd', p, kv[:, :, 1].astype(jnp.float32))
        return (m_new, l * c + jnp.sum(p, axis=-1), acc), None

    (m, l, acc), _ = lax.scan(step, (m, l, acc), (page_table.T, r_pages))
    o = (acc / l[..., None]).transpose(0, 2, 1, 3).reshape(b, t, D_MODEL).astype(h.dtype)
    return o @ w_out, jnp.stack([k, v], axis=2), logf.astype(h.dtype)


def setup_inputs(seed: int = 0) -> dict:
    key = jax.random.key(seed)
    keys = iter(jax.random.split(key, 64))
    nrm = lambda shape, scale=1.0: jax.random.normal(next(keys), shape, jnp.float32) * scale
    gain = lambda shape: 1.0 + nrm(shape, 0.05)
    d = D_MODEL
    n_pages = PAST_LEN // PAGE_SIZE
    n_used = DEC_BATCH * n_pages
    n_pool = n_used + max(1, n_used // 4)
    page_table = jax.random.permutation(next(keys), n_pool)[:n_used].reshape(DEC_BATCH, n_pages).astype(jnp.int32)
    return {
        "x_prompt": nrm((BATCH, SEQ, d)),
        "x_sample": nrm((DEC_BATCH, DEC_SEQ, d)),
        "state_conv_a": nrm((DEC_BATCH, A_CONV - 1, d)),
        "cache_dsa_g0": nrm((DEC_BATCH, min(DSA_WINDOWS[0], PAST_LEN), 2, DSA_HPG, DSA_HEAD_DIM)),
        "cache_dsa_g1": nrm((DEC_BATCH, min(DSA_WINDOWS[1], PAST_LEN), 2, DSA_HPG, DSA_HEAD_DIM)),
        "cache_dsa_g2": nrm((DEC_BATCH, min(DSA_WINDOWS[2], PAST_LEN), 2, DSA_HPG, DSA_HEAD_DIM)),
        "state_conv_c": nrm((DEC_BATCH, C_CONV - 1, d)),
        "cache_fox_kv": nrm((n_pool, PAGE_SIZE, 2, FOX_HEADS, FOX_HEAD_DIM)),
        "cache_fox_logf": jax.nn.log_sigmoid(FOX_FORGET_BIAS + nrm((n_pool, PAGE_SIZE, FOX_HEADS), 0.5)),
        "state_ffn_conv": nrm((DEPTH, DEC_BATCH, FFN_CONV - 1, D_FF)),
        "page_table": page_table,
        "p_prompt": nrm((DEPTH, BATCH, SEQ, PLE_DIM)),
        "p_sample": nrm((DEPTH, DEC_BATCH, DEC_SEQ, PLE_DIM)),
        "norm_mix": gain((DEPTH, d)),
        "norm_ffn": gain((DEPTH, d)),
        "w_a_in": nrm((d, 3 * d), d ** -0.5),
        "conv_a_w": nrm((A_CONV, d), A_CONV ** -0.5),
        "w_a_out": nrm((d, d), d ** -0.5),
        "w_b_qkv": nrm((d, 3 * DSA_HEADS * DSA_HEAD_DIM), d ** -0.5),
        "qn_b": gain((DSA_HEAD_DIM,)),
        "kn_b": gain((DSA_HEAD_DIM,)),
        "w_b_out": nrm((DSA_HPG * DSA_HEAD_DIM, d), (DSA_HPG * DSA_HEAD_DIM) ** -0.5),
        "w_c_in": nrm((d, 2 * d), d ** -0.5),
        "conv_c_w": nrm((C_CONV, d), C_CONV ** -0.5),
        "ln_c_g": gain((d,)),
        "ln_c_b": nrm((d,), 0.02),
        "w_c_out": nrm((d, d), d ** -0.5),
        "w_d_qkvf": nrm((d, 3 * d + FOX_HEADS), d ** -0.5),
        "b_f": FOX_FORGET_BIAS + nrm((FOX_HEADS,), 0.5),
        "qn_d": gain((FOX_HEAD_DIM,)),
        "kn_d": gain((FOX_HEAD_DIM,)),
        "w_d_out": nrm((d, d), d ** -0.5),
        "w_ffn_up": nrm((DEPTH, d, 2 * D_FF), d ** -0.5),
        "conv_ffn_w": nrm((DEPTH, FFN_CONV, D_FF), FFN_CONV ** -0.5),
        "w_ffn_down": nrm((DEPTH, D_FF, d), D_FF ** -0.5),
        "norm_ple": gain((DEPTH, d)),
        "w_ple_gate": nrm((DEPTH, d, d), d ** -0.5),
        "w_ple_proj": nrm((DEPTH, PLE_DIM, d), PLE_DIM ** -0.5),
    }


def reference(x_prompt, x_sample, state_conv_a, cache_dsa_g0, cache_dsa_g1, cache_dsa_g2, state_conv_c,
              cache_fox_kv, cache_fox_logf, state_ffn_conv, page_table, p_prompt, p_sample,
              norm_mix, norm_ffn, w_a_in, conv_a_w, w_a_out, w_b_qkv, qn_b, kn_b, w_b_out,
              w_c_in, conv_c_w, ln_c_g, ln_c_b, w_c_out, w_d_qkvf, b_f, qn_d, kn_d, w_d_out,
              w_ffn_up, conv_ffn_w, w_ffn_down, norm_ple, w_ple_gate, w_ple_proj):
    xp, xs = x_prompt, x_sample
    bp = xp.shape[0]
    ffn_p, ffn_s = [], []
    for i in range(DEPTH):
        hp = rmsnorm(xp, norm_mix[i])
        hs = rmsnorm(xs, norm_mix[i])
        kind = i % N_MIXERS
        if kind == 0:
            mp, conv_a_p = short_gated_conv(hp, jnp.zeros((bp, A_CONV - 1, D_MODEL), hp.dtype), w_a_in, conv_a_w, w_a_out)
            ms, conv_a_s = short_gated_conv(hs, state_conv_a, w_a_in, conv_a_w, w_a_out)
        elif kind == 1:
            mp, dsa_p = dsa_prompt(hp, w_b_qkv, qn_b, kn_b, w_b_out)
            ms, dsa_s = dsa_sample(hs, [cache_dsa_g0, cache_dsa_g1, cache_dsa_g2], w_b_qkv, qn_b, kn_b, w_b_out)
        elif kind == 2:
            mp, conv_c_p = conformer_conv(hp, jnp.zeros((bp, C_CONV - 1, D_MODEL), hp.dtype), w_c_in, conv_c_w, ln_c_g, ln_c_b, w_c_out)
            ms, conv_c_s = conformer_conv(hs, state_conv_c, w_c_in, conv_c_w, ln_c_g, ln_c_b, w_c_out)
        else:
            mp, fox_kv_p, fox_logf_p = fox_prompt(hp, w_d_qkvf, b_f, qn_d, kn_d, w_d_out)
            ms, fox_kv_s, fox_logf_s = fox_sample(hs, cache_fox_kv, cache_fox_logf, page_table, w_d_qkvf, b_f, qn_d, kn_d, w_d_out)
        xp = xp + mp
        xs = xs + ms
        fp, hist_p = gated_conv_ffn(rmsnorm(xp, norm_ffn[i]), jnp.zeros((bp, FFN_CONV - 1, D_FF), xp.dtype),
                                    w_ffn_up[i], conv_ffn_w[i], w_ffn_down[i])
        fs, hist_s = gated_conv_ffn(rmsnorm(xs, norm_ffn[i]), state_ffn_conv[i], w_ffn_up[i], conv_ffn_w[i], w_ffn_down[i])
        xp = per_layer_embed(xp + fp, p_prompt[i], norm_ple[i], w_ple_gate[i], w_ple_proj[i])
        xs = per_layer_embed(xs + fs, p_sample[i], norm_ple[i], w_ple_gate[i], w_ple_proj[i])
        ffn_p.append(hist_p)
        ffn_s.append(hist_s)
    return (xp, xs, conv_a_p, conv_a_s, dsa_p[0], dsa_s[0], dsa_p[1], dsa_s[1], dsa_p[2], dsa_s[2],
            conv_c_p, conv_c_s, fox_kv_p, fox_kv_s, fox_logf_p, fox_logf_s, jnp.stack(ffn_p), jnp.stack(ffn_s))
```

```python
import functools

import jax
import jax.numpy as jnp
from jax import lax
from jax.experimental import pallas as pl
from jax.experimental.pallas import tpu as pltpu

F32 = jnp.float32
BF16 = jnp.bfloat16
EPS = 1e-6
NEG = -1e30

VMEM_LIMIT_BYTES = 56 * 1024 * 1024
LANES = 128
SUBLANES = 8

A_CONV = 3
C_CONV = 31
FFN_CONV = 3
DSA_WINDOWS = (128, 512, 2048)
DSA_DILATIONS = (1, 4, 16)
DSA_GROUPS = 3
DSA_HPG = 8
DSA_HEAD_DIM = 64
DSA_HEADS = DSA_GROUPS * DSA_HPG
DSA_GW = DSA_HPG * DSA_HEAD_DIM
DSA_KEYS = 128
FOX_HEADS = 16
Q_BLOCK = 128


def _cp(*sem):
    return pltpu.CompilerParams(dimension_semantics=sem, vmem_limit_bytes=VMEM_LIMIT_BYTES)


def _sds(shape, dtype):
    return jax.ShapeDtypeStruct(shape, dtype)


def _rmsnorm_kernel(x_ref, g_ref, o_ref):
    x = x_ref[...]
    ms = jnp.mean(x * x, axis=-1, keepdims=True)
    o_ref[...] = (x * lax.rsqrt(ms + EPS) * g_ref[...]).astype(o_ref.dtype)


def rmsnorm_bf16(x, g):
    m, d = x.shape
    tm = min(m, 512)
    return pl.pallas_call(
        _rmsnorm_kernel,
        out_shape=_sds((m, d), BF16),
        grid=(m // tm,),
        in_specs=[pl.BlockSpec((tm, d), lambda i: (i, 0)),
                  pl.BlockSpec((1, d), lambda i: (0, 0))],
        out_specs=pl.BlockSpec((tm, d), lambda i: (i, 0)),
        compiler_params=_cp("parallel"),
        name="rmsnorm",
    )(x, g.reshape(1, d))


def _mm_body(*refs, parts, n_extra, epilogue):
    a = refs[0][...]
    accs = [jnp.dot(a, refs[1 + p][...], preferred_element_type=F32) for p in range(parts)]
    extras = refs[1 + parts:1 + parts + n_extra]
    outs = refs[1 + parts + n_extra:]
    epilogue(accs, extras, outs)


def matmul(a, w, *, tm, tn, epilogue, out_shapes, out_specs, parts=1, extras=(), extra_specs=(), name):
    m, k = a.shape
    np_ = w.shape[1] // parts
    nj = np_ // tn
    tm = min(tm, m)
    in_specs = [pl.BlockSpec((tm, k), lambda i, j: (i, 0))]
    in_specs += [pl.BlockSpec((k, tn), functools.partial(lambda i, j, p: (0, p * nj + j), p=p))
                 for p in range(parts)]
    in_specs += list(extra_specs)
    return pl.pallas_call(
        functools.partial(_mm_body, parts=parts, n_extra=len(extras), epilogue=epilogue),
        out_shape=out_shapes,
        grid=(m // tm, nj),
        in_specs=in_specs,
        out_specs=out_specs,
        compiler_params=_cp("parallel", "arbitrary"),
        name=name,
    )(a, *([w] * parts), *extras)


def _tile_spec(tm, tn):
    return pl.BlockSpec((tm, tn), lambda i, j: (i, j))


def _ep_plain(accs, extras, outs):
    outs[0][...] = accs[0].astype(outs[0].dtype)


def _ep_residual(accs, extras, outs):
    outs[0][...] = extras[0][...] + accs[0]


def matmul_plain(a, w, *, tm=1024, tn=1024, name):
    m, n = a.shape[0], w.shape[1]
    tm = min(tm, m)
    return matmul(a, w, tm=tm, tn=tn, epilogue=_ep_plain, out_shapes=_sds((m, n), F32),
                  out_specs=_tile_spec(tm, tn), name=name)


def matmul_residual(a, w, res, *, tm=1024, tn=512, name):
    m, n = a.shape[0], w.shape[1]
    tm = min(tm, m)
    return matmul(a, w, tm=tm, tn=tn, epilogue=_ep_residual, out_shapes=_sds((m, n), F32),
                  out_specs=_tile_spec(tm, tn), extras=(res,), extra_specs=(_tile_spec(tm, tn),), name=name)


def _ep_gate_split(accs, extras, outs):
    outs[0][...] = accs[0]
    outs[1][...] = accs[1] * accs[2]


def _ep_glu(accs, extras, outs):
    outs[0][...] = accs[0] * jax.nn.sigmoid(accs[1])


def _ep_qk_norm(accs, extras, outs, *, head_dim, blocks_per_part):
    j = pl.program_id(1)
    acc = accs[0]
    gain = extras[0][0]

    @pl.when(j < 2 * blocks_per_part)
    def _():
        tn = acc.shape[1]
        for c in range(tn // LANES):
            x = acc[:, c * LANES:(c + 1) * LANES]
            g = gain[:, c * LANES:(c + 1) * LANES]
            sq = x * x
            if head_dim == LANES:
                ms = jnp.sum(sq, axis=-1, keepdims=True) * (1.0 / head_dim)
            else:
                lane = lax.broadcasted_iota(jnp.int32, sq.shape, 1)
                first = lane < head_dim
                s_lo = jnp.sum(jnp.where(first, sq, 0.0), axis=-1, keepdims=True)
                s_hi = jnp.sum(jnp.where(first, 0.0, sq), axis=-1, keepdims=True)
                ms = jnp.where(first, s_lo, s_hi) * (1.0 / head_dim)
            outs[0][:, c * LANES:(c + 1) * LANES] = x * lax.rsqrt(ms + EPS) * g

    @pl.when(j >= 2 * blocks_per_part)
    def _():
        outs[0][...] = acc


def matmul_qk_norm(a, w, qn, kn, *, head_dim, tm=1024, tn=512, name):
    m, n = a.shape[0], w.shape[1]
    tm = min(tm, m)
    bpp = n // 3 // tn
    reps = tn // head_dim
    gains = jnp.stack([jnp.tile(qn, reps), jnp.tile(kn, reps), jnp.ones((tn,), F32)]).reshape(3, 1, tn)
    gain_spec = pl.BlockSpec((1, 1, tn), lambda i, j: (j // bpp, 0, 0))
    return matmul(a, w, tm=tm, tn=tn,
                  epilogue=functools.partial(_ep_qk_norm, head_dim=head_dim, blocks_per_part=bpp),
                  out_shapes=_sds((m, n), F32), out_specs=_tile_spec(tm, tn),
                  extras=(gains,), extra_specs=(gain_spec,), name=name)


def _log_sigmoid(x):
    return jnp.minimum(x, 0.0) - jnp.log1p(jnp.exp(-jnp.abs(x)))


def _ep_logf(accs, extras, outs):
    outs[0][...] = _log_sigmoid(accs[0] + extras[0][...])


def _ple_body(h_ref, wg_ref, p_ref, wp_ref, x_ref, o_ref):
    gate = jax.nn.sigmoid(jnp.dot(h_ref[...], wg_ref[...], preferred_element_type=F32))
    proj = jnp.dot(p_ref[...].astype(BF16), wp_ref[...], preferred_element_type=F32)
    o_ref[...] = x_ref[...] + gate * proj


def per_layer_embed(x, h, p, w_gate, w_proj, *, tm=1024, tn=512):
    m, d = x.shape
    tm = min(tm, m)
    pd = p.shape[1]
    return pl.pallas_call(
        _ple_body,
        out_shape=_sds((m, d), F32),
        grid=(m // tm, d // tn),
        in_specs=[pl.BlockSpec((tm, d), lambda i, j: (i, 0)),
                  pl.BlockSpec((d, tn), lambda i, j: (0, j)),
                  pl.BlockSpec((tm, pd), lambda i, j: (i, 0)),
                  pl.BlockSpec((pd, tn), lambda i, j: (0, j)),
                  _tile_spec(tm, tn)],
        out_specs=_tile_spec(tm, tn),
        compiler_params=_cp("parallel", "arbitrary"),
        name="per_layer_embed",
    )(h, w_gate, p, w_proj, x)


def _causal_taps(prev, cur, w, taps):
    p_rows, tm = prev.shape[0], cur.shape[0]
    ext = jnp.concatenate([prev, cur], axis=0)
    n = p_rows + tm
    acc = None
    for rot in range(SUBLANES):
        ks = [k for k in range(taps) if (p_rows - (taps - 1) + k) % SUBLANES == rot]
        if not ks:
            continue
        shifted = ext if rot == 0 else pltpu.roll(ext, n - rot, axis=0)
        for k in ks:
            base = p_rows - (taps - 1) + k - rot
            term = w[k:k + 1, :] * shifted[base:base + tm, :]
            acc = term if acc is None else acc + term
    return acc


def _conv_gate_kernel(cur_ref, prev_ref, hist_ref, w_ref, gate_ref, o_ref, *, taps, kind, chunk):
    at_start = pl.program_id(1) == 0
    nb, tm, tc = cur_ref.shape
    for b in range(nb):
        for c0 in range(0, tc, chunk):
            cs = slice(c0, c0 + chunk)
            prev = jnp.where(at_start, hist_ref[b, :, cs], prev_ref[b, :, cs])
            y = _causal_taps(prev, cur_ref[b, :, cs], w_ref[:, cs], taps)
            g = gate_ref[b, :, cs]
            if kind == "gate":
                out = g * y
            else:
                out = y * jax.nn.sigmoid(y) * g
            o_ref[b, :, cs] = out.astype(o_ref.dtype)


def _conv_ln_kernel(cur_ref, prev_ref, hist_ref, w_ref, lng_ref, lnb_ref, o_ref, c_sc, *, taps, chunk):
    at_start = pl.program_id(1) == 0
    nb, tm, tc = cur_ref.shape
    for b in range(nb):
        for c0 in range(0, tc, chunk):
            cs = slice(c0, c0 + chunk)
            prev = jnp.where(at_start, hist_ref[b, :, cs], prev_ref[b, :, cs])
            c_sc[:, cs] = _causal_taps(prev, cur_ref[b, :, cs], w_ref[:, cs], taps)
        c = c_sc[...]
        mu = jnp.mean(c, axis=-1, keepdims=True)
        xc = c - mu
        var = jnp.mean(xc * xc, axis=-1, keepdims=True)
        y = xc * lax.rsqrt(var + EPS) * lng_ref[...] + lnb_ref[...]
        o_ref[b] = (y * jax.nn.sigmoid(y)).astype(o_ref.dtype)


def _pad_hist(hist, p_rows):
    b, k1, c = hist.shape
    return jnp.concatenate([jnp.zeros((b, p_rows - k1, c), hist.dtype), hist], axis=1)


def causal_conv_fused(src, hist, w, *, taps, kind, gate=None, gate_col0=0, src_cols=None,
                      ln=None, tm, tc, nb):
    bsz, t, _ = src.shape
    c = src_cols if src_cols is not None else src.shape[2]
    p_rows = SUBLANES if taps - 1 <= SUBLANES else 32
    hist_p = _pad_hist(hist, p_rows)
    tm = min(tm, t)
    n_t = t // tm
    chunk = min(tc, 256)
    if t >= p_rows:
        prev_src = src
        per = tm // p_rows
        prev_spec = pl.BlockSpec((nb, p_rows, tc), lambda b, i, j: (b, jnp.maximum(i * per - 1, 0), j))
    else:
        prev_src = hist_p
        prev_spec = pl.BlockSpec((nb, p_rows, tc), lambda b, i, j: (b, 0, j))
    cur_spec = pl.BlockSpec((nb, tm, tc), lambda b, i, j: (b, i, j))
    hist_spec = pl.BlockSpec((nb, p_rows, tc), lambda b, i, j: (b, 0, j))
    w_spec = pl.BlockSpec((taps, tc), lambda b, i, j: (0, j))
    grid = (bsz // nb, n_t, c // tc)
    if kind == "ln":
        lng, lnb = ln
        vec_spec = pl.BlockSpec((1, tc), lambda b, i, j: (0, j))
        return pl.pallas_call(
            functools.partial(_conv_ln_kernel, taps=taps, chunk=chunk),
            out_shape=_sds((bsz, t, c), BF16), grid=grid,
            in_specs=[cur_spec, prev_spec, hist_spec, w_spec, vec_spec, vec_spec],
            out_specs=cur_spec,
            scratch_shapes=[pltpu.VMEM((tm, tc), F32)],
            compiler_params=_cp("parallel", "arbitrary", "arbitrary"),
            name="conv_ln",
        )(src, prev_src, hist_p, w, lng.reshape(1, c), lnb.reshape(1, c))
    gate_spec = pl.BlockSpec((nb, tm, tc), lambda b, i, j: (b, i, gate_col0 + j))
    return pl.pallas_call(
        functools.partial(_conv_gate_kernel, taps=taps, kind=kind, chunk=chunk),
        out_shape=_sds((bsz, t, c), BF16), grid=grid,
        in_specs=[cur_spec, prev_spec, hist_spec, w_spec, gate_spec],
        out_specs=cur_spec,
        compiler_params=_cp("parallel", "arbitrary", "arbitrary"),
        name="conv_" + kind,
    )(src, prev_src, hist_p, w, gate)


def _alibi_slope(head):
    return 2.0 ** (-8.0 * (head + 1) / DSA_HEADS)


def _dsa_prompt_kernel(q_ref, kp_ref, kc_ref, vp_ref, vc_ref, o_ref, lse_ref, *, dil, group):
    i = pl.program_id(2)
    tq = Q_BLOCK
    q = q_ref[0].astype(BF16)
    k = jnp.concatenate([kp_ref[0], kc_ref[0]], axis=0).astype(BF16)
    v = jnp.concatenate([vp_ref[0], vc_ref[0]], axis=0).astype(BF16)
    row = lax.broadcasted_iota(jnp.int32, (tq, 2 * tq), 0)
    col = lax.broadcasted_iota(jnp.int32, (tq, 2 * tq), 1)
    dist = row + tq - col
    valid = (dist >= 0) & (dist <= DSA_KEYS) & ((col >= tq) | (i > 0))
    distf = dist.astype(F32) * float(dil)
    for h in range(DSA_HPG):
        sl = slice(h * DSA_HEAD_DIM, (h + 1) * DSA_HEAD_DIM)
        s = lax.dot_general(q[:, sl], k[:, sl], (((1,), (1,)), ((), ())), preferred_element_type=F32)
        s = s * (DSA_HEAD_DIM ** -0.5) - _alibi_slope(group * DSA_HPG + h) * distf
        s = jnp.where(valid, s, NEG)
        m = jnp.max(s, axis=-1, keepdims=True)
        p = jnp.exp(s - m)
        l = jnp.sum(p, axis=-1, keepdims=True)
        o = jnp.dot(p.astype(BF16), v[:, sl], preferred_element_type=F32) / l
        o_ref[0, :, sl] = o
        lse_ref[0, :, sl] = jnp.broadcast_to(m + jnp.log(l), (tq, DSA_HEAD_DIM))


def dsa_prompt_group(qkv, group):
    bsz, t, n = qkv.shape
    dil = DSA_DILATIONS[group]
    s_len = t // dil
    nblk = n // DSA_GW
    view = qkv.reshape(bsz, s_len, dil * n)
    tq = Q_BLOCK

    def spec(part, prev):
        if prev:
            return pl.BlockSpec((1, tq, DSA_GW),
                                lambda b, r, i: (b, jnp.maximum(i - 1, 0), r * nblk + part * DSA_GROUPS + group))
        return pl.BlockSpec((1, tq, DSA_GW), lambda b, r, i: (b, i, r * nblk + part * DSA_GROUPS + group))

    out_spec = pl.BlockSpec((1, tq, DSA_GW), lambda b, r, i: (b, i, r))
    o, lse = pl.pallas_call(
        functools.partial(_dsa_prompt_kernel, dil=dil, group=group),
        out_shape=(_sds((bsz, s_len, dil * DSA_GW), F32), _sds((bsz, s_len, dil * DSA_GW), F32)),
        grid=(bsz, dil, s_len // tq),
        in_specs=[spec(0, False), spec(1, True), spec(1, False), spec(2, True), spec(2, False)],
        out_specs=(out_spec, out_spec),
        compiler_params=_cp("parallel", "parallel", "arbitrary"),
        name=f"dsa_prompt_g{group}",
    )(view, view, view, view, view)
    return o.reshape(bsz, t, DSA_GW), lse.reshape(bsz, t, DSA_GW)


def _dsa_sample_kernel(q_ref, kc_ref, vc_ref, kn_ref, vn_ref, o_ref, lse_ref, *, cache_len, dil, window, group):
    t = q_ref.shape[1]
    pad = jnp.zeros((LANES - t, DSA_GW), F32)
    q = q_ref[0].astype(BF16)
    k = jnp.concatenate([kc_ref[0], kn_ref[0], pad], axis=0).astype(BF16)
    v = jnp.concatenate([vc_ref[0], vn_ref[0], pad], axis=0).astype(BF16)
    rows = cache_len + LANES
    qi = lax.broadcasted_iota(jnp.int32, (t, rows), 0)
    col = lax.broadcasted_iota(jnp.int32, (t, rows), 1)
    dist = cache_len + qi - col
    valid = (dist >= 0) & (dist <= window) & ((dist & (dil - 1)) == 0)
    distf = dist.astype(F32)
    for h in range(DSA_HPG):
        sl = slice(h * DSA_HEAD_DIM, (h + 1) * DSA_HEAD_DIM)
        s = lax.dot_general(q[:, sl], k[:, sl], (((1,), (1,)), ((), ())), preferred_element_type=F32)
        s = s * (DSA_HEAD_DIM ** -0.5) - _alibi_slope(group * DSA_HPG + h) * distf
        s = jnp.where(valid, s, NEG)
        m = jnp.max(s, axis=-1, keepdims=True)
        p = jnp.exp(s - m)
        l = jnp.sum(p, axis=-1, keepdims=True)
        o = jnp.dot(p.astype(BF16), v[:, sl], preferred_element_type=F32) / l
        o_ref[0, :, sl] = o
        lse_ref[0, :, sl] = jnp.broadcast_to(m + jnp.log(l), (t, DSA_HEAD_DIM))


def dsa_sample_group(qkv, cache, group):
    bsz, t, _ = qkv.shape
    cache_len = cache.shape[1]
    cview = cache.reshape(bsz, cache_len, 2 * DSA_GW)
    new_spec = lambda part: pl.BlockSpec((1, t, DSA_GW), lambda b: (b, 0, part * DSA_GROUPS + group))
    cache_spec = lambda part: pl.BlockSpec((1, cache_len, DSA_GW), lambda b: (b, 0, part))
    out_spec = pl.BlockSpec((1, t, DSA_GW), lambda b: (b, 0, 0))
    return pl.pallas_call(
        functools.partial(_dsa_sample_kernel, cache_len=cache_len, dil=DSA_DILATIONS[group],
                          window=DSA_WINDOWS[group], group=group),
        out_shape=(_sds((bsz, t, DSA_GW), F32), _sds((bsz, t, DSA_GW), F32)),
        grid=(bsz,),
        in_specs=[new_spec(0), cache_spec(0), cache_spec(1), new_spec(1), new_spec(2)],
        out_specs=(out_spec, out_spec),
        compiler_params=_cp("parallel"),
        name=f"dsa_sample_g{group}",
    )(qkv, cview, cview, qkv, qkv)


def _dsa_combine_kernel(o0, o1, o2, l0, l1, l2, out_ref):
    ls = (l0[...], l1[...], l2[...])
    m = jnp.maximum(jnp.maximum(ls[0], ls[1]), ls[2])
    es = [jnp.exp(x - m) for x in ls]
    den = es[0] + es[1] + es[2]
    num = es[0] * o0[...] + es[1] * o1[...] + es[2] * o2[...]
    out_ref[...] = (num / den).astype(out_ref.dtype)


def dsa_combine(outs, lses):
    m, n = outs[0].shape
    tm = min(m, 1024)
    spec = pl.BlockSpec((tm, n), lambda i: (i, 0))
    return pl.pallas_call(
        _dsa_combine_kernel, out_shape=_sds((m, n), BF16), grid=(m // tm,),
        in_specs=[spec] * 6, out_specs=spec, compiler_params=_cp("parallel"), name="dsa_combine",
    )(*outs, *lses)


def _split_bf16(x):
    hi = x.astype(BF16)
    lo = (x - hi.astype(F32)).astype(BF16)
    return hi, lo


def _cumsum_kernel(x_ref, o_ref, *, tn):
    j = pl.program_id(0)
    t = x_ref.shape[1]
    src = lax.broadcasted_iota(jnp.int32, (t, tn), 0)
    dst = lax.broadcasted_iota(jnp.int32, (t, tn), 1) + j * tn
    tri = jnp.where(src <= dst, 1.0, 0.0).astype(BF16)
    hi, lo = _split_bf16(x_ref[...])
    o_ref[...] = (jnp.dot(hi, tri, preferred_element_type=F32)
                  + jnp.dot(lo, tri, preferred_element_type=F32))


def cumsum_lanes(x):
    r, t = x.shape
    tn = min(t, 512)
    return pl.pallas_call(
        functools.partial(_cumsum_kernel, tn=tn), out_shape=_sds((r, t), F32), grid=(t // tn,),
        in_specs=[pl.BlockSpec((r, t), lambda j: (0, 0))],
        out_specs=pl.BlockSpec((r, tn), lambda j: (0, j)),
        compiler_params=_cp("arbitrary"), name="fox_cumsum",
    )(x)


def _fox_prompt_kernel(q_ref, k_ref, v_ref, fk_ref, o_ref, *, scale):
    i = pl.program_id(2)
    tq = q_ref.shape[1]
    t = k_ref.shape[1]
    q = q_ref[0].astype(BF16)
    k = k_ref[0].astype(BF16)
    v = v_ref[0].astype(BF16)
    s = lax.dot_general(q, k, (((1,), (1,)), ((), ())), preferred_element_type=F32) * scale - fk_ref[0]
    qpos = lax.broadcasted_iota(jnp.int32, (tq, t), 0) + i * tq
    kpos = lax.broadcasted_iota(jnp.int32, (tq, t), 1)
    s = jnp.where(kpos <= qpos, s, NEG)
    m = jnp.max(s, axis=-1, keepdims=True)
    p = jnp.exp(s - m)
    l = jnp.sum(p, axis=-1, keepdims=True)
    o_ref[0] = (jnp.dot(p.astype(BF16), v, preferred_element_type=F32) / l).astype(o_ref.dtype)


def fox_prompt_attention(qkv, f_cum, *, tq=256):
    bsz, t, n = qkv.shape
    d = n // 3
    hd = d // FOX_HEADS
    return pl.pallas_call(
        functools.partial(_fox_prompt_kernel, scale=hd ** -0.5),
        out_shape=_sds((bsz, t, d), BF16),
        grid=(bsz, FOX_HEADS, t // tq),
        in_specs=[pl.BlockSpec((1, tq, hd), lambda b, h, i: (b, i, h)),
                  pl.BlockSpec((1, t, hd), lambda b, h, i: (b, 0, FOX_HEADS + h)),
                  pl.BlockSpec((1, t, hd), lambda b, h, i: (b, 0, 2 * FOX_HEADS + h)),
                  pl.BlockSpec((1, 1, t), lambda b, h, i: (b * FOX_HEADS + h, 0, 0))],
        out_specs=pl.BlockSpec((1, tq, hd), lambda b, h, i: (b, i, h)),
        compiler_params=_cp("parallel", "parallel", "arbitrary"),
        name="fox_prompt_attn",
    )(qkv, qkv, qkv, f_cum)


def _fox_sample_kernel(pt_ref, qbd_ref, kv_ref, lf_ref, nkv_ref, nlf_ref, o_ref,
                       m_sc, l_sc, acc_sc, suf_sc, *, scale, t_new, n_pages):
    j = pl.program_id(1)
    r = qbd_ref.shape[1]
    page = kv_ref.shape[1]
    d = qbd_ref.shape[2]
    heads = r // t_new
    expand = jnp.where(lax.broadcasted_iota(jnp.int32, (r, heads), 0) // t_new
                       == lax.broadcasted_iota(jnp.int32, (r, heads), 1), 1.0, 0.0).astype(BF16)
    src = lax.broadcasted_iota(jnp.int32, (page, page), 0)
    dst = lax.broadcasted_iota(jnp.int32, (page, page), 1)

    def expand_and_sum(lf, tri):
        rows, sums = None, None
        for piece in _split_bf16(lf):
            e = jnp.dot(expand, piece, preferred_element_type=F32)
            c = jnp.dot(e.astype(BF16), tri, preferred_element_type=F32)
            rows = e if rows is None else rows + e
            sums = c if sums is None else sums + c
        return rows, sums

    def scores(k_f32):
        return lax.dot_general(qbd_ref[0], k_f32.astype(BF16), (((1,), (1,)), ((), ())),
                               preferred_element_type=F32) * scale

    @pl.when(j == 0)
    def _():
        tri = jnp.where(src <= dst, 1.0, 0.0).astype(BF16)
        _, fq = expand_and_sum(nlf_ref[0], tri)
        s = scores(nkv_ref[0, :, :d]) - fq
        tq = lax.broadcasted_iota(jnp.int32, (r, page), 0) % t_new
        tk = lax.broadcasted_iota(jnp.int32, (r, page), 1)
        s = jnp.where((tk <= tq) & (tk < t_new), s, NEG)
        m = jnp.max(s, axis=-1, keepdims=True)
        p = jnp.exp(s - m)
        m_sc[...] = m
        l_sc[...] = jnp.sum(p, axis=-1, keepdims=True)
        acc_sc[...] = jnp.dot(p.astype(BF16), nkv_ref[0, :, d:].astype(BF16), preferred_element_type=F32)
        suf_sc[...] = jnp.zeros_like(suf_sc)

    @pl.when(j > 0)
    def _():
        tri = jnp.where(src > dst, 1.0, 0.0).astype(BF16)
        rows, later = expand_and_sum(lf_ref[0], tri)
        suf = suf_sc[...]
        s = scores(kv_ref[0, :, :d]) + (later + suf)
        m_old = m_sc[...]
        m_new = jnp.maximum(m_old, jnp.max(s, axis=-1, keepdims=True))
        c = jnp.exp(m_old - m_new)
        p = jnp.exp(s - m_new)
        l_sc[...] = l_sc[...] * c + jnp.sum(p, axis=-1, keepdims=True)
        acc_sc[...] = acc_sc[...] * c + jnp.dot(p.astype(BF16), kv_ref[0, :, d:].astype(BF16),
                                                preferred_element_type=F32)
        m_sc[...] = m_new
        suf_sc[...] = suf + jnp.sum(rows, axis=-1, keepdims=True)

    @pl.when(j == n_pages)
    def _():
        inv = 1.0 / l_sc[...]
        hd = d // heads
        for h in range(heads):
            rs = slice(h * t_new, (h + 1) * t_new)
            cs = slice(h * hd, (h + 1) * hd)
            o_ref[0, :, cs] = (acc_sc[rs, cs] * inv[rs]).astype(o_ref.dtype)


def fox_sample_attention(qkv, logf, kv_pool, logf_pool, page_table):
    bsz, t, n = qkv.shape
    d = n // 3
    heads = FOX_HEADS
    hd = d // heads
    n_pool, page = kv_pool.shape[0], kv_pool.shape[1]
    n_pages = page_table.shape[1]
    r = heads * t
    q = qkv[:, :, :d].reshape(bsz, t, heads, hd).transpose(0, 2, 1, 3)
    eye = jnp.eye(heads, dtype=F32)
    qbd = (q[:, :, :, None, :] * eye[None, :, None, :, None]).reshape(bsz, r, d).astype(BF16)
    pool = kv_pool.reshape(n_pool, page, 2 * d)
    pool_lf = logf_pool.transpose(0, 2, 1)
    new_kv = jnp.concatenate([qkv[:, :, d:], jnp.zeros((bsz, page - t, 2 * d), F32)], axis=1)
    new_lf = jnp.concatenate([logf.transpose(0, 2, 1), jnp.zeros((bsz, heads, page - t), F32)], axis=2)
    pt = page_table.reshape(-1)

    def page_idx(b, j, pt_ref):
        return pt_ref[b * n_pages + n_pages - jnp.maximum(j, 1)]

    grid_spec = pltpu.PrefetchScalarGridSpec(
        num_scalar_prefetch=1,
        grid=(bsz, n_pages + 1),
        in_specs=[pl.BlockSpec((1, r, d), lambda b, j, pt_ref: (b, 0, 0)),
                  pl.BlockSpec((1, page, 2 * d), lambda b, j, pt_ref: (page_idx(b, j, pt_ref), 0, 0)),
                  pl.BlockSpec((1, heads, page), lambda b, j, pt_ref: (page_idx(b, j, pt_ref), 0, 0)),
                  pl.BlockSpec((1, page, 2 * d), lambda b, j, pt_ref: (b, 0, 0)),
                  pl.BlockSpec((1, heads, page), lambda b, j, pt_ref: (b, 0, 0))],
        out_specs=pl.BlockSpec((1, t, d), lambda b, j, pt_ref: (b, 0, 0)),
        scratch_shapes=[pltpu.VMEM((r, 1), F32), pltpu.VMEM((r, 1), F32),
                        pltpu.VMEM((r, d), F32), pltpu.VMEM((r, 1), F32)],
    )
    return pl.pallas_call(
        functools.partial(_fox_sample_kernel, scale=hd ** -0.5, t_new=t, n_pages=n_pages),
        out_shape=_sds((bsz, t, d), BF16),
        grid_spec=grid_spec,
        compiler_params=_cp("parallel", "arbitrary"),
        name="fox_sample_attn",
    )(pt, qbd, pool, pool_lf, new_kv, new_lf)


def _flat(x):
    return x.reshape(-1, x.shape[-1])


def _last_rows(hist, new, n):
    t = new.shape[1]
    if t >= n:
        return new[:, t - n:]
    return jnp.concatenate([hist.astype(new.dtype)[:, t:], new], axis=1)


def _conv_tiles(t):
    return dict(tm=256, nb=1) if t >= 256 else dict(tm=t, nb=None)


def _mixer_a(x, h, hist, w_in, conv_w, w_out, shape):
    bsz, t, d = shape
    gb, u = matmul(h, w_in, tm=1024, tn=512, parts=3, epilogue=_ep_gate_split,
                   out_shapes=(_sds((bsz * t, d), F32), _sds((bsz * t, d), F32)),
                   out_specs=(_tile_spec(min(1024, bsz * t), 512), _tile_spec(min(1024, bsz * t), 512)),
                   name="a_in")
    u3 = u.reshape(bsz, t, d)
    tiles = _conv_tiles(t)
    nb = tiles["nb"] or bsz
    a = causal_conv_fused(u3, hist, conv_w, taps=A_CONV, kind="gate", gate=gb.reshape(bsz, t, d),
                          tm=tiles["tm"], tc=512, nb=nb)
    x = matmul_residual(_flat(a), w_out, x, name="a_out")
    return x, _last_rows(hist, u3, A_CONV - 1)


def _dsa_qkv(h, w_qkv, qn, kn):
    return matmul_qk_norm(h, w_qkv, qn, kn, head_dim=DSA_HEAD_DIM, name="b_qkv")


def _dsa_kv_rows(qkv3, group, n_last):
    qkv3 = qkv3[:, qkv3.shape[1] - n_last:]
    bsz, t, _ = qkv3.shape
    k0 = DSA_GROUPS * DSA_GW + group * DSA_GW
    v0 = 2 * DSA_GROUPS * DSA_GW + group * DSA_GW
    k = qkv3[:, :, k0:k0 + DSA_GW].reshape(bsz, t, DSA_HPG, DSA_HEAD_DIM)
    v = qkv3[:, :, v0:v0 + DSA_GW].reshape(bsz, t, DSA_HPG, DSA_HEAD_DIM)
    return jnp.stack([k, v], axis=2)


def _mixer_b_prompt(x, h, w_qkv, qn, kn, w_out, shape):
    bsz, t, d = shape
    qkv3 = _dsa_qkv(h, w_qkv, qn, kn).reshape(bsz, t, -1)
    outs, lses = zip(*[dsa_prompt_group(qkv3, g) for g in range(DSA_GROUPS)])
    o = dsa_combine([_flat(a) for a in outs], [_flat(a) for a in lses])
    x = matmul_residual(o, w_out, x, name="b_out")
    bufs = [_dsa_kv_rows(qkv3, g, min(DSA_WINDOWS[g], t)) for g in range(DSA_GROUPS)]
    return x, bufs


def _mixer_b_sample(x, h, caches, w_qkv, qn, kn, w_out, shape):
    bsz, t, d = shape
    qkv3 = _dsa_qkv(h, w_qkv, qn, kn).reshape(bsz, t, -1)
    outs, lses = zip(*[dsa_sample_group(qkv3, caches[g], g) for g in range(DSA_GROUPS)])
    o = dsa_combine([_flat(a) for a in outs], [_flat(a) for a in lses])
    x = matmul_residual(o, w_out, x, name="b_out")
    bufs = [jnp.concatenate([caches[g][:, t:], _dsa_kv_rows(qkv3, g, t)], axis=1) for g in range(DSA_GROUPS)]
    return x, bufs


def _mixer_c(x, h, hist, w_in, conv_w, ln_g, ln_b, w_out, shape):
    bsz, t, d = shape
    m = bsz * t
    u = matmul(h, w_in, tm=1024, tn=512, parts=2, epilogue=_ep_glu, out_shapes=_sds((m, d), F32),
               out_specs=_tile_spec(min(1024, m), 512), name="c_in")
    u3 = u.reshape(bsz, t, d)
    tiles = _conv_tiles(t)
    nb = tiles["nb"] or bsz
    a = causal_conv_fused(u3, hist, conv_w, taps=C_CONV, kind="ln", ln=(ln_g, ln_b),
                          tm=min(tiles["tm"], 128), tc=d, nb=nb)
    x = matmul_residual(_flat(a), w_out, x, name="c_out")
    return x, _last_rows(hist, u3, C_CONV - 1)


def _fox_project(h, w_qkv, w_f, b_f, qn, kn, shape):
    bsz, t, d = shape
    m = bsz * t
    qkv = matmul_qk_norm(h, w_qkv, qn, kn, head_dim=d // FOX_HEADS, name="d_qkv")
    tm = min(1024, m)
    logf = matmul(h, w_f, tm=tm, tn=LANES, epilogue=_ep_logf, out_shapes=_sds((m, LANES), F32),
                  out_specs=_tile_spec(tm, LANES), extras=(b_f,),
                  extra_specs=(pl.BlockSpec((1, LANES), lambda i, j: (0, 0)),), name="d_logf")
    logf = logf[:, :FOX_HEADS].reshape(bsz, t, FOX_HEADS)
    qkv3 = qkv.reshape(bsz, t, 3 * d)
    new_kv = qkv3[:, :, d:].reshape(bsz, t, 2, FOX_HEADS, d // FOX_HEADS)
    return qkv3, logf, new_kv


def _mixer_d_prompt(x, h, w_qkv, w_f, b_f, qn, kn, w_out, shape):
    bsz, t, d = shape
    qkv3, logf, new_kv = _fox_project(h, w_qkv, w_f, b_f, qn, kn, shape)
    f_cum = cumsum_lanes(logf.transpose(0, 2, 1).reshape(bsz * FOX_HEADS, t)).reshape(bsz * FOX_HEADS, 1, t)
    o = fox_prompt_attention(qkv3, f_cum)
    x = matmul_residual(_flat(o), w_out, x, name="d_out")
    return x, new_kv, logf


def _mixer_d_sample(x, h, kv_pool, logf_pool, page_table, w_qkv, w_f, b_f, qn, kn, w_out, shape):
    qkv3, logf, new_kv = _fox_project(h, w_qkv, w_f, b_f, qn, kn, shape)
    o = fox_sample_attention(qkv3, logf, kv_pool, logf_pool, page_table)
    x = matmul_residual(_flat(o), w_out, x, name="d_out")
    return x, new_kv, logf


def _ffn(x, h, hist, w_up, conv_w, w_down, shape):
    bsz, t, d = shape
    d_ff = w_down.shape[0]
    z = matmul_plain(h, w_up, name="ffn_up")
    z3 = z.reshape(bsz, t, 2 * d_ff)
    tiles = _conv_tiles(t)
    nb = tiles["nb"] or bsz
    tc = 512
    a = causal_conv_fused(z3, hist, conv_w, taps=FFN_CONV, kind="silu_mul", gate=z3, gate_col0=d_ff // tc,
                          src_cols=d_ff, tm=tiles["tm"], tc=tc, nb=nb)
    x = matmul_residual(_flat(a), w_down, x, name="ffn_down")
    return x, _last_rows(hist, z3[:, t - min(t, FFN_CONV - 1):, :d_ff], FFN_CONV - 1)


def kernel(x_prompt, x_sample, state_conv_a, cache_dsa_g0, cache_dsa_g1, cache_dsa_g2, state_conv_c,
           cache_fox_kv, cache_fox_logf, state_ffn_conv, page_table, p_prompt, p_sample,
           norm_mix, norm_ffn, w_a_in, conv_a_w, w_a_out, w_b_qkv, qn_b, kn_b, w_b_out,
           w_c_in, conv_c_w, ln_c_g, ln_c_b, w_c_out, w_d_qkvf, b_f, qn_d, kn_d, w_d_out,
           w_ffn_up, conv_ffn_w, w_ffn_down, norm_ple, w_ple_gate, w_ple_proj):
    depth = norm_mix.shape[0]
    d = x_prompt.shape[-1]
    d_ff = w_ffn_down.shape[1]
    shapes = (x_prompt.shape, x_sample.shape)
    xs = [_flat(x_prompt), _flat(x_sample)]
    ps = (p_prompt, p_sample)
    bf = lambda w: w.astype(BF16)

    w_d_qkv = bf(w_d_qkvf[:, :3 * d])
    n_f = w_d_qkvf.shape[1] - 3 * d
    w_d_f = bf(jnp.pad(w_d_qkvf[:, 3 * d:], ((0, 0), (0, LANES - n_f))))
    b_f_pad = jnp.pad(b_f.astype(F32), (0, LANES - n_f)).reshape(1, LANES)
    caches = (cache_dsa_g0, cache_dsa_g1, cache_dsa_g2)

    mixer_out = [dict(), dict()]
    ffn_hist = [[], []]
    for i in range(depth):
        kind = i % 4
        for grp in range(2):
            shape = shapes[grp]
            bsz, t, _ = shape
            x = xs[grp]
            out = mixer_out[grp]
            h = rmsnorm_bf16(x, norm_mix[i])
            if kind == 0:
                hist = jnp.zeros((bsz, A_CONV - 1, d), F32) if grp == 0 else state_conv_a
                x, out["conv_a"] = _mixer_a(x, h, hist, bf(w_a_in), conv_a_w, bf(w_a_out), shape)
            elif kind == 1:
                if grp == 0:
                    x, out["dsa"] = _mixer_b_prompt(x, h, bf(w_b_qkv), qn_b, kn_b, bf(w_b_out), shape)
                else:
                    x, out["dsa"] = _mixer_b_sample(x, h, caches, bf(w_b_qkv), qn_b, kn_b, bf(w_b_out), shape)
            elif kind == 2:
                hist = jnp.zeros((bsz, C_CONV - 1, d), F32) if grp == 0 else state_conv_c
                x, out["conv_c"] = _mixer_c(x, h, hist, bf(w_c_in), conv_c_w, ln_c_g, ln_c_b, bf(w_c_out), shape)
            else:
                if grp == 0:
                    x, out["fox_kv"], out["fox_logf"] = _mixer_d_prompt(
                        x, h, w_d_qkv, w_d_f, b_f_pad, qn_d, kn_d, bf(w_d_out), shape)
                else:
                    x, out["fox_kv"], out["fox_logf"] = _mixer_d_sample(
                        x, h, cache_fox_kv, cache_fox_logf, page_table,
                        w_d_qkv, w_d_f, b_f_pad, qn_d, kn_d, bf(w_d_out), shape)
            h = rmsnorm_bf16(x, norm_ffn[i])
            hist = jnp.zeros((bsz, FFN_CONV - 1, d_ff), F32) if grp == 0 else state_ffn_conv[i]
            x, fh = _ffn(x, h, hist, bf(w_ffn_up[i]), conv_ffn_w[i], bf(w_ffn_down[i]), shape)
            ffn_hist[grp].append(fh)
            h = rmsnorm_bf16(x, norm_ple[i])
            x = per_layer_embed(x, h, _flat(ps[grp][i]), bf(w_ple_gate[i]), bf(w_ple_proj[i]))
            xs[grp] = x

    mp, ms = mixer_out
    return (xs[0].reshape(shapes[0]), xs[1].reshape(shapes[1]),
            mp["conv_a"], ms["conv_a"],
            mp["dsa"][0], ms["dsa"][0], mp["dsa"][1], ms["dsa"][1], mp["dsa"][2], ms["dsa"][2],
            mp["conv_c"], ms["conv_c"],
            mp["fox_kv"], ms["fox_kv"], mp["fox_logf"], ms["fox_logf"],
            jnp.stack(ffn_hist[0]), jnp.stack(ffn_hist[1]))
```

```python
import functools

import jax
import jax.numpy as jnp
from jax import lax
from jax.experimental import pallas as pl
from jax.experimental.pallas import tpu as pltpu

F32 = jnp.float32
BF16 = jnp.bfloat16
EPS = 1e-6
NEG = -1e30

VMEM_LIMIT_BYTES = 56 * 1024 * 1024
LANES = 128
SUBLANES = 8
BF16_SUBLANES = 16

A_CONV = 3
C_CONV = 31
FFN_CONV = 3
DSA_WINDOWS = (128, 512, 2048)
DSA_DILATIONS = (1, 4, 16)
DSA_GROUPS = 3
DSA_HPG = 8
DSA_HEAD_DIM = 64
DSA_HEADS = DSA_GROUPS * DSA_HPG
DSA_GW = DSA_HPG * DSA_HEAD_DIM
DSA_KEYS = 128
FOX_HEADS = 16
Q_BLOCK = 128
FOX_PAGES_PER_STEP = 4


def _cp(*sem):
    return pltpu.CompilerParams(dimension_semantics=sem, vmem_limit_bytes=VMEM_LIMIT_BYTES)


def _sds(shape, dtype):
    return jax.ShapeDtypeStruct(shape, dtype)


def _rmsnorm(x, g):
    ms = jnp.mean(x * x, axis=-1, keepdims=True)
    return x * lax.rsqrt(ms + EPS) * g


def _mm_body(*refs, parts, n_extra, epilogue, norm):
    if norm:
        x_ref, g_ref, a_sc = refs[0], refs[1], refs[-1]
        refs = refs[2:-1]

        @pl.when(pl.program_id(1) == 0)
        def _():
            a_sc[...] = _rmsnorm(x_ref[...], g_ref[...]).astype(BF16)

        a = a_sc[...]
    else:
        a = refs[0][...]
        refs = refs[1:]
    accs = [jnp.dot(a, refs[p][...], preferred_element_type=F32) for p in range(parts)]
    extras = refs[parts:parts + n_extra]
    outs = refs[parts + n_extra:]
    epilogue(accs, extras, outs)


def matmul(a, w, *, tm, tn, epilogue, out_shapes, out_specs, parts=1, extras=(), extra_specs=(),
           norm_gain=None, name):
    m, k = a.shape
    np_ = w.shape[1] // parts
    nj = np_ // tn
    tm = min(tm, m)
    norm = norm_gain is not None
    in_specs = [pl.BlockSpec((tm, k), lambda i, j: (i, 0))]
    args = [a]
    if norm:
        in_specs.append(pl.BlockSpec((1, k), lambda i, j: (0, 0)))
        args.append(norm_gain.reshape(1, k))
    in_specs += [pl.BlockSpec((k, tn), functools.partial(lambda i, j, p: (0, p * nj + j), p=p))
                 for p in range(parts)]
    in_specs += list(extra_specs)
    return pl.pallas_call(
        functools.partial(_mm_body, parts=parts, n_extra=len(extras), epilogue=epilogue, norm=norm),
        out_shape=out_shapes,
        grid=(m // tm, nj),
        in_specs=in_specs,
        out_specs=out_specs,
        scratch_shapes=[pltpu.VMEM((tm, k), BF16)] if norm else [],
        compiler_params=_cp("parallel", "arbitrary"),
        name=name,
    )(*args, *([w] * parts), *extras)


def _tile_spec(tm, tn):
    return pl.BlockSpec((tm, tn), lambda i, j: (i, j))


def _ep_residual(accs, extras, outs):
    outs[0][...] = extras[0][...] + accs[0]


def matmul_residual(a, w, res, *, tm=1024, tn=512, name):
    m, n = a.shape[0], w.shape[1]
    tm = min(tm, m)
    return matmul(a, w, tm=tm, tn=tn, epilogue=_ep_residual, out_shapes=_sds((m, n), F32),
                  out_specs=_tile_spec(tm, tn), extras=(res,), extra_specs=(_tile_spec(tm, tn),), name=name)


def _ep_gate_split(accs, extras, outs):
    outs[0][...] = accs[0]
    outs[1][...] = accs[1] * accs[2]


def _ep_glu(accs, extras, outs):
    outs[0][...] = accs[0] * jax.nn.sigmoid(accs[1])


def _ep_qk_norm(accs, extras, outs, *, head_dim, blocks_per_part):
    j = pl.program_id(1)
    acc = accs[0]
    gain = extras[0][0]

    def store(cols, val):
        for o in outs:
            o[:, cols] = val.astype(o.dtype)

    @pl.when(j < 2 * blocks_per_part)
    def _():
        tn = acc.shape[1]
        for c in range(tn // LANES):
            cols = slice(c * LANES, (c + 1) * LANES)
            x = acc[:, cols]
            sq = x * x
            if head_dim == LANES:
                ms = jnp.sum(sq, axis=-1, keepdims=True) * (1.0 / head_dim)
            else:
                lane = lax.broadcasted_iota(jnp.int32, sq.shape, 1)
                first = lane < head_dim
                s_lo = jnp.sum(jnp.where(first, sq, 0.0), axis=-1, keepdims=True)
                s_hi = jnp.sum(jnp.where(first, 0.0, sq), axis=-1, keepdims=True)
                ms = jnp.where(first, s_lo, s_hi) * (1.0 / head_dim)
            store(cols, x * lax.rsqrt(ms + EPS) * gain[:, cols])

    @pl.when(j >= 2 * blocks_per_part)
    def _():
        store(slice(None), acc)


def matmul_qk_norm(x, norm_gain, w, qn, kn, *, head_dim, with_bf16=False, tm=1024, tn=512, name):
    m, n = x.shape[0], w.shape[1]
    tm = min(tm, m)
    bpp = n // 3 // tn
    reps = tn // head_dim
    gains = jnp.stack([jnp.tile(qn, reps), jnp.tile(kn, reps), jnp.ones((tn,), F32)]).reshape(3, 1, tn)
    gain_spec = pl.BlockSpec((1, 1, tn), lambda i, j: (j // bpp, 0, 0))
    shapes = [_sds((m, n), F32)] + ([_sds((m, n), BF16)] if with_bf16 else [])
    return matmul(x, w, tm=tm, tn=tn, norm_gain=norm_gain,
                  epilogue=functools.partial(_ep_qk_norm, head_dim=head_dim, blocks_per_part=bpp),
                  out_shapes=tuple(shapes), out_specs=tuple(_tile_spec(tm, tn) for _ in shapes),
                  extras=(gains,), extra_specs=(gain_spec,), name=name)


def _log_sigmoid(x):
    return jnp.minimum(x, 0.0) - jnp.log1p(jnp.exp(-jnp.abs(x)))


def _ep_logf(accs, extras, outs):
    outs[0][...] = _log_sigmoid(accs[0] + extras[0][...])


def _ple_body(x_ref, g_ref, wg_ref, p_ref, wp_ref, o_ref, h_sc, *, tn):
    j = pl.program_id(1)

    @pl.when(j == 0)
    def _():
        h_sc[...] = _rmsnorm(x_ref[...], g_ref[...]).astype(BF16)

    gate = jax.nn.sigmoid(jnp.dot(h_sc[...], wg_ref[...], preferred_element_type=F32))
    proj = jnp.dot(p_ref[...].astype(BF16), wp_ref[...], preferred_element_type=F32)
    o_ref[...] = x_ref[:, pl.ds(pl.multiple_of(j * tn, tn), tn)] + gate * proj


def per_layer_embed(x, norm_gain, p, w_gate, w_proj, *, tm=1024, tn=512):
    m, d = x.shape
    tm = min(tm, m)
    pd = p.shape[1]
    return pl.pallas_call(
        functools.partial(_ple_body, tn=tn),
        out_shape=_sds((m, d), F32),
        grid=(m // tm, d // tn),
        in_specs=[pl.BlockSpec((tm, d), lambda i, j: (i, 0)),
                  pl.BlockSpec((1, d), lambda i, j: (0, 0)),
                  pl.BlockSpec((d, tn), lambda i, j: (0, j)),
                  pl.BlockSpec((tm, pd), lambda i, j: (i, 0)),
                  pl.BlockSpec((pd, tn), lambda i, j: (0, j))],
        out_specs=_tile_spec(tm, tn),
        scratch_shapes=[pltpu.VMEM((tm, d), BF16)],
        compiler_params=_cp("parallel", "arbitrary"),
        name="per_layer_embed",
    )(x, norm_gain.reshape(1, d), w_gate, p, w_proj)


def _causal_taps(prev, cur, w, taps):
    p_rows, tm = prev.shape[0], cur.shape[0]
    ext = jnp.concatenate([prev, cur], axis=0)
    n = p_rows + tm
    acc = None
    for rot in range(SUBLANES):
        ks = [k for k in range(taps) if (p_rows - (taps - 1) + k) % SUBLANES == rot]
        if not ks:
            continue
        shifted = ext if rot == 0 else pltpu.roll(ext, n - rot, axis=0)
        for k in ks:
            base = p_rows - (taps - 1) + k - rot
            term = w[k:k + 1, :] * shifted[base:base + tm, :]
            acc = term if acc is None else acc + term
    return acc


def _conv_gate_kernel(cur_ref, prev_ref, hist_ref, w_ref, gate_ref, o_ref, *, taps, chunk):
    at_start = pl.program_id(1) == 0
    nb, tm, tc = cur_ref.shape
    for b in range(nb):
        for c0 in range(0, tc, chunk):
            cs = slice(c0, c0 + chunk)
            prev = jnp.where(at_start, hist_ref[b, :, cs], prev_ref[b, :, cs])
            y = _causal_taps(prev, cur_ref[b, :, cs], w_ref[:, cs], taps)
            o_ref[b, :, cs] = (gate_ref[b, :, cs] * y).astype(o_ref.dtype)


def _conv_ln_kernel(cur_ref, prev_ref, hist_ref, w_ref, lng_ref, lnb_ref, o_ref, c_sc, *, taps, chunk):
    at_start = pl.program_id(1) == 0
    nb, tm, tc = cur_ref.shape
    for b in range(nb):
        for c0 in range(0, tc, chunk):
            cs = slice(c0, c0 + chunk)
            prev = jnp.where(at_start, hist_ref[b, :, cs], prev_ref[b, :, cs])
            c_sc[:, cs] = _causal_taps(prev, cur_ref[b, :, cs], w_ref[:, cs], taps)
        c = c_sc[...]
        mu = jnp.mean(c, axis=-1, keepdims=True)
        xc = c - mu
        var = jnp.mean(xc * xc, axis=-1, keepdims=True)
        y = xc * lax.rsqrt(var + EPS) * lng_ref[...] + lnb_ref[...]
        o_ref[b] = (y * jax.nn.sigmoid(y)).astype(o_ref.dtype)


def _pad_hist(hist, p_rows):
    b, k1, c = hist.shape
    return jnp.concatenate([jnp.zeros((b, p_rows - k1, c), F32), hist.astype(F32)], axis=1)


def causal_conv_fused(src, hist, w, *, taps, gate=None, ln=None, tm, tc, nb):
    bsz, t, c = src.shape
    p_rows = SUBLANES if taps - 1 <= SUBLANES else 32
    hist_p = _pad_hist(hist, p_rows)
    tm = min(tm, t)
    n_t = t // tm
    chunk = min(tc, 256)
    if t >= p_rows:
        prev_src = src
        per = tm // p_rows
        prev_spec = pl.BlockSpec((nb, p_rows, tc), lambda b, i, j: (b, jnp.maximum(i * per - 1, 0), j))
    else:
        prev_src = hist_p
        prev_spec = pl.BlockSpec((nb, p_rows, tc), lambda b, i, j: (b, 0, j))
    cur_spec = pl.BlockSpec((nb, tm, tc), lambda b, i, j: (b, i, j))
    hist_spec = pl.BlockSpec((nb, p_rows, tc), lambda b, i, j: (b, 0, j))
    w_spec = pl.BlockSpec((taps, tc), lambda b, i, j: (0, j))
    grid = (bsz // nb, n_t, c // tc)
    if ln is not None:
        lng, lnb = ln
        vec_spec = pl.BlockSpec((1, tc), lambda b, i, j: (0, j))
        return pl.pallas_call(
            functools.partial(_conv_ln_kernel, taps=taps, chunk=chunk),
            out_shape=_sds((bsz, t, c), BF16), grid=grid,
            in_specs=[cur_spec, prev_spec, hist_spec, w_spec, vec_spec, vec_spec],
            out_specs=cur_spec,
            scratch_shapes=[pltpu.VMEM((tm, tc), F32)],
            compiler_params=_cp("parallel", "arbitrary", "arbitrary"),
            name="conv_ln",
        )(src, prev_src, hist_p, w, lng.reshape(1, c), lnb.reshape(1, c))
    return pl.pallas_call(
        functools.partial(_conv_gate_kernel, taps=taps, chunk=chunk),
        out_shape=_sds((bsz, t, c), BF16), grid=grid,
        in_specs=[cur_spec, prev_spec, hist_spec, w_spec, cur_spec],
        out_specs=cur_spec,
        compiler_params=_cp("parallel", "arbitrary", "arbitrary"),
        name="conv_gate",
    )(src, prev_src, hist_p, w, gate)


def _ffn_kernel(x_ref, halo_ref, hist_ref, gain_ref, wg_ref, wu_ref, cw_ref, wd_ref, o_ref, gh_ref, h_sc,
                *, n_seq, tiles_per_seq, sub):
    i, c = pl.program_id(0), pl.program_id(1)
    tm = x_ref.shape[0]
    halo = h_sc.shape[0] - tm

    @pl.when(c == 0)
    def _():
        h_sc[halo:, :] = _rmsnorm(x_ref[...], gain_ref[...]).astype(BF16)
        if halo:
            h_sc[:halo, :] = _rmsnorm(halo_ref[...], gain_ref[...]).astype(BF16)

    d = None
    for c0 in range(0, wg_ref.shape[1], sub):
        cs = slice(c0, c0 + sub)
        g_ext = jnp.dot(h_sc[...], wg_ref[:, cs], preferred_element_type=F32)
        u = jnp.dot(h_sc[halo:, :], wu_ref[:, cs], preferred_element_type=F32)
        cw = cw_ref[:, cs]
        if halo:
            at_start = (i % tiles_per_seq) == 0
            prev = jnp.where(at_start, hist_ref[0, :, cs], g_ext[halo - SUBLANES:halo])
            g = g_ext[halo:]
            y = _causal_taps(prev, g, cw, FFN_CONV)
            gh_ref[0, :, cs] = g[tm - SUBLANES:]
        else:
            rows = tm // n_seq
            ys = []
            for b in range(n_seq):
                g = g_ext[b * rows:(b + 1) * rows]
                ys.append(_causal_taps(hist_ref[b, :, cs], g, cw, FFN_CONV))
                gh_ref[b, :, cs] = g[rows - SUBLANES:]
            y = jnp.concatenate(ys, axis=0)
        a = (y * jax.nn.sigmoid(y) * u).astype(BF16)
        part = jnp.dot(a, wd_ref[cs, :], preferred_element_type=F32)
        d = part if d is None else d + part

    @pl.when(c == 0)
    def _():
        o_ref[...] = x_ref[...] + d

    @pl.when(c > 0)
    def _():
        o_ref[...] += d


def ffn_fused(x, norm_gain, hist, w_up, conv_w, w_down, shape, *, tm=512, tf=512):
    bsz, t, d = shape
    m = bsz * t
    d_ff = w_down.shape[0]
    nc = d_ff // tf
    hist_p = _pad_hist(hist, SUBLANES)
    if t >= tm:
        n_seq, tiles_per_seq, halo = 1, t // tm, BF16_SUBLANES
        seq_blk = lambda i: i // tiles_per_seq
    else:
        tm, n_seq, tiles_per_seq, halo = m, bsz, 1, 0
        seq_blk = lambda i: 0
    per = tm // BF16_SUBLANES
    hist_spec = pl.BlockSpec((n_seq, SUBLANES, tf), lambda i, c: (seq_blk(i), 0, c))
    x_new, g_tail = pl.pallas_call(
        functools.partial(_ffn_kernel, n_seq=n_seq, tiles_per_seq=tiles_per_seq, sub=min(tf, 256)),
        out_shape=(_sds((m, d), F32), _sds((bsz * tiles_per_seq, SUBLANES, d_ff), F32)),
        grid=(m // tm, nc),
        in_specs=[pl.BlockSpec((tm, d), lambda i, c: (i, 0)),
                  pl.BlockSpec((BF16_SUBLANES, d), lambda i, c: (jnp.maximum(i * per - 1, 0), 0)),
                  hist_spec,
                  pl.BlockSpec((1, d), lambda i, c: (0, 0)),
                  pl.BlockSpec((d, tf), lambda i, c: (0, c)),
                  pl.BlockSpec((d, tf), lambda i, c: (0, nc + c)),
                  pl.BlockSpec((FFN_CONV, tf), lambda i, c: (0, c)),
                  pl.BlockSpec((tf, d), lambda i, c: (c, 0))],
        out_specs=(pl.BlockSpec((tm, d), lambda i, c: (i, 0)),
                   pl.BlockSpec((n_seq, SUBLANES, tf), lambda i, c: (i, 0, c))),
        scratch_shapes=[pltpu.VMEM((halo + tm, d), BF16)],
        compiler_params=_cp("parallel", "arbitrary"),
        name="ffn",
    )(x, x, hist_p, norm_gain.reshape(1, d), w_up, w_up, conv_w, w_down)
    return x_new, g_tail.reshape(bsz, tiles_per_seq, SUBLANES, d_ff)[:, -1]


def _alibi_slope(head):
    return 2.0 ** (-8.0 * (head + 1) / DSA_HEADS)


def _dsa_prompt_kernel(q_ref, kp_ref, kc_ref, vp_ref, vc_ref, o_ref, lse_ref, *, dil, group):
    i = pl.program_id(2)
    tq = Q_BLOCK
    q = q_ref[0].astype(BF16)
    k = jnp.concatenate([kp_ref[0], kc_ref[0]], axis=0).astype(BF16)
    v = jnp.concatenate([vp_ref[0], vc_ref[0]], axis=0).astype(BF16)
    row = lax.broadcasted_iota(jnp.int32, (tq, 2 * tq), 0)
    col = lax.broadcasted_iota(jnp.int32, (tq, 2 * tq), 1)
    dist = row + tq - col
    valid = (dist >= 0) & (dist <= DSA_KEYS) & ((col >= tq) | (i > 0))
    distf = dist.astype(F32) * float(dil)
    for h in range(DSA_HPG):
        sl = slice(h * DSA_HEAD_DIM, (h + 1) * DSA_HEAD_DIM)
        s = lax.dot_general(q[:, sl], k[:, sl], (((1,), (1,)), ((), ())), preferred_element_type=F32)
        s = s * (DSA_HEAD_DIM ** -0.5) - _alibi_slope(group * DSA_HPG + h) * distf
        s = jnp.where(valid, s, NEG)
        m = jnp.max(s, axis=-1, keepdims=True)
        p = jnp.exp(s - m)
        l = jnp.sum(p, axis=-1, keepdims=True)
        o = jnp.dot(p.astype(BF16), v[:, sl], preferred_element_type=F32) / l
        o_ref[0, :, sl] = o
        lse_ref[0, :, sl] = jnp.broadcast_to(m + jnp.log(l), (tq, DSA_HEAD_DIM))


def dsa_prompt_group(qkv, group):
    bsz, t, n = qkv.shape
    dil = DSA_DILATIONS[group]
    s_len = t // dil
    nblk = n // DSA_GW
    view = qkv.reshape(bsz, s_len, dil * n)
    tq = Q_BLOCK

    def spec(part, prev):
        if prev:
            return pl.BlockSpec((1, tq, DSA_GW),
                                lambda b, r, i: (b, jnp.maximum(i - 1, 0), r * nblk + part * DSA_GROUPS + group))
        return pl.BlockSpec((1, tq, DSA_GW), lambda b, r, i: (b, i, r * nblk + part * DSA_GROUPS + group))

    out_spec = pl.BlockSpec((1, tq, DSA_GW), lambda b, r, i: (b, i, r))
    o, lse = pl.pallas_call(
        functools.partial(_dsa_prompt_kernel, dil=dil, group=group),
        out_shape=(_sds((bsz, s_len, dil * DSA_GW), F32), _sds((bsz, s_len, dil * DSA_GW), F32)),
        grid=(bsz, dil, s_len // tq),
        in_specs=[spec(0, False), spec(1, True), spec(1, False), spec(2, True), spec(2, False)],
        out_specs=(out_spec, out_spec),
        compiler_params=_cp("parallel", "parallel", "arbitrary"),
        name=f"dsa_prompt_g{group}",
    )(view, view, view, view, view)
    return o.reshape(bsz, t, DSA_GW), lse.reshape(bsz, t, DSA_GW)


def _dsa_sample_kernel(q_ref, kc_ref, vc_ref, kn_ref, vn_ref, o_ref, lse_ref, *, cache_len, dil, window, group):
    t = q_ref.shape[1]
    pad = jnp.zeros((LANES - t, DSA_GW), F32)
    q = q_ref[0].astype(BF16)
    k = jnp.concatenate([kc_ref[0], kn_ref[0], pad], axis=0).astype(BF16)
    v = jnp.concatenate([vc_ref[0], vn_ref[0], pad], axis=0).astype(BF16)
    rows = cache_len + LANES
    qi = lax.broadcasted_iota(jnp.int32, (t, rows), 0)
    col = lax.broadcasted_iota(jnp.int32, (t, rows), 1)
    dist = cache_len + qi - col
    valid = (dist >= 0) & (dist <= window) & ((dist & (dil - 1)) == 0)
    distf = dist.astype(F32)
    for h in range(DSA_HPG):
        sl = slice(h * DSA_HEAD_DIM, (h + 1) * DSA_HEAD_DIM)
        s = lax.dot_general(q[:, sl], k[:, sl], (((1,), (1,)), ((), ())), preferred_element_type=F32)
        s = s * (DSA_HEAD_DIM ** -0.5) - _alibi_slope(group * DSA_HPG + h) * distf
        s = jnp.where(valid, s, NEG)
        m = jnp.max(s, axis=-1, keepdims=True)
        p = jnp.exp(s - m)
        l = jnp.sum(p, axis=-1, keepdims=True)
        o = jnp.dot(p.astype(BF16), v[:, sl], preferred_element_type=F32) / l
        o_ref[0, :, sl] = o
        lse_ref[0, :, sl] = jnp.broadcast_to(m + jnp.log(l), (t, DSA_HEAD_DIM))


def dsa_sample_group(qkv, cache, group):
    bsz, t, _ = qkv.shape
    cache_len = cache.shape[1]
    cview = cache.reshape(bsz, cache_len, 2 * DSA_GW)
    new_spec = lambda part: pl.BlockSpec((1, t, DSA_GW), lambda b: (b, 0, part * DSA_GROUPS + group))
    cache_spec = lambda part: pl.BlockSpec((1, cache_len, DSA_GW), lambda b: (b, 0, part))
    out_spec = pl.BlockSpec((1, t, DSA_GW), lambda b: (b, 0, 0))
    return pl.pallas_call(
        functools.partial(_dsa_sample_kernel, cache_len=cache_len, dil=DSA_DILATIONS[group],
                          window=DSA_WINDOWS[group], group=group),
        out_shape=(_sds((bsz, t, DSA_GW), F32), _sds((bsz, t, DSA_GW), F32)),
        grid=(bsz,),
        in_specs=[new_spec(0), cache_spec(0), cache_spec(1), new_spec(1), new_spec(2)],
        out_specs=(out_spec, out_spec),
        compiler_params=_cp("parallel"),
        name=f"dsa_sample_g{group}",
    )(qkv, cview, cview, qkv, qkv)


def _dsa_combine_kernel(o0, o1, o2, l0, l1, l2, out_ref):
    ls = (l0[...], l1[...], l2[...])
    m = jnp.maximum(jnp.maximum(ls[0], ls[1]), ls[2])
    es = [jnp.exp(x - m) for x in ls]
    den = es[0] + es[1] + es[2]
    num = es[0] * o0[...] + es[1] * o1[...] + es[2] * o2[...]
    out_ref[...] = (num / den).astype(out_ref.dtype)


def dsa_combine(outs, lses):
    m, n = outs[0].shape
    tm = min(m, 1024)
    spec = pl.BlockSpec((tm, n), lambda i: (i, 0))
    return pl.pallas_call(
        _dsa_combine_kernel, out_shape=_sds((m, n), BF16), grid=(m // tm,),
        in_specs=[spec] * 6, out_specs=spec, compiler_params=_cp("parallel"), name="dsa_combine",
    )(*outs, *lses)


def _split_bf16(x):
    hi = x.astype(BF16)
    lo = (x - hi.astype(F32)).astype(BF16)
    return hi, lo


def _tri_sum(x, tri):
    hi, lo = _split_bf16(x)
    return (jnp.dot(hi, tri, preferred_element_type=F32) + jnp.dot(lo, tri, preferred_element_type=F32))


def _cumsum_kernel(x_ref, o_ref, *, tn):
    j = pl.program_id(0)
    t = x_ref.shape[1]
    src = lax.broadcasted_iota(jnp.int32, (t, tn), 0)
    dst = lax.broadcasted_iota(jnp.int32, (t, tn), 1) + j * tn
    o_ref[...] = _tri_sum(x_ref[...], jnp.where(src <= dst, 1.0, 0.0).astype(BF16))


def cumsum_lanes(x):
    r, t = x.shape
    tn = min(t, 512)
    return pl.pallas_call(
        functools.partial(_cumsum_kernel, tn=tn), out_shape=_sds((r, t), F32), grid=(t // tn,),
        in_specs=[pl.BlockSpec((r, t), lambda j: (0, 0))],
        out_specs=pl.BlockSpec((r, tn), lambda j: (0, j)),
        compiler_params=_cp("arbitrary"), name="fox_cumsum",
    )(x)


def _fox_prompt_kernel(q_ref, k_ref, v_ref, fk_ref, o_ref, *, scale):
    i = pl.program_id(2)
    tq, hd = q_ref.shape[1], q_ref.shape[2]
    q = q_ref[0]

    def chunk(c, carry, diagonal):
        m, l, acc = carry
        start = pl.multiple_of(c * tq, tq)
        k = k_ref[0, pl.ds(start, tq), :]
        v = v_ref[0, pl.ds(start, tq), :]
        s = lax.dot_general(q, k, (((1,), (1,)), ((), ())), preferred_element_type=F32) * scale
        s = s - fk_ref[0, pl.ds(c, 1), :]
        if diagonal:
            row = lax.broadcasted_iota(jnp.int32, (tq, tq), 0)
            col = lax.broadcasted_iota(jnp.int32, (tq, tq), 1)
            s = jnp.where(col <= row, s, NEG)
        m_new = jnp.maximum(m, jnp.max(s, axis=-1, keepdims=True))
        alpha = jnp.exp(m - m_new)
        p = jnp.exp(s - m_new)
        l = l * alpha + jnp.sum(p, axis=-1, keepdims=True)
        acc = acc * alpha + jnp.dot(p.astype(BF16), v, preferred_element_type=F32)
        return m_new, l, acc

    init = (jnp.full((tq, 1), NEG, F32), jnp.zeros((tq, 1), F32), jnp.zeros((tq, hd), F32))
    carry = lax.fori_loop(0, i, lambda c, cr: chunk(c, cr, False), init)
    _, l, acc = chunk(i, carry, True)
    o_ref[0] = (acc / l).astype(o_ref.dtype)


def fox_prompt_attention(qkv, f_cum, *, tq=256):
    bsz, t, n = qkv.shape
    d = n // 3
    hd = d // FOX_HEADS
    return pl.pallas_call(
        functools.partial(_fox_prompt_kernel, scale=hd ** -0.5),
        out_shape=_sds((bsz, t, d), BF16),
        grid=(bsz, FOX_HEADS, t // tq),
        in_specs=[pl.BlockSpec((1, tq, hd), lambda b, h, i: (b, i, h)),
                  pl.BlockSpec((1, t, hd), lambda b, h, i: (b, 0, FOX_HEADS + h)),
                  pl.BlockSpec((1, t, hd), lambda b, h, i: (b, 0, 2 * FOX_HEADS + h)),
                  pl.BlockSpec((1, t // tq, tq), lambda b, h, i: (b * FOX_HEADS + h, 0, 0))],
        out_specs=pl.BlockSpec((1, tq, hd), lambda b, h, i: (b, i, h)),
        compiler_params=_cp("parallel", "parallel", "arbitrary"),
        name="fox_prompt_attn",
    )(qkv, qkv, qkv, f_cum)


def _fox_sample_kernel(pt_ref, q_ref, nkv_ref, nlf_ref, *rest, scale, t_new, heads, pps):
    kv_refs, lf_refs = rest[:pps], rest[pps:2 * pps]
    o_ref, m_sc, l_sc, acc_sc, suf_sc = rest[2 * pps:]
    j = pl.program_id(1)
    page = nlf_ref.shape[2]
    hd = q_ref.shape[2] // heads
    r = heads * t_new
    src = lax.broadcasted_iota(jnp.int32, (page, page), 0)
    dst = lax.broadcasted_iota(jnp.int32, (page, page), 1)

    def attend(kv_ref, bias, valid, first):
        s_rows = []
        for h in range(heads):
            q_h = q_ref[0, :, h * hd:(h + 1) * hd].astype(BF16)
            k_h = kv_ref[0, pl.ds(h, page, stride=2 * heads), :].astype(BF16)
            s_h = lax.dot_general(q_h, k_h, (((1,), (1,)), ((), ())), preferred_element_type=F32)
            s_rows.append(s_h * scale + bias[h:h + 1, :])
        s = jnp.concatenate(s_rows, axis=0)
        if valid is not None:
            s = jnp.where(valid, s, NEG)
        m_cur = jnp.max(s, axis=-1, keepdims=True)
        m_new = m_cur if first else jnp.maximum(m_sc[...], m_cur)
        p = jnp.exp(s - m_new)
        pv = jnp.concatenate(
            [jnp.dot(p[h * t_new:(h + 1) * t_new].astype(BF16),
                     kv_ref[0, pl.ds(heads + h, page, stride=2 * heads), :].astype(BF16),
                     preferred_element_type=F32) for h in range(heads)], axis=0)
        l_cur = jnp.sum(p, axis=-1, keepdims=True)
        if first:
            l_sc[...] = l_cur
            acc_sc[...] = pv
        else:
            c = jnp.exp(m_sc[...] - m_new)
            l_sc[...] = l_sc[...] * c + l_cur
            acc_sc[...] = acc_sc[...] * c + pv
        m_sc[...] = m_new

    @pl.when(j == 0)
    def _():
        fq = _tri_sum(nlf_ref[0], jnp.where(src <= dst, 1.0, 0.0).astype(BF16))
        tq = lax.broadcasted_iota(jnp.int32, (r, page), 0) % t_new
        tk = lax.broadcasted_iota(jnp.int32, (r, page), 1)
        attend(nkv_ref, -fq, tk <= tq, True)
        suf_sc[...] = jnp.zeros_like(suf_sc)

    @pl.when(j > 0)
    def _():
        later_mask = jnp.where(src > dst, 1.0, 0.0).astype(BF16)
        for u in range(pps):
            lf = lf_refs[u][0]
            suf = suf_sc[...]
            attend(kv_refs[u], _tri_sum(lf, later_mask) + suf, None, False)
            suf_sc[...] = suf + jnp.sum(lf, axis=-1, keepdims=True)

    @pl.when(j == pl.num_programs(1) - 1)
    def _():
        inv = 1.0 / l_sc[...]
        for h in range(heads):
            rs = slice(h * t_new, (h + 1) * t_new)
            o_ref[0, :, h * hd:(h + 1) * hd] = (acc_sc[rs, :] * inv[rs]).astype(o_ref.dtype)


def fox_sample_attention(qkv, logf, kv_pool, logf_pool, page_table):
    bsz, t, n = qkv.shape
    d = n // 3
    heads = FOX_HEADS
    hd = d // heads
    n_pool, page = kv_pool.shape[0], kv_pool.shape[1]
    n_pages = page_table.shape[1]
    pps = FOX_PAGES_PER_STEP
    n_steps = n_pages // pps
    rows = page * 2 * heads
    pool = kv_pool.reshape(n_pool, rows, hd)
    pool_lf = logf_pool.transpose(0, 2, 1)
    new_kv = jnp.pad(qkv[:, :, d:].reshape(bsz, t, 2 * heads, hd), ((0, 0), (0, page - t), (0, 0), (0, 0)))
    new_kv = new_kv.reshape(bsz, rows, hd)
    new_lf = jnp.pad(logf.transpose(0, 2, 1), ((0, 0), (0, 0), (0, page - t)))
    pt = page_table.reshape(-1)

    def page_idx(u):
        def index_map(b, j, pt_ref):
            return (pt_ref[b * n_pages + n_pages - 1 - (jnp.maximum(j, 1) - 1) * pps - u], 0, 0)
        return index_map

    own = lambda b, j, pt_ref: (b, 0, 0)
    grid_spec = pltpu.PrefetchScalarGridSpec(
        num_scalar_prefetch=1,
        grid=(bsz, n_steps + 1),
        in_specs=[pl.BlockSpec((1, t, d), own),
                  pl.BlockSpec((1, rows, hd), own),
                  pl.BlockSpec((1, heads, page), own)]
                 + [pl.BlockSpec((1, rows, hd), page_idx(u)) for u in range(pps)]
                 + [pl.BlockSpec((1, heads, page), page_idx(u)) for u in range(pps)],
        out_specs=pl.BlockSpec((1, t, d), own),
        scratch_shapes=[pltpu.VMEM((heads * t, 1), F32), pltpu.VMEM((heads * t, 1), F32),
                        pltpu.VMEM((heads * t, hd), F32), pltpu.VMEM((heads, 1), F32)],
    )
    return pl.pallas_call(
        functools.partial(_fox_sample_kernel, scale=hd ** -0.5, t_new=t, heads=heads, pps=pps),
        out_shape=_sds((bsz, t, d), BF16),
        grid_spec=grid_spec,
        compiler_params=_cp("parallel", "arbitrary"),
        name="fox_sample_attn",
    )(pt, qkv, new_kv, new_lf, *([pool] * pps), *([pool_lf] * pps))


def _flat(x):
    return x.reshape(-1, x.shape[-1])


def _last_rows(hist, new, n):
    t = new.shape[1]
    if t >= n:
        return new[:, t - n:]
    return jnp.concatenate([hist.astype(new.dtype)[:, t:], new], axis=1)


def _conv_tiles(bsz, t):
    return dict(tm=256, nb=1) if t >= 256 else dict(tm=t, nb=bsz)


def _mixer_a(x, gain, hist, w_in, conv_w, w_out, shape):
    bsz, t, d = shape
    m = bsz * t
    tm = min(1024, m)
    gb, u = matmul(x, w_in, tm=tm, tn=512, parts=3, epilogue=_ep_gate_split, norm_gain=gain,
                   out_shapes=(_sds((m, d), F32), _sds((m, d), F32)),
                   out_specs=(_tile_spec(tm, 512), _tile_spec(tm, 512)), name="a_in")
    u3 = u.reshape(bsz, t, d)
    a = causal_conv_fused(u3, hist, conv_w, taps=A_CONV, gate=gb.reshape(bsz, t, d), tc=512,
                          **_conv_tiles(bsz, t))
    x = matmul_residual(_flat(a), w_out, x, name="a_out")
    return x, _last_rows(hist, u3, A_CONV - 1)


def _dsa_kv_rows(qkv3, group, n_last):
    qkv3 = qkv3[:, qkv3.shape[1] - n_last:]
    bsz, t, _ = qkv3.shape
    k0 = DSA_GROUPS * DSA_GW + group * DSA_GW
    v0 = 2 * DSA_GROUPS * DSA_GW + group * DSA_GW
    k = qkv3[:, :, k0:k0 + DSA_GW].reshape(bsz, t, DSA_HPG, DSA_HEAD_DIM)
    v = qkv3[:, :, v0:v0 + DSA_GW].reshape(bsz, t, DSA_HPG, DSA_HEAD_DIM)
    return jnp.stack([k, v], axis=2)


def _mixer_b(x, gain, caches, w_qkv, qn, kn, w_out, shape):
    bsz, t, d = shape
    (qkv,) = matmul_qk_norm(x, gain, w_qkv, qn, kn, head_dim=DSA_HEAD_DIM, name="b_qkv")
    qkv3 = qkv.reshape(bsz, t, -1)
    if caches is None:
        outs, lses = zip(*[dsa_prompt_group(qkv3, g) for g in range(DSA_GROUPS)])
        bufs = [_dsa_kv_rows(qkv3, g, min(DSA_WINDOWS[g], t)) for g in range(DSA_GROUPS)]
    else:
        outs, lses = zip(*[dsa_sample_group(qkv3, caches[g], g) for g in range(DSA_GROUPS)])
        bufs = [jnp.concatenate([caches[g][:, t:], _dsa_kv_rows(qkv3, g, t)], axis=1) for g in range(DSA_GROUPS)]
    o = dsa_combine([_flat(a) for a in outs], [_flat(a) for a in lses])
    x = matmul_residual(o, w_out, x, name="b_out")
    return x, bufs


def _mixer_c(x, gain, hist, w_in, conv_w, ln_g, ln_b, w_out, shape):
    bsz, t, d = shape
    m = bsz * t
    tm = min(1024, m)
    u = matmul(x, w_in, tm=tm, tn=512, parts=2, epilogue=_ep_glu, norm_gain=gain,
               out_shapes=_sds((m, d), F32), out_specs=_tile_spec(tm, 512), name="c_in")
    u3 = u.reshape(bsz, t, d)
    tiles = _conv_tiles(bsz, t)
    a = causal_conv_fused(u3, hist, conv_w, taps=C_CONV, ln=(ln_g, ln_b), tm=min(tiles["tm"], 128), tc=d,
                          nb=tiles["nb"])
    x = matmul_residual(_flat(a), w_out, x, name="c_out")
    return x, _last_rows(hist, u3, C_CONV - 1)


def _mixer_d(x, gain, paged, w_qkv, w_f, b_f, qn, kn, w_out, shape):
    bsz, t, d = shape
    m = bsz * t
    tm = min(1024, m)
    qkv, qkv_bf = matmul_qk_norm(x, gain, w_qkv, qn, kn, head_dim=d // FOX_HEADS, with_bf16=True, name="d_qkv")
    logf = matmul(x, w_f, tm=tm, tn=LANES, epilogue=_ep_logf, norm_gain=gain, out_shapes=_sds((m, LANES), F32),
                  out_specs=_tile_spec(tm, LANES), extras=(b_f,),
                  extra_specs=(pl.BlockSpec((1, LANES), lambda i, j: (0, 0)),), name="d_logf")
    logf = logf[:, :FOX_HEADS].reshape(bsz, t, FOX_HEADS)
    qkv3 = qkv.reshape(bsz, t, 3 * d)
    new_kv = qkv3[:, :, d:].reshape(bsz, t, 2, FOX_HEADS, d // FOX_HEADS)
    if paged is None:
        tq = 256
        f_cum = cumsum_lanes(logf.transpose(0, 2, 1).reshape(bsz * FOX_HEADS, t))
        o = fox_prompt_attention(qkv_bf.reshape(bsz, t, 3 * d), f_cum.reshape(bsz * FOX_HEADS, t // tq, tq), tq=tq)
    else:
        o = fox_sample_attention(qkv3, logf, *paged)
    x = matmul_residual(_flat(o), w_out, x, name="d_out")
    return x, new_kv, logf


def kernel(x_prompt, x_sample, state_conv_a, cache_dsa_g0, cache_dsa_g1, cache_dsa_g2, state_conv_c,
           cache_fox_kv, cache_fox_logf, state_ffn_conv, page_table, p_prompt, p_sample,
           norm_mix, norm_ffn, w_a_in, conv_a_w, w_a_out, w_b_qkv, qn_b, kn_b, w_b_out,
           w_c_in, conv_c_w, ln_c_g, ln_c_b, w_c_out, w_d_qkvf, b_f, qn_d, kn_d, w_d_out,
           w_ffn_up, conv_ffn_w, w_ffn_down, norm_ple, w_ple_gate, w_ple_proj):
    depth = norm_mix.shape[0]
    d = x_prompt.shape[-1]
    d_ff = w_ffn_down.shape[1]
    shapes = (x_prompt.shape, x_sample.shape)
    xs = [_flat(x_prompt), _flat(x_sample)]
    ps = (p_prompt, p_sample)
    bf = lambda w: w.astype(BF16)

    w_d_qkv = bf(w_d_qkvf[:, :3 * d])
    n_f = w_d_qkvf.shape[1] - 3 * d
    w_d_f = bf(jnp.pad(w_d_qkvf[:, 3 * d:], ((0, 0), (0, LANES - n_f))))
    b_f_pad = jnp.pad(b_f.astype(F32), (0, LANES - n_f)).reshape(1, LANES)
    caches = (cache_dsa_g0, cache_dsa_g1, cache_dsa_g2)
    paged = (cache_fox_kv, cache_fox_logf, page_table)

    mixer_out = [dict(), dict()]
    ffn_hist = [[], []]
    for i in range(depth):
        kind = i % 4
        for grp in range(2):
            shape = shapes[grp]
            bsz, t, _ = shape
            x = xs[grp]
            out = mixer_out[grp]
            gain = norm_mix[i]
            if kind == 0:
                hist = jnp.zeros((bsz, A_CONV - 1, d), F32) if grp == 0 else state_conv_a
                x, out["conv_a"] = _mixer_a(x, gain, hist, bf(w_a_in), conv_a_w, bf(w_a_out), shape)
            elif kind == 1:
                x, out["dsa"] = _mixer_b(x, gain, None if grp == 0 else caches, bf(w_b_qkv), qn_b, kn_b,
                                         bf(w_b_out), shape)
            elif kind == 2:
                hist = jnp.zeros((bsz, C_CONV - 1, d), F32) if grp == 0 else state_conv_c
                x, out["conv_c"] = _mixer_c(x, gain, hist, bf(w_c_in), conv_c_w, ln_c_g, ln_c_b, bf(w_c_out), shape)
            else:
                x, out["fox_kv"], out["fox_logf"] = _mixer_d(
                    x, gain, None if grp == 0 else paged, w_d_qkv, w_d_f, b_f_pad, qn_d, kn_d, bf(w_d_out), shape)
            hist = jnp.zeros((bsz, FFN_CONV - 1, d_ff), F32) if grp == 0 else state_ffn_conv[i]
            x, g_last = ffn_fused(x, norm_ffn[i], hist, bf(w_ffn_up[i]), conv_ffn_w[i], bf(w_ffn_down[i]), shape)
            ffn_hist[grp].append(_last_rows(hist, g_last[:, SUBLANES - min(t, SUBLANES):], FFN_CONV - 1))
            xs[grp] = per_layer_embed(x, norm_ple[i], _flat(ps[grp][i]), bf(w_ple_gate[i]), bf(w_ple_proj[i]))

    mp, ms = mixer_out
    return (xs[0].reshape(shapes[0]), xs[1].reshape(shapes[1]),
            mp["conv_a"], ms["conv_a"],
            mp["dsa"][0], ms["dsa"][0], mp["dsa"][1], ms["dsa"][1], mp["dsa"][2], ms["dsa"][2],
            mp["conv_c"], ms["conv_c"],
            mp["fox_kv"], ms["fox_kv"], mp["fox_logf"], ms["fox_logf"],
            jnp.stack(ffn_hist[0]), jnp.stack(ffn_hist[1]))
```

```python
import functools

import jax
import jax.numpy as jnp
from jax import lax
from jax.experimental import pallas as pl
from jax.experimental.pallas import tpu as pltpu

F32 = jnp.float32
BF16 = jnp.bfloat16
EPS = 1e-6
NEG = -1e30

VMEM_LIMIT_BYTES = 56 * 1024 * 1024
LANES = 128
SUBLANES = 8
BF16_SUBLANES = 16

A_CONV = 3
C_CONV = 31
FFN_CONV = 3
DSA_WINDOWS = (128, 512, 2048)
DSA_DILATIONS = (1, 4, 16)
DSA_GROUPS = 3
DSA_HPG = 8
DSA_HEAD_DIM = 64
DSA_HEADS = DSA_GROUPS * DSA_HPG
DSA_GW = DSA_HPG * DSA_HEAD_DIM
DSA_KEYS = 128
FOX_HEADS = 16
Q_BLOCK = 128
FOX_PAGES_PER_STEP = 4
FOX_PITCH_PAD = 8


def _cp(*sem):
    return pltpu.CompilerParams(dimension_semantics=sem, vmem_limit_bytes=VMEM_LIMIT_BYTES)


def _sds(shape, dtype):
    return jax.ShapeDtypeStruct(shape, dtype)


def _rmsnorm(x, g):
    ms = jnp.mean(x * x, axis=-1, keepdims=True)
    return x * lax.rsqrt(ms + EPS) * g


def _mm_body(*refs, parts, n_extra, epilogue, norm):
    if norm:
        x_ref, g_ref, a_sc = refs[0], refs[1], refs[-1]
        refs = refs[2:-1]

        @pl.when(pl.program_id(1) == 0)
        def _():
            a_sc[...] = _rmsnorm(x_ref[...], g_ref[...]).astype(BF16)

        a = a_sc[...]
    else:
        a = refs[0][...]
        refs = refs[1:]
    accs = [jnp.dot(a, refs[p][...], preferred_element_type=F32) for p in range(parts)]
    extras = refs[parts:parts + n_extra]
    outs = refs[parts + n_extra:]
    epilogue(accs, extras, outs)


def matmul(a, w, *, tm, tn, epilogue, out_shapes, out_specs, parts=1, extras=(), extra_specs=(),
           norm_gain=None, name):
    m, k = a.shape
    np_ = w.shape[1] // parts
    nj = np_ // tn
    tm = min(tm, m)
    norm = norm_gain is not None
    in_specs = [pl.BlockSpec((tm, k), lambda i, j: (i, 0))]
    args = [a]
    if norm:
        in_specs.append(pl.BlockSpec((1, k), lambda i, j: (0, 0)))
        args.append(norm_gain.reshape(1, k))
    in_specs += [pl.BlockSpec((k, tn), functools.partial(lambda i, j, p: (0, p * nj + j), p=p))
                 for p in range(parts)]
    in_specs += list(extra_specs)
    return pl.pallas_call(
        functools.partial(_mm_body, parts=parts, n_extra=len(extras), epilogue=epilogue, norm=norm),
        out_shape=out_shapes,
        grid=(m // tm, nj),
        in_specs=in_specs,
        out_specs=out_specs,
        scratch_shapes=[pltpu.VMEM((tm, k), BF16)] if norm else [],
        compiler_params=_cp("parallel", "arbitrary"),
        name=name,
    )(*args, *([w] * parts), *extras)


def _tile_spec(tm, tn):
    return pl.BlockSpec((tm, tn), lambda i, j: (i, j))


def _ep_residual(accs, extras, outs):
    outs[0][...] = extras[0][...] + accs[0]


def matmul_residual(a, w, res, *, tm=1024, tn=512, name):
    m, n = a.shape[0], w.shape[1]
    tm = min(tm, m)
    return matmul(a, w, tm=tm, tn=tn, epilogue=_ep_residual, out_shapes=_sds((m, n), F32),
                  out_specs=_tile_spec(tm, tn), extras=(res,), extra_specs=(_tile_spec(tm, tn),), name=name)


def _ep_gate_split(accs, extras, outs):
    outs[0][...] = accs[0]
    outs[1][...] = accs[1] * accs[2]


def _ep_glu(accs, extras, outs):
    outs[0][...] = accs[0] * jax.nn.sigmoid(accs[1])


def _ep_qk_norm(accs, extras, outs, *, head_dim, blocks_per_part, split_kv):
    j = pl.program_id(1)
    acc = accs[0]
    gain = extras[0][0]

    def store(cols, val, with_kv):
        outs[0][:, cols] = val.astype(outs[0].dtype)
        if split_kv and with_kv:
            outs[1][:, cols] = val

    def normed(with_kv):
        tn = acc.shape[1]
        for c in range(tn // LANES):
            cols = slice(c * LANES, (c + 1) * LANES)
            x = acc[:, cols]
            sq = x * x
            if head_dim == LANES:
                ms = jnp.sum(sq, axis=-1, keepdims=True) * (1.0 / head_dim)
            else:
                lane = lax.broadcasted_iota(jnp.int32, sq.shape, 1)
                first = lane < head_dim
                s_lo = jnp.sum(jnp.where(first, sq, 0.0), axis=-1, keepdims=True)
                s_hi = jnp.sum(jnp.where(first, 0.0, sq), axis=-1, keepdims=True)
                ms = jnp.where(first, s_lo, s_hi) * (1.0 / head_dim)
            store(cols, x * lax.rsqrt(ms + EPS) * gain[:, cols], with_kv)

    @pl.when(j < blocks_per_part)
    def _():
        normed(False)

    @pl.when((j >= blocks_per_part) & (j < 2 * blocks_per_part))
    def _():
        normed(True)

    @pl.when(j >= 2 * blocks_per_part)
    def _():
        store(slice(None), acc, True)


def matmul_qk_norm(x, norm_gain, w, qn, kn, *, head_dim, split_kv=False, tm=1024, tn=512, name):
    m, n = x.shape[0], w.shape[1]
    tm = min(tm, m)
    bpp = n // 3 // tn
    reps = tn // head_dim
    gains = jnp.stack([jnp.tile(qn, reps), jnp.tile(kn, reps), jnp.ones((tn,), F32)]).reshape(3, 1, tn)
    gain_spec = pl.BlockSpec((1, 1, tn), lambda i, j: (j // bpp, 0, 0))
    if split_kv:
        shapes = (_sds((m, n), BF16), _sds((m, n - n // 3), F32))
        specs = (_tile_spec(tm, tn), pl.BlockSpec((tm, tn), lambda i, j: (i, jnp.maximum(j - bpp, 0))))
    else:
        shapes, specs = (_sds((m, n), F32),), (_tile_spec(tm, tn),)
    return matmul(x, w, tm=tm, tn=tn, norm_gain=norm_gain,
                  epilogue=functools.partial(_ep_qk_norm, head_dim=head_dim, blocks_per_part=bpp, split_kv=split_kv),
                  out_shapes=shapes, out_specs=specs, extras=(gains,), extra_specs=(gain_spec,), name=name)


def _log_sigmoid(x):
    return jnp.minimum(x, 0.0) - jnp.log1p(jnp.exp(-jnp.abs(x)))


def _ep_logf(accs, extras, outs):
    outs[0][...] = _log_sigmoid(accs[0] + extras[0][...])


def _ple_body(x_ref, g_ref, wg_ref, p_ref, wp_ref, o_ref, h_sc, *, tn):
    j = pl.program_id(1)

    @pl.when(j == 0)
    def _():
        h_sc[...] = _rmsnorm(x_ref[...], g_ref[...]).astype(BF16)

    gate = jax.nn.sigmoid(jnp.dot(h_sc[...], wg_ref[...], preferred_element_type=F32))
    proj = jnp.dot(p_ref[...].astype(BF16), wp_ref[...], preferred_element_type=F32)
    o_ref[...] = x_ref[:, pl.ds(pl.multiple_of(j * tn, tn), tn)] + gate * proj


def per_layer_embed(x, norm_gain, p, w_gate, w_proj, layer, *, tm=1024, tn=512):
    m, d = x.shape
    tm = min(tm, m)
    pd = p.shape[2]
    return pl.pallas_call(
        functools.partial(_ple_body, tn=tn),
        out_shape=_sds((m, d), F32),
        grid=(m // tm, d // tn),
        in_specs=[pl.BlockSpec((tm, d), lambda i, j: (i, 0)),
                  pl.BlockSpec((1, d), lambda i, j: (0, 0)),
                  pl.BlockSpec((None, d, tn), lambda i, j: (layer, 0, j)),
                  pl.BlockSpec((None, tm, pd), lambda i, j: (layer, i, 0)),
                  pl.BlockSpec((None, pd, tn), lambda i, j: (layer, 0, j))],
        out_specs=_tile_spec(tm, tn),
        scratch_shapes=[pltpu.VMEM((tm, d), BF16)],
        compiler_params=_cp("parallel", "arbitrary"),
        name="per_layer_embed",
    )(x, norm_gain.reshape(1, d), w_gate, p, w_proj)


def _causal_taps(prev, cur, w, taps):
    p_rows, tm = prev.shape[0], cur.shape[0]
    ext = jnp.concatenate([prev, cur], axis=0)
    n = p_rows + tm
    acc = None
    for rot in range(SUBLANES):
        ks = [k for k in range(taps) if (p_rows - (taps - 1) + k) % SUBLANES == rot]
        if not ks:
            continue
        shifted = ext if rot == 0 else pltpu.roll(ext, n - rot, axis=0)
        for k in ks:
            base = p_rows - (taps - 1) + k - rot
            term = w[k:k + 1, :] * shifted[base:base + tm, :]
            acc = term if acc is None else acc + term
    return acc


def _conv_gate_kernel(cur_ref, prev_ref, hist_ref, w_ref, gate_ref, o_ref, *, taps, chunk):
    at_start = pl.program_id(1) == 0
    nb, tm, tc = cur_ref.shape
    for b in range(nb):
        for c0 in range(0, tc, chunk):
            cs = slice(c0, c0 + chunk)
            prev = jnp.where(at_start, hist_ref[b, :, cs], prev_ref[b, :, cs])
            y = _causal_taps(prev, cur_ref[b, :, cs], w_ref[:, cs], taps)
            o_ref[b, :, cs] = (gate_ref[b, :, cs] * y).astype(o_ref.dtype)


def _conv_ln_kernel(cur_ref, prev_ref, hist_ref, w_ref, lng_ref, lnb_ref, o_ref, c_sc, *, taps, chunk):
    at_start = pl.program_id(1) == 0
    nb, tm, tc = cur_ref.shape
    for b in range(nb):
        for c0 in range(0, tc, chunk):
            cs = slice(c0, c0 + chunk)
            prev = jnp.where(at_start, hist_ref[b, :, cs], prev_ref[b, :, cs])
            c_sc[:, cs] = _causal_taps(prev, cur_ref[b, :, cs], w_ref[:, cs], taps)
        c = c_sc[...]
        mu = jnp.mean(c, axis=-1, keepdims=True)
        xc = c - mu
        var = jnp.mean(xc * xc, axis=-1, keepdims=True)
        y = xc * lax.rsqrt(var + EPS) * lng_ref[...] + lnb_ref[...]
        o_ref[b] = (y * jax.nn.sigmoid(y)).astype(o_ref.dtype)


def _pad_hist(hist, p_rows):
    b, k1, c = hist.shape
    return jnp.concatenate([jnp.zeros((b, p_rows - k1, c), F32), hist.astype(F32)], axis=1)


def causal_conv_fused(src, hist, w, *, taps, gate=None, ln=None, tm, tc, nb):
    bsz, t, c = src.shape
    p_rows = SUBLANES if taps - 1 <= SUBLANES else 32
    hist_p = _pad_hist(hist, p_rows)
    tm = min(tm, t)
    n_t = t // tm
    chunk = min(tc, 256)
    if t >= p_rows:
        prev_src = src
        per = tm // p_rows
        prev_spec = pl.BlockSpec((nb, p_rows, tc), lambda b, i, j: (b, jnp.maximum(i * per - 1, 0), j))
    else:
        prev_src = hist_p
        prev_spec = pl.BlockSpec((nb, p_rows, tc), lambda b, i, j: (b, 0, j))
    cur_spec = pl.BlockSpec((nb, tm, tc), lambda b, i, j: (b, i, j))
    hist_spec = pl.BlockSpec((nb, p_rows, tc), lambda b, i, j: (b, 0, j))
    w_spec = pl.BlockSpec((taps, tc), lambda b, i, j: (0, j))
    grid = (bsz // nb, n_t, c // tc)
    if ln is not None:
        lng, lnb = ln
        vec_spec = pl.BlockSpec((1, tc), lambda b, i, j: (0, j))
        return pl.pallas_call(
            functools.partial(_conv_ln_kernel, taps=taps, chunk=chunk),
            out_shape=_sds((bsz, t, c), BF16), grid=grid,
            in_specs=[cur_spec, prev_spec, hist_spec, w_spec, vec_spec, vec_spec],
            out_specs=cur_spec,
            scratch_shapes=[pltpu.VMEM((tm, tc), F32)],
            compiler_params=_cp("parallel", "arbitrary", "arbitrary"),
            name="conv_ln",
        )(src, prev_src, hist_p, w, lng.reshape(1, c), lnb.reshape(1, c))
    return pl.pallas_call(
        functools.partial(_conv_gate_kernel, taps=taps, chunk=chunk),
        out_shape=_sds((bsz, t, c), BF16), grid=grid,
        in_specs=[cur_spec, prev_spec, hist_spec, w_spec, cur_spec],
        out_specs=cur_spec,
        compiler_params=_cp("parallel", "arbitrary", "arbitrary"),
        name="conv_gate",
    )(src, prev_src, hist_p, w, gate)


def _ffn_kernel(x_ref, halo_ref, hist_ref, gain_ref, wg_ref, wu_ref, cw_ref, wd_ref, o_ref, gh_ref, h_sc,
                *, n_seq, tiles_per_seq, sub):
    i, c = pl.program_id(0), pl.program_id(1)
    tm = x_ref.shape[0]
    halo = h_sc.shape[0] - tm

    @pl.when(c == 0)
    def _():
        x = x_ref[...]
        o_ref[...] = x
        h_sc[halo:, :] = _rmsnorm(x, gain_ref[...]).astype(BF16)
        if halo:
            h_sc[:halo, :] = _rmsnorm(halo_ref[...], gain_ref[...]).astype(BF16)

    d = None
    for c0 in range(0, wg_ref.shape[1], sub):
        cs = slice(c0, c0 + sub)
        g_ext = jnp.dot(h_sc[...], wg_ref[:, cs], preferred_element_type=F32)
        u = jnp.dot(h_sc[halo:, :], wu_ref[:, cs], preferred_element_type=F32)
        cw = cw_ref[:, cs]
        if halo:
            at_start = (i % tiles_per_seq) == 0
            prev = jnp.where(at_start, hist_ref[0, :, cs], g_ext[halo - SUBLANES:halo])
            g = g_ext[halo:]
            y = _causal_taps(prev, g, cw, FFN_CONV)
            gh_ref[0, :, cs] = g[tm - SUBLANES:]
        else:
            rows = tm // n_seq
            ys = []
            for b in range(n_seq):
                g = g_ext[b * rows:(b + 1) * rows]
                ys.append(_causal_taps(hist_ref[b, :, cs], g, cw, FFN_CONV))
                gh_ref[b, :, cs] = g[rows - SUBLANES:]
            y = jnp.concatenate(ys, axis=0)
        a = (y * jax.nn.sigmoid(y) * u).astype(BF16)
        part = jnp.dot(a, wd_ref[cs, :], preferred_element_type=F32)
        d = part if d is None else d + part

    o_ref[...] += d


def ffn_fused(x, norm_gain, hist, w_up, conv_w, w_down, layer, shape, *, tm=512, tf=512):
    bsz, t, d = shape
    m = bsz * t
    d_ff = w_down.shape[1]
    nc = d_ff // tf
    hist_p = _pad_hist(hist, SUBLANES)
    if t >= tm:
        n_seq, tiles_per_seq, halo = 1, t // tm, BF16_SUBLANES
        seq_blk = lambda i: i // tiles_per_seq
    else:
        tm, n_seq, tiles_per_seq, halo = m, bsz, 1, 0
        seq_blk = lambda i: 0
    per = tm // BF16_SUBLANES
    hist_spec = pl.BlockSpec((n_seq, SUBLANES, tf), lambda i, c: (seq_blk(i), 0, c))
    x_new, g_tail = pl.pallas_call(
        functools.partial(_ffn_kernel, n_seq=n_seq, tiles_per_seq=tiles_per_seq, sub=min(tf, 256)),
        out_shape=(_sds((m, d), F32), _sds((bsz * tiles_per_seq, SUBLANES, d_ff), F32)),
        grid=(m // tm, nc),
        in_specs=[pl.BlockSpec((tm, d), lambda i, c: (i, 0)),
                  pl.BlockSpec((BF16_SUBLANES, d), lambda i, c: (jnp.maximum(i * per - 1, 0), 0)),
                  hist_spec,
                  pl.BlockSpec((1, d), lambda i, c: (0, 0)),
                  pl.BlockSpec((None, d, tf), lambda i, c: (layer, 0, c)),
                  pl.BlockSpec((None, d, tf), lambda i, c: (layer, 0, nc + c)),
                  pl.BlockSpec((None, FFN_CONV, tf), lambda i, c: (layer, 0, c)),
                  pl.BlockSpec((None, tf, d), lambda i, c: (layer, c, 0))],
        out_specs=(pl.BlockSpec((tm, d), lambda i, c: (i, 0)),
                   pl.BlockSpec((n_seq, SUBLANES, tf), lambda i, c: (i, 0, c))),
        scratch_shapes=[pltpu.VMEM((halo + tm, d), BF16)],
        compiler_params=_cp("parallel", "arbitrary"),
        name="ffn",
    )(x, x, hist_p, norm_gain.reshape(1, d), w_up, w_up, conv_w, w_down)
    return x_new, g_tail.reshape(bsz, tiles_per_seq, SUBLANES, d_ff)[:, -1]


def _alibi_slope(head):
    return 2.0 ** (-8.0 * (head + 1) / DSA_HEADS)


def _dsa_prompt_kernel(q_ref, kp_ref, kc_ref, vp_ref, vc_ref, o_ref, lse_ref, *, dil, group):
    i = pl.program_id(2)
    tq = Q_BLOCK
    q = q_ref[0].astype(BF16)
    k = jnp.concatenate([kp_ref[0], kc_ref[0]], axis=0).astype(BF16)
    v = jnp.concatenate([vp_ref[0], vc_ref[0]], axis=0).astype(BF16)
    row = lax.broadcasted_iota(jnp.int32, (tq, 2 * tq), 0)
    col = lax.broadcasted_iota(jnp.int32, (tq, 2 * tq), 1)
    dist = row + tq - col
    valid = (dist >= 0) & (dist <= DSA_KEYS) & ((col >= tq) | (i > 0))
    distf = dist.astype(F32) * float(dil)
    for h in range(DSA_HPG):
        sl = slice(h * DSA_HEAD_DIM, (h + 1) * DSA_HEAD_DIM)
        s = lax.dot_general(q[:, sl], k[:, sl], (((1,), (1,)), ((), ())), preferred_element_type=F32)
        s = s * (DSA_HEAD_DIM ** -0.5) - _alibi_slope(group * DSA_HPG + h) * distf
        s = jnp.where(valid, s, NEG)
        m = jnp.max(s, axis=-1, keepdims=True)
        p = jnp.exp(s - m)
        l = jnp.sum(p, axis=-1, keepdims=True)
        o = jnp.dot(p.astype(BF16), v[:, sl], preferred_element_type=F32) / l
        o_ref[0, :, sl] = o
        lse_ref[0, :, sl] = jnp.broadcast_to(m + jnp.log(l), (tq, DSA_HEAD_DIM))


def dsa_prompt_group(qkv, group):
    bsz, t, n = qkv.shape
    dil = DSA_DILATIONS[group]
    s_len = t // dil
    nblk = n // DSA_GW
    view = qkv.reshape(bsz, s_len, dil * n)
    tq = Q_BLOCK

    def spec(part, prev):
        if prev:
            return pl.BlockSpec((1, tq, DSA_GW),
                                lambda b, r, i: (b, jnp.maximum(i - 1, 0), r * nblk + part * DSA_GROUPS + group))
        return pl.BlockSpec((1, tq, DSA_GW), lambda b, r, i: (b, i, r * nblk + part * DSA_GROUPS + group))

    out_spec = pl.BlockSpec((1, tq, DSA_GW), lambda b, r, i: (b, i, r))
    o, lse = pl.pallas_call(
        functools.partial(_dsa_prompt_kernel, dil=dil, group=group),
        out_shape=(_sds((bsz, s_len, dil * DSA_GW), F32), _sds((bsz, s_len, dil * DSA_GW), F32)),
        grid=(bsz, dil, s_len // tq),
        in_specs=[spec(0, False), spec(1, True), spec(1, False), spec(2, True), spec(2, False)],
        out_specs=(out_spec, out_spec),
        compiler_params=_cp("parallel", "parallel", "arbitrary"),
        name=f"dsa_prompt_g{group}",
    )(view, view, view, view, view)
    return o.reshape(bsz, t, DSA_GW), lse.reshape(bsz, t, DSA_GW)


def _dsa_sample_kernel(q_ref, kc_ref, vc_ref, kn_ref, vn_ref, o_ref, lse_ref, *, cache_len, dil, window, group):
    t = q_ref.shape[1]
    pad = jnp.zeros((LANES - t, DSA_GW), F32)
    q = q_ref[0].astype(BF16)
    k = jnp.concatenate([kc_ref[0], kn_ref[0], pad], axis=0).astype(BF16)
    v = jnp.concatenate([vc_ref[0], vn_ref[0], pad], axis=0).astype(BF16)
    rows = cache_len + LANES
    qi = lax.broadcasted_iota(jnp.int32, (t, rows), 0)
    col = lax.broadcasted_iota(jnp.int32, (t, rows), 1)
    dist = cache_len + qi - col
    valid = (dist >= 0) & (dist <= window) & ((dist & (dil - 1)) == 0)
    distf = dist.astype(F32)
    for h in range(DSA_HPG):
        sl = slice(h * DSA_HEAD_DIM, (h + 1) * DSA_HEAD_DIM)
        s = lax.dot_general(q[:, sl], k[:, sl], (((1,), (1,)), ((), ())), preferred_element_type=F32)
        s = s * (DSA_HEAD_DIM ** -0.5) - _alibi_slope(group * DSA_HPG + h) * distf
        s = jnp.where(valid, s, NEG)
        m = jnp.max(s, axis=-1, keepdims=True)
        p = jnp.exp(s - m)
        l = jnp.sum(p, axis=-1, keepdims=True)
        o = jnp.dot(p.astype(BF16), v[:, sl], preferred_element_type=F32) / l
        o_ref[0, :, sl] = o
        lse_ref[0, :, sl] = jnp.broadcast_to(m + jnp.log(l), (t, DSA_HEAD_DIM))


def dsa_sample_group(qkv, cache, group):
    bsz, t, _ = qkv.shape
    cache_len = cache.shape[1]
    cview = cache.reshape(bsz, cache_len, 2 * DSA_GW)
    new_spec = lambda part: pl.BlockSpec((1, t, DSA_GW), lambda b: (b, 0, part * DSA_GROUPS + group))
    cache_spec = lambda part: pl.BlockSpec((1, cache_len, DSA_GW), lambda b: (b, 0, part))
    out_spec = pl.BlockSpec((1, t, DSA_GW), lambda b: (b, 0, 0))
    return pl.pallas_call(
        functools.partial(_dsa_sample_kernel, cache_len=cache_len, dil=DSA_DILATIONS[group],
                          window=DSA_WINDOWS[group], group=group),
        out_shape=(_sds((bsz, t, DSA_GW), F32), _sds((bsz, t, DSA_GW), F32)),
        grid=(bsz,),
        in_specs=[new_spec(0), cache_spec(0), cache_spec(1), new_spec(1), new_spec(2)],
        out_specs=(out_spec, out_spec),
        compiler_params=_cp("parallel"),
        name=f"dsa_sample_g{group}",
    )(qkv, cview, cview, qkv, qkv)


def _dsa_combine_kernel(o0, o1, o2, l0, l1, l2, out_ref):
    ls = (l0[...], l1[...], l2[...])
    m = jnp.maximum(jnp.maximum(ls[0], ls[1]), ls[2])
    es = [jnp.exp(x - m) for x in ls]
    den = es[0] + es[1] + es[2]
    num = es[0] * o0[...] + es[1] * o1[...] + es[2] * o2[...]
    out_ref[...] = (num / den).astype(out_ref.dtype)


def dsa_combine(outs, lses):
    m, n = outs[0].shape
    tm = min(m, 1024)
    spec = pl.BlockSpec((tm, n), lambda i: (i, 0))
    return pl.pallas_call(
        _dsa_combine_kernel, out_shape=_sds((m, n), BF16), grid=(m // tm,),
        in_specs=[spec] * 6, out_specs=spec, compiler_params=_cp("parallel"), name="dsa_combine",
    )(*outs, *lses)


def _split_bf16(x):
    hi = x.astype(BF16)
    lo = (x - hi.astype(F32)).astype(BF16)
    return hi, lo


def _tri_sum(x, tri):
    hi, lo = _split_bf16(x)
    return (jnp.dot(hi, tri, preferred_element_type=F32) + jnp.dot(lo, tri, preferred_element_type=F32))


def _cumsum_kernel(x_ref, o_ref, *, tn):
    j = pl.program_id(0)
    t = x_ref.shape[1]
    src = lax.broadcasted_iota(jnp.int32, (t, tn), 0)
    dst = lax.broadcasted_iota(jnp.int32, (t, tn), 1) + j * tn
    o_ref[...] = _tri_sum(x_ref[...], jnp.where(src <= dst, 1.0, 0.0).astype(BF16))


def cumsum_lanes(x):
    r, t = x.shape
    tn = min(t, 512)
    return pl.pallas_call(
        functools.partial(_cumsum_kernel, tn=tn), out_shape=_sds((r, t), F32), grid=(t // tn,),
        in_specs=[pl.BlockSpec((r, t), lambda j: (0, 0))],
        out_specs=pl.BlockSpec((r, tn), lambda j: (0, j)),
        compiler_params=_cp("arbitrary"), name="fox_cumsum",
    )(x)


def _fox_prompt_kernel(q_ref, k_ref, v_ref, fk_ref, o_ref, *, scale):
    i = pl.program_id(2)
    tq = q_ref.shape[1]
    for ib in range(k_ref.shape[1] // tq):
        @pl.when(i == ib)
        def _(ib=ib):
            nk = (ib + 1) * tq
            s = lax.dot_general(q_ref[0], k_ref[0, :nk, :], (((1,), (1,)), ((), ())),
                                preferred_element_type=F32) * scale - fk_ref[0, :, :nk]
            row = lax.broadcasted_iota(jnp.int32, (tq, nk), 0) + ib * tq
            col = lax.broadcasted_iota(jnp.int32, (tq, nk), 1)
            s = jnp.where(col <= row, s, NEG)
            m = jnp.max(s, axis=-1, keepdims=True)
            p = jnp.exp(s - m)
            l = jnp.sum(p, axis=-1, keepdims=True)
            o = jnp.dot(p.astype(BF16), v_ref[0, :nk, :], preferred_element_type=F32)
            o_ref[0] = (o / l).astype(o_ref.dtype)


def fox_prompt_attention(qkv, f_cum, *, tq=256):
    bsz, t, n = qkv.shape
    d = n // 3
    hd = d // FOX_HEADS
    return pl.pallas_call(
        functools.partial(_fox_prompt_kernel, scale=hd ** -0.5),
        out_shape=_sds((bsz, t, d), BF16),
        grid=(bsz, FOX_HEADS, t // tq),
        in_specs=[pl.BlockSpec((1, tq, hd), lambda b, h, i: (b, i, h)),
                  pl.BlockSpec((1, t, hd), lambda b, h, i: (b, 0, FOX_HEADS + h)),
                  pl.BlockSpec((1, t, hd), lambda b, h, i: (b, 0, 2 * FOX_HEADS + h)),
                  pl.BlockSpec((1, 1, t), lambda b, h, i: (b * FOX_HEADS + h, 0, 0))],
        out_specs=pl.BlockSpec((1, tq, hd), lambda b, h, i: (b, i, h)),
        compiler_params=_cp("parallel", "parallel", "arbitrary"),
        name="fox_prompt_attn",
    )(qkv, qkv, qkv, f_cum)


def _fox_sample_kernel(pt_ref, q_ref, nkv_ref, nlf_ref, *rest, scale, t_new, heads, pps):
    kv_refs, lf_refs = rest[:pps], rest[pps:2 * pps]
    o_ref, pad_sc, m_sc, l_sc, acc_sc, suf_sc = rest[2 * pps:]
    j = pl.program_id(1)
    page = nlf_ref.shape[2]
    hd = q_ref.shape[2] // heads
    r = heads * t_new
    key_rows = 2 * heads
    pitch = pad_sc.shape[1] // page
    src = lax.broadcasted_iota(jnp.int32, (page, page), 0)
    dst = lax.broadcasted_iota(jnp.int32, (page, page), 1)

    def attend(kv_ref, slot, bias, valid, first):
        for key in range(page):
            pad_sc[slot, key * pitch:key * pitch + key_rows, :] = kv_ref[0, key * key_rows:(key + 1) * key_rows, :]
        rows_ref, row_pitch = pad_sc.at[slot], pitch
        s_rows = []
        for h in range(heads):
            k_h = rows_ref[pl.ds(h, page, stride=row_pitch), :].astype(BF16)
            s_h = lax.dot_general(q_ref[0, :, h * hd:(h + 1) * hd], k_h, (((1,), (1,)), ((), ())),
                                  preferred_element_type=F32)
            s_rows.append(s_h * scale + bias[h:h + 1, :])
        s = jnp.concatenate(s_rows, axis=0)
        if valid is not None:
            s = jnp.where(valid, s, NEG)
        m_cur = jnp.max(s, axis=-1, keepdims=True)
        m_new = m_cur if first else jnp.maximum(m_sc[...], m_cur)
        p = jnp.exp(s - m_new)
        pv = jnp.concatenate(
            [jnp.dot(p[h * t_new:(h + 1) * t_new].astype(BF16),
                     rows_ref[pl.ds(heads + h, page, stride=row_pitch), :].astype(BF16),
                     preferred_element_type=F32) for h in range(heads)], axis=0)
        l_cur = jnp.sum(p, axis=-1, keepdims=True)
        if first:
            l_sc[...] = l_cur
            acc_sc[...] = pv
        else:
            c = jnp.exp(m_sc[...] - m_new)
            l_sc[...] = l_sc[...] * c + l_cur
            acc_sc[...] = acc_sc[...] * c + pv
        m_sc[...] = m_new

    @pl.when(j == 0)
    def _():
        fq = _tri_sum(nlf_ref[0], jnp.where(src <= dst, 1.0, 0.0).astype(BF16))
        tq = lax.broadcasted_iota(jnp.int32, (r, page), 0) % t_new
        tk = lax.broadcasted_iota(jnp.int32, (r, page), 1)
        attend(nkv_ref, 0, -fq, tk <= tq, True)
        suf_sc[...] = jnp.zeros_like(suf_sc)

    @pl.when(j > 0)
    def _():
        later_mask = jnp.where(src > dst, 1.0, 0.0).astype(BF16)
        for u in range(pps):
            lf = lf_refs[u][0]
            suf = suf_sc[...]
            attend(kv_refs[u], u, _tri_sum(lf, later_mask) + suf, None, False)
            suf_sc[...] = suf + jnp.sum(lf, axis=-1, keepdims=True)

    @pl.when(j == pl.num_programs(1) - 1)
    def _():
        inv = 1.0 / l_sc[...]
        for h in range(heads):
            rs = slice(h * t_new, (h + 1) * t_new)
            o_ref[0, :, h * hd:(h + 1) * hd] = (acc_sc[rs, :] * inv[rs]).astype(o_ref.dtype)


def fox_sample_attention(qkv, kv_new, logf, kv_pool, logf_pool, page_table):
    bsz, t, n = qkv.shape
    d = n // 3
    heads = FOX_HEADS
    hd = d // heads
    n_pool, page = kv_pool.shape[0], kv_pool.shape[1]
    n_pages = page_table.shape[1]
    pps = FOX_PAGES_PER_STEP
    n_steps = n_pages // pps
    key_rows = 2 * heads
    rows = page * key_rows
    pool = kv_pool.reshape(n_pool, rows, hd)
    pool_lf = logf_pool.transpose(0, 2, 1)
    new_kv = jnp.pad(kv_new.reshape(bsz, t, key_rows, hd), ((0, 0), (0, page - t), (0, 0), (0, 0)))
    new_kv = new_kv.reshape(bsz, rows, hd)
    new_lf = jnp.pad(logf.transpose(0, 2, 1), ((0, 0), (0, 0), (0, page - t)))

    def page_idx(u):
        def index_map(b, j, pt_ref):
            return (pt_ref[b * n_pages + n_pages - 1 - (jnp.maximum(j, 1) - 1) * pps - u], 0, 0)
        return index_map

    own = lambda b, j, pt_ref: (b, 0, 0)
    grid_spec = pltpu.PrefetchScalarGridSpec(
        num_scalar_prefetch=1,
        grid=(bsz, n_steps + 1),
        in_specs=[pl.BlockSpec((1, t, d), own),
                  pl.BlockSpec((1, rows, hd), own),
                  pl.BlockSpec((1, heads, page), own)]
                 + [pl.BlockSpec((1, rows, hd), page_idx(u)) for u in range(pps)]
                 + [pl.BlockSpec((1, heads, page), page_idx(u)) for u in range(pps)],
        out_specs=pl.BlockSpec((1, t, d), own),
        scratch_shapes=[pltpu.VMEM((pps, page * (key_rows + FOX_PITCH_PAD), hd), F32),
                        pltpu.VMEM((heads * t, 1), F32), pltpu.VMEM((heads * t, 1), F32),
                        pltpu.VMEM((heads * t, hd), F32), pltpu.VMEM((heads, 1), F32)],
    )
    return pl.pallas_call(
        functools.partial(_fox_sample_kernel, scale=hd ** -0.5, t_new=t, heads=heads, pps=pps),
        out_shape=_sds((bsz, t, d), BF16),
        grid_spec=grid_spec,
        compiler_params=_cp("parallel", "arbitrary"),
        name="fox_sample_attn",
    )(page_table.reshape(-1), qkv, new_kv, new_lf, *([pool] * pps), *([pool_lf] * pps))


def _flat(x):
    return x.reshape(-1, x.shape[-1])


def _last_rows(hist, new, n):
    t = new.shape[1]
    if t >= n:
        return new[:, t - n:]
    return jnp.concatenate([hist.astype(new.dtype)[:, t:], new], axis=1)


def _conv_tiles(bsz, t):
    return dict(tm=256, nb=1) if t >= 256 else dict(tm=t, nb=bsz)


def _mixer_a(x, gain, hist, w_in, conv_w, w_out, shape):
    bsz, t, d = shape
    m = bsz * t
    tm = min(1024, m)
    gb, u = matmul(x, w_in, tm=tm, tn=512, parts=3, epilogue=_ep_gate_split, norm_gain=gain,
                   out_shapes=(_sds((m, d), F32), _sds((m, d), F32)),
                   out_specs=(_tile_spec(tm, 512), _tile_spec(tm, 512)), name="a_in")
    u3 = u.reshape(bsz, t, d)
    a = causal_conv_fused(u3, hist, conv_w, taps=A_CONV, gate=gb.reshape(bsz, t, d), tc=512,
                          **_conv_tiles(bsz, t))
    x = matmul_residual(_flat(a), w_out, x, name="a_out")
    return x, _last_rows(hist, u3, A_CONV - 1)


def _dsa_kv_rows(qkv3, group, n_last):
    qkv3 = qkv3[:, qkv3.shape[1] - n_last:]
    bsz, t, _ = qkv3.shape
    k0 = DSA_GROUPS * DSA_GW + group * DSA_GW
    v0 = 2 * DSA_GROUPS * DSA_GW + group * DSA_GW
    k = qkv3[:, :, k0:k0 + DSA_GW].reshape(bsz, t, DSA_HPG, DSA_HEAD_DIM)
    v = qkv3[:, :, v0:v0 + DSA_GW].reshape(bsz, t, DSA_HPG, DSA_HEAD_DIM)
    return jnp.stack([k, v], axis=2)


def _mixer_b(x, gain, caches, w_qkv, qn, kn, w_out, shape):
    bsz, t, d = shape
    (qkv,) = matmul_qk_norm(x, gain, w_qkv, qn, kn, head_dim=DSA_HEAD_DIM, name="b_qkv")
    qkv3 = qkv.reshape(bsz, t, -1)
    if caches is None:
        outs, lses = zip(*[dsa_prompt_group(qkv3, g) for g in range(DSA_GROUPS)])
        bufs = [_dsa_kv_rows(qkv3, g, min(DSA_WINDOWS[g], t)) for g in range(DSA_GROUPS)]
    else:
        outs, lses = zip(*[dsa_sample_group(qkv3, caches[g], g) for g in range(DSA_GROUPS)])
        bufs = [jnp.concatenate([caches[g][:, t:], _dsa_kv_rows(qkv3, g, t)], axis=1) for g in range(DSA_GROUPS)]
    o = dsa_combine([_flat(a) for a in outs], [_flat(a) for a in lses])
    x = matmul_residual(o, w_out, x, name="b_out")
    return x, bufs


def _mixer_c(x, gain, hist, w_in, conv_w, ln_g, ln_b, w_out, shape):
    bsz, t, d = shape
    m = bsz * t
    tm = min(1024, m)
    u = matmul(x, w_in, tm=tm, tn=512, parts=2, epilogue=_ep_glu, norm_gain=gain,
               out_shapes=_sds((m, d), F32), out_specs=_tile_spec(tm, 512), name="c_in")
    u3 = u.reshape(bsz, t, d)
    tiles = _conv_tiles(bsz, t)
    a = causal_conv_fused(u3, hist, conv_w, taps=C_CONV, ln=(ln_g, ln_b), tm=min(tiles["tm"], 128), tc=d,
                          nb=tiles["nb"])
    x = matmul_residual(_flat(a), w_out, x, name="c_out")
    return x, _last_rows(hist, u3, C_CONV - 1)


def _mixer_d(x, gain, paged, w_qkv, w_f, b_f, qn, kn, w_out, shape):
    bsz, t, d = shape
    m = bsz * t
    tm = min(1024, m)
    qkv, kv = matmul_qk_norm(x, gain, w_qkv, qn, kn, head_dim=d // FOX_HEADS, split_kv=True, name="d_qkv")
    logf = matmul(x, w_f, tm=tm, tn=LANES, epilogue=_ep_logf, norm_gain=gain, out_shapes=_sds((m, LANES), F32),
                  out_specs=_tile_spec(tm, LANES), extras=(b_f,),
                  extra_specs=(pl.BlockSpec((1, LANES), lambda i, j: (0, 0)),), name="d_logf")
    logf = logf[:, :FOX_HEADS].reshape(bsz, t, FOX_HEADS)
    qkv3 = qkv.reshape(bsz, t, 3 * d)
    kv3 = kv.reshape(bsz, t, 2 * d)
    if paged is None:
        f_cum = cumsum_lanes(logf.transpose(0, 2, 1).reshape(bsz * FOX_HEADS, t))
        o = fox_prompt_attention(qkv3, f_cum.reshape(bsz * FOX_HEADS, 1, t))
    else:
        o = fox_sample_attention(qkv3, kv3, logf, *paged)
    x = matmul_residual(_flat(o), w_out, x, name="d_out")
    return x, kv3.reshape(bsz, t, 2, FOX_HEADS, d // FOX_HEADS), logf


def kernel(x_prompt, x_sample, state_conv_a, cache_dsa_g0, cache_dsa_g1, cache_dsa_g2, state_conv_c,
           cache_fox_kv, cache_fox_logf, state_ffn_conv, page_table, p_prompt, p_sample,
           norm_mix, norm_ffn, w_a_in, conv_a_w, w_a_out, w_b_qkv, qn_b, kn_b, w_b_out,
           w_c_in, conv_c_w, ln_c_g, ln_c_b, w_c_out, w_d_qkvf, b_f, qn_d, kn_d, w_d_out,
           w_ffn_up, conv_ffn_w, w_ffn_down, norm_ple, w_ple_gate, w_ple_proj):
    depth = norm_mix.shape[0]
    d = x_prompt.shape[-1]
    d_ff = w_ffn_down.shape[1]
    shapes = (x_prompt.shape, x_sample.shape)
    xs = [_flat(x_prompt), _flat(x_sample)]
    ps = tuple(p.reshape(depth, -1, p.shape[-1]) for p in (p_prompt, p_sample))
    bf = lambda w: w.astype(BF16)
    w_up_bf, w_down_bf, w_gate_bf, w_proj_bf = bf(w_ffn_up), bf(w_ffn_down), bf(w_ple_gate), bf(w_ple_proj)

    w_d_qkv = bf(w_d_qkvf[:, :3 * d])
    n_f = w_d_qkvf.shape[1] - 3 * d
    w_d_f = bf(jnp.pad(w_d_qkvf[:, 3 * d:], ((0, 0), (0, LANES - n_f))))
    b_f_pad = jnp.pad(b_f.astype(F32), (0, LANES - n_f)).reshape(1, LANES)
    caches = (cache_dsa_g0, cache_dsa_g1, cache_dsa_g2)
    paged = (cache_fox_kv, cache_fox_logf, page_table)

    mixer_out = [dict(), dict()]
    ffn_hist = [[], []]
    for i in range(depth):
        kind = i % 4
        for grp in range(2):
            shape = shapes[grp]
            bsz, t, _ = shape
            x = xs[grp]
            out = mixer_out[grp]
            gain = norm_mix[i]
            if kind == 0:
                hist = jnp.zeros((bsz, A_CONV - 1, d), F32) if grp == 0 else state_conv_a
                x, out["conv_a"] = _mixer_a(x, gain, hist, bf(w_a_in), conv_a_w, bf(w_a_out), shape)
            elif kind == 1:
                x, out["dsa"] = _mixer_b(x, gain, None if grp == 0 else caches, bf(w_b_qkv), qn_b, kn_b,
                                         bf(w_b_out), shape)
            elif kind == 2:
                hist = jnp.zeros((bsz, C_CONV - 1, d), F32) if grp == 0 else state_conv_c
                x, out["conv_c"] = _mixer_c(x, gain, hist, bf(w_c_in), conv_c_w, ln_c_g, ln_c_b, bf(w_c_out), shape)
            else:
                x, out["fox_kv"], out["fox_logf"] = _mixer_d(
                    x, gain, None if grp == 0 else paged, w_d_qkv, w_d_f, b_f_pad, qn_d, kn_d, bf(w_d_out), shape)
            hist = jnp.zeros((bsz, FFN_CONV - 1, d_ff), F32) if grp == 0 else state_ffn_conv[i]
            x, g_last = ffn_fused(x, norm_ffn[i], hist, w_up_bf, conv_ffn_w, w_down_bf, i, shape)
            ffn_hist[grp].append(_last_rows(hist, g_last[:, SUBLANES - min(t, SUBLANES):], FFN_CONV - 1))
            xs[grp] = per_layer_embed(x, norm_ple[i], ps[grp], w_gate_bf, w_proj_bf, i)

    mp, ms = mixer_out
    return (xs[0].reshape(shapes[0]), xs[1].reshape(shapes[1]),
            mp["conv_a"], ms["conv_a"],
            mp["dsa"][0], ms["dsa"][0], mp["dsa"][1], ms["dsa"][1], mp["dsa"][2], ms["dsa"][2],
            mp["conv_c"], ms["conv_c"],
            mp["fox_kv"], ms["fox_kv"], mp["fox_logf"], ms["fox_logf"],
            jnp.stack(ffn_hist[0]), jnp.stack(ffn_hist[1]))
```

```python
import functools

import jax
import jax.numpy as jnp
from jax import lax
from jax.experimental import pallas as pl
from jax.experimental.pallas import tpu as pltpu

F32 = jnp.float32
BF16 = jnp.bfloat16
EPS = 1e-6
NEG = -1e30

VMEM_LIMIT_BYTES = 56 * 1024 * 1024
LANES = 128
SUBLANES = 8
BF16_SUBLANES = 16

A_CONV = 3
C_CONV = 31
FFN_CONV = 3
DSA_WINDOWS = (128, 512, 2048)
DSA_DILATIONS = (1, 4, 16)
DSA_GROUPS = 3
DSA_HPG = 8
DSA_HEAD_DIM = 64
DSA_HEADS = DSA_GROUPS * DSA_HPG
DSA_GW = DSA_HPG * DSA_HEAD_DIM
DSA_KEYS = 128
FOX_HEADS = 16
Q_BLOCK = 128
FOX_PAGES_PER_STEP = 4
FOX_PITCH_PAD = 8


def _cp(*sem):
    return pltpu.CompilerParams(dimension_semantics=sem, vmem_limit_bytes=VMEM_LIMIT_BYTES)


def _sds(shape, dtype):
    return jax.ShapeDtypeStruct(shape, dtype)


def _rmsnorm(x, g):
    ms = jnp.mean(x * x, axis=-1, keepdims=True)
    return x * lax.rsqrt(ms + EPS) * g


def _mm_body(*refs, parts, n_extra, epilogue, norm):
    if norm:
        x_ref, g_ref, a_sc = refs[0], refs[1], refs[-1]
        refs = refs[2:-1]

        @pl.when(pl.program_id(1) == 0)
        def _():
            a_sc[...] = _rmsnorm(x_ref[...], g_ref[...]).astype(BF16)

        a = a_sc[...]
    else:
        a = refs[0][...]
        refs = refs[1:]
    accs = [jnp.dot(a, refs[p][...], preferred_element_type=F32) for p in range(parts)]
    extras = refs[parts:parts + n_extra]
    outs = refs[parts + n_extra:]
    epilogue(accs, extras, outs)


def matmul(a, w, *, tm, tn, epilogue, out_shapes, out_specs, parts=1, extras=(), extra_specs=(),
           norm_gain=None, name):
    m, k = a.shape
    np_ = w.shape[1] // parts
    nj = np_ // tn
    tm = min(tm, m)
    norm = norm_gain is not None
    in_specs = [pl.BlockSpec((tm, k), lambda i, j: (i, 0))]
    args = [a]
    if norm:
        in_specs.append(pl.BlockSpec((1, k), lambda i, j: (0, 0)))
        args.append(norm_gain.reshape(1, k))
    in_specs += [pl.BlockSpec((k, tn), functools.partial(lambda i, j, p: (0, p * nj + j), p=p))
                 for p in range(parts)]
    in_specs += list(extra_specs)
    return pl.pallas_call(
        functools.partial(_mm_body, parts=parts, n_extra=len(extras), epilogue=epilogue, norm=norm),
        out_shape=out_shapes,
        grid=(m // tm, nj),
        in_specs=in_specs,
        out_specs=out_specs,
        scratch_shapes=[pltpu.VMEM((tm, k), BF16)] if norm else [],
        compiler_params=_cp("parallel", "arbitrary"),
        name=name,
    )(*args, *([w] * parts), *extras)


def _tile_spec(tm, tn):
    return pl.BlockSpec((tm, tn), lambda i, j: (i, j))


def _ep_residual(accs, extras, outs):
    outs[0][...] = extras[0][...] + accs[0]


def matmul_residual(a, w, res, *, tm=512, name):
    m, n = a.shape[0], w.shape[1]
    tm, tn = min(tm, m), n
    return matmul(a, w, tm=tm, tn=tn, epilogue=_ep_residual, out_shapes=_sds((m, n), F32),
                  out_specs=_tile_spec(tm, tn), extras=(res,), extra_specs=(_tile_spec(tm, tn),), name=name)


def _ep_gate_split(accs, extras, outs):
    outs[0][...] = accs[0]
    outs[1][...] = accs[1] * accs[2]


def _ep_glu(accs, extras, outs):
    outs[0][...] = accs[0] * jax.nn.sigmoid(accs[1])


def _ep_qk_norm(accs, extras, outs, *, head_dim, blocks_per_part, split_kv):
    j = pl.program_id(1)
    acc = accs[0]
    gain = extras[0][0]

    def store(cols, val, with_kv):
        outs[0][:, cols] = val.astype(outs[0].dtype)
        if split_kv and with_kv:
            outs[1][:, cols] = val

    def normed(with_kv):
        tn = acc.shape[1]
        for c in range(tn // LANES):
            cols = slice(c * LANES, (c + 1) * LANES)
            x = acc[:, cols]
            sq = x * x
            if head_dim == LANES:
                ms = jnp.sum(sq, axis=-1, keepdims=True) * (1.0 / head_dim)
            else:
                lane = lax.broadcasted_iota(jnp.int32, sq.shape, 1)
                first = lane < head_dim
                s_lo = jnp.sum(jnp.where(first, sq, 0.0), axis=-1, keepdims=True)
                s_hi = jnp.sum(jnp.where(first, 0.0, sq), axis=-1, keepdims=True)
                ms = jnp.where(first, s_lo, s_hi) * (1.0 / head_dim)
            store(cols, x * lax.rsqrt(ms + EPS) * gain[:, cols], with_kv)

    @pl.when(j < blocks_per_part)
    def _():
        normed(False)

    @pl.when((j >= blocks_per_part) & (j < 2 * blocks_per_part))
    def _():
        normed(True)

    @pl.when(j >= 2 * blocks_per_part)
    def _():
        store(slice(None), acc, True)


def matmul_qk_norm(x, norm_gain, w, qn, kn, *, head_dim, split_kv=False, tm=1024, tn=512, name):
    m, n = x.shape[0], w.shape[1]
    tm = min(tm, m)
    bpp = n // 3 // tn
    reps = tn // head_dim
    gains = jnp.stack([jnp.tile(qn, reps), jnp.tile(kn, reps), jnp.ones((tn,), F32)]).reshape(3, 1, tn)
    gain_spec = pl.BlockSpec((1, 1, tn), lambda i, j: (j // bpp, 0, 0))
    if split_kv:
        shapes = (_sds((m, n), BF16), _sds((m, n - n // 3), F32))
        specs = (_tile_spec(tm, tn), pl.BlockSpec((tm, tn), lambda i, j: (i, jnp.maximum(j - bpp, 0))))
    else:
        shapes, specs = (_sds((m, n), F32),), (_tile_spec(tm, tn),)
    return matmul(x, w, tm=tm, tn=tn, norm_gain=norm_gain,
                  epilogue=functools.partial(_ep_qk_norm, head_dim=head_dim, blocks_per_part=bpp, split_kv=split_kv),
                  out_shapes=shapes, out_specs=specs, extras=(gains,), extra_specs=(gain_spec,), name=name)


def _log_sigmoid(x):
    return jnp.minimum(x, 0.0) - jnp.log1p(jnp.exp(-jnp.abs(x)))


def _ep_logf(accs, extras, outs):
    outs[0][...] = _log_sigmoid(accs[0] + extras[0][...])


def _ple_body(x_ref, g_ref, wg_ref, p_ref, wp_ref, o_ref, h_sc, *, tn):
    j = pl.program_id(1)

    @pl.when(j == 0)
    def _():
        h_sc[...] = _rmsnorm(x_ref[...], g_ref[...]).astype(BF16)

    gate = jax.nn.sigmoid(jnp.dot(h_sc[...], wg_ref[...], preferred_element_type=F32))
    proj = jnp.dot(p_ref[...].astype(BF16), wp_ref[...], preferred_element_type=F32)
    o_ref[...] = x_ref[:, pl.ds(pl.multiple_of(j * tn, tn), tn)] + gate * proj


def per_layer_embed(x, norm_gain, p, w_gate, w_proj, layer, *, tm=1024, tn=1024):
    m, d = x.shape
    tm = min(tm, m)
    pd = p.shape[2]
    return pl.pallas_call(
        functools.partial(_ple_body, tn=tn),
        out_shape=_sds((m, d), F32),
        grid=(m // tm, d // tn),
        in_specs=[pl.BlockSpec((tm, d), lambda i, j: (i, 0)),
                  pl.BlockSpec((1, d), lambda i, j: (0, 0)),
                  pl.BlockSpec((None, d, tn), lambda i, j: (layer, 0, j)),
                  pl.BlockSpec((None, tm, pd), lambda i, j: (layer, i, 0)),
                  pl.BlockSpec((None, pd, tn), lambda i, j: (layer, 0, j))],
        out_specs=_tile_spec(tm, tn),
        scratch_shapes=[pltpu.VMEM((tm, d), BF16)],
        compiler_params=_cp("parallel", "arbitrary"),
        name="per_layer_embed",
    )(x, norm_gain.reshape(1, d), w_gate, p, w_proj)


def _causal_taps(prev, cur, w, taps):
    p_rows, tm = prev.shape[0], cur.shape[0]
    ext = jnp.concatenate([prev, cur], axis=0)
    n = p_rows + tm
    acc = None
    for rot in range(SUBLANES):
        ks = [k for k in range(taps) if (p_rows - (taps - 1) + k) % SUBLANES == rot]
        if not ks:
            continue
        shifted = ext if rot == 0 else pltpu.roll(ext, n - rot, axis=0)
        for k in ks:
            base = p_rows - (taps - 1) + k - rot
            term = w[k:k + 1, :] * shifted[base:base + tm, :]
            acc = term if acc is None else acc + term
    return acc


def _conv_gate_kernel(cur_ref, prev_ref, hist_ref, w_ref, gate_ref, o_ref, *, taps, chunk):
    at_start = pl.program_id(1) == 0
    nb, tm, tc = cur_ref.shape
    for b in range(nb):
        for c0 in range(0, tc, chunk):
            cs = slice(c0, c0 + chunk)
            prev = jnp.where(at_start, hist_ref[b, :, cs], prev_ref[b, :, cs])
            y = _causal_taps(prev, cur_ref[b, :, cs], w_ref[:, cs], taps)
            o_ref[b, :, cs] = (gate_ref[b, :, cs] * y).astype(o_ref.dtype)


def _conv_ln_kernel(cur_ref, prev_ref, hist_ref, w_ref, lng_ref, lnb_ref, o_ref, c_sc, *, taps, chunk):
    at_start = pl.program_id(1) == 0
    nb, tm, tc = cur_ref.shape
    for b in range(nb):
        for c0 in range(0, tc, chunk):
            cs = slice(c0, c0 + chunk)
            prev = jnp.where(at_start, hist_ref[b, :, cs], prev_ref[b, :, cs])
            c_sc[:, cs] = _causal_taps(prev, cur_ref[b, :, cs], w_ref[:, cs], taps)
        c = c_sc[...]
        mu = jnp.mean(c, axis=-1, keepdims=True)
        xc = c - mu
        var = jnp.mean(xc * xc, axis=-1, keepdims=True)
        y = xc * lax.rsqrt(var + EPS) * lng_ref[...] + lnb_ref[...]
        o_ref[b] = (y * jax.nn.sigmoid(y)).astype(o_ref.dtype)


def _pad_hist(hist, p_rows):
    b, k1, c = hist.shape
    return jnp.concatenate([jnp.zeros((b, p_rows - k1, c), F32), hist.astype(F32)], axis=1)


def causal_conv_fused(src, hist, w, *, taps, gate=None, ln=None, tm, tc, nb):
    bsz, t, c = src.shape
    p_rows = SUBLANES if taps - 1 <= SUBLANES else 32
    hist_p = _pad_hist(hist, p_rows)
    tm = min(tm, t)
    n_t = t // tm
    chunk = min(tc, 256)
    if t >= p_rows:
        prev_src = src
        per = tm // p_rows
        prev_spec = pl.BlockSpec((nb, p_rows, tc), lambda b, i, j: (b, jnp.maximum(i * per - 1, 0), j))
    else:
        prev_src = hist_p
        prev_spec = pl.BlockSpec((nb, p_rows, tc), lambda b, i, j: (b, 0, j))
    cur_spec = pl.BlockSpec((nb, tm, tc), lambda b, i, j: (b, i, j))
    hist_spec = pl.BlockSpec((nb, p_rows, tc), lambda b, i, j: (b, 0, j))
    w_spec = pl.BlockSpec((taps, tc), lambda b, i, j: (0, j))
    grid = (bsz // nb, n_t, c // tc)
    if ln is not None:
        lng, lnb = ln
        vec_spec = pl.BlockSpec((1, tc), lambda b, i, j: (0, j))
        return pl.pallas_call(
            functools.partial(_conv_ln_kernel, taps=taps, chunk=chunk),
            out_shape=_sds((bsz, t, c), BF16), grid=grid,
            in_specs=[cur_spec, prev_spec, hist_spec, w_spec, vec_spec, vec_spec],
            out_specs=cur_spec,
            scratch_shapes=[pltpu.VMEM((tm, tc), F32)],
            compiler_params=_cp("parallel", "arbitrary", "arbitrary"),
            name="conv_ln",
        )(src, prev_src, hist_p, w, lng.reshape(1, c), lnb.reshape(1, c))
    return pl.pallas_call(
        functools.partial(_conv_gate_kernel, taps=taps, chunk=chunk),
        out_shape=_sds((bsz, t, c), BF16), grid=grid,
        in_specs=[cur_spec, prev_spec, hist_spec, w_spec, cur_spec],
        out_specs=cur_spec,
        compiler_params=_cp("parallel", "arbitrary", "arbitrary"),
        name="conv_gate",
    )(src, prev_src, hist_p, w, gate)


def _ffn_kernel(x_ref, halo_ref, hist_ref, gain_ref, wg_ref, wu_ref, cw_ref, wd_ref, o_ref, gh_ref, h_sc,
                *, n_seq, tiles_per_seq, sub):
    i, c = pl.program_id(0), pl.program_id(1)
    tm = x_ref.shape[0]
    halo = h_sc.shape[0] - tm

    @pl.when(c == 0)
    def _():
        x = x_ref[...]
        o_ref[...] = x
        h_sc[halo:, :] = _rmsnorm(x, gain_ref[...]).astype(BF16)
        if halo:
            h_sc[:halo, :] = _rmsnorm(halo_ref[...], gain_ref[...]).astype(BF16)

    d = None
    for c0 in range(0, wg_ref.shape[1], sub):
        cs = slice(c0, c0 + sub)
        g_ext = jnp.dot(h_sc[...], wg_ref[:, cs], preferred_element_type=F32)
        u = jnp.dot(h_sc[halo:, :], wu_ref[:, cs], preferred_element_type=F32)
        cw = cw_ref[:, cs]
        if halo:
            at_start = (i % tiles_per_seq) == 0
            prev = jnp.where(at_start, hist_ref[0, :, cs], g_ext[halo - SUBLANES:halo])
            g = g_ext[halo:]
            y = _causal_taps(prev, g, cw, FFN_CONV)
            gh_ref[0, :, cs] = g[tm - SUBLANES:]
        else:
            rows = tm // n_seq
            ys = []
            for b in range(n_seq):
                g = g_ext[b * rows:(b + 1) * rows]
                ys.append(_causal_taps(hist_ref[b, :, cs], g, cw, FFN_CONV))
                gh_ref[b, :, cs] = g[rows - SUBLANES:]
            y = jnp.concatenate(ys, axis=0)
        a = (y * jax.nn.sigmoid(y) * u).astype(BF16)
        part = jnp.dot(a, wd_ref[cs, :], preferred_element_type=F32)
        d = part if d is None else d + part

    o_ref[...] += d


def ffn_fused(x, norm_gain, hist, w_up, conv_w, w_down, layer, shape, *, tm=1024, tf=512):
    bsz, t, d = shape
    m = bsz * t
    d_ff = w_down.shape[1]
    nc = d_ff // tf
    hist_p = _pad_hist(hist, SUBLANES)
    if t >= tm:
        n_seq, tiles_per_seq, halo = 1, t // tm, BF16_SUBLANES
        seq_blk = lambda i: i // tiles_per_seq
    else:
        tm, n_seq, tiles_per_seq, halo = m, bsz, 1, 0
        seq_blk = lambda i: 0
    per = tm // BF16_SUBLANES
    hist_spec = pl.BlockSpec((n_seq, SUBLANES, tf), lambda i, c: (seq_blk(i), 0, c))
    x_new, g_tail = pl.pallas_call(
        functools.partial(_ffn_kernel, n_seq=n_seq, tiles_per_seq=tiles_per_seq, sub=min(tf, 256)),
        out_shape=(_sds((m, d), F32), _sds((bsz * tiles_per_seq, SUBLANES, d_ff), F32)),
        grid=(m // tm, nc),
        in_specs=[pl.BlockSpec((tm, d), lambda i, c: (i, 0), pipeline_mode=pl.Buffered(1)),
                  pl.BlockSpec((BF16_SUBLANES, d), lambda i, c: (jnp.maximum(i * per - 1, 0), 0)),
                  hist_spec,
                  pl.BlockSpec((1, d), lambda i, c: (0, 0)),
                  pl.BlockSpec((None, d, tf), lambda i, c: (layer, 0, c)),
                  pl.BlockSpec((None, d, tf), lambda i, c: (layer, 0, nc + c)),
                  pl.BlockSpec((None, FFN_CONV, tf), lambda i, c: (layer, 0, c)),
                  pl.BlockSpec((None, tf, d), lambda i, c: (layer, c, 0))],
        out_specs=(pl.BlockSpec((tm, d), lambda i, c: (i, 0)),
                   pl.BlockSpec((n_seq, SUBLANES, tf), lambda i, c: (i, 0, c))),
        scratch_shapes=[pltpu.VMEM((halo + tm, d), BF16)],
        compiler_params=_cp("parallel", "arbitrary"),
        name="ffn",
    )(x, x, hist_p, norm_gain.reshape(1, d), w_up, w_up, conv_w, w_down)
    return x_new, g_tail.reshape(bsz, tiles_per_seq, SUBLANES, d_ff)[:, -1]


def _alibi_slope(head):
    return 2.0 ** (-8.0 * (head + 1) / DSA_HEADS)


def _dsa_prompt_kernel(q_ref, kp_ref, kc_ref, vp_ref, vc_ref, o_ref, lse_ref, *, dil, group):
    i = pl.program_id(2)
    tq = Q_BLOCK
    q = q_ref[0]
    k = jnp.concatenate([kp_ref[0], kc_ref[0]], axis=0)
    v = jnp.concatenate([vp_ref[0], vc_ref[0]], axis=0)
    row = lax.broadcasted_iota(jnp.int32, (tq, 2 * tq), 0)
    col = lax.broadcasted_iota(jnp.int32, (tq, 2 * tq), 1)
    dist = row + tq - col
    valid = (dist >= 0) & (dist <= DSA_KEYS) & ((col >= tq) | (i > 0))
    distf = dist.astype(F32) * float(dil)
    for h in range(DSA_HPG):
        sl = slice(h * DSA_HEAD_DIM, (h + 1) * DSA_HEAD_DIM)
        s = lax.dot_general(q[:, sl], k[:, sl], (((1,), (1,)), ((), ())), preferred_element_type=F32)
        s = s * (DSA_HEAD_DIM ** -0.5) - _alibi_slope(group * DSA_HPG + h) * distf
        s = jnp.where(valid, s, NEG)
        m = jnp.max(s, axis=-1, keepdims=True)
        p = jnp.exp(s - m)
        l = jnp.sum(p, axis=-1, keepdims=True)
        o = jnp.dot(p.astype(BF16), v[:, sl], preferred_element_type=F32) / l
        o_ref[0, :, sl] = o
        lse_ref[0, :, sl] = jnp.broadcast_to(m + jnp.log(l), (tq, DSA_HEAD_DIM))


def dsa_prompt_group(qkv, group):
    bsz, t, n = qkv.shape
    dil = DSA_DILATIONS[group]
    s_len = t // dil
    nblk = n // DSA_GW
    view = qkv.reshape(bsz, s_len, dil * n)
    tq = Q_BLOCK

    def spec(part, prev):
        if prev:
            return pl.BlockSpec((1, tq, DSA_GW),
                                lambda b, r, i: (b, jnp.maximum(i - 1, 0), r * nblk + part * DSA_GROUPS + group))
        return pl.BlockSpec((1, tq, DSA_GW), lambda b, r, i: (b, i, r * nblk + part * DSA_GROUPS + group))

    out_spec = pl.BlockSpec((1, tq, DSA_GW), lambda b, r, i: (b, i, r))
    o, lse = pl.pallas_call(
        functools.partial(_dsa_prompt_kernel, dil=dil, group=group),
        out_shape=(_sds((bsz, s_len, dil * DSA_GW), F32), _sds((bsz, s_len, dil * DSA_GW), F32)),
        grid=(bsz, dil, s_len // tq),
        in_specs=[spec(0, False), spec(1, True), spec(1, False), spec(2, True), spec(2, False)],
        out_specs=(out_spec, out_spec),
        compiler_params=_cp("parallel", "parallel", "arbitrary"),
        name=f"dsa_prompt_g{group}",
    )(view, view, view, view, view)
    return o.reshape(bsz, t, DSA_GW), lse.reshape(bsz, t, DSA_GW)


def _dsa_sample_kernel(q_ref, kc_ref, vc_ref, kn_ref, vn_ref, o_ref, lse_ref, *, cache_len, dil, window, group):
    t = q_ref.shape[1]
    pad = jnp.zeros((LANES - t, DSA_GW), F32)
    q = q_ref[0]
    k = jnp.concatenate([kc_ref[0], kn_ref[0], pad], axis=0).astype(BF16)
    v = jnp.concatenate([vc_ref[0], vn_ref[0], pad], axis=0).astype(BF16)
    rows = cache_len + LANES
    qi = lax.broadcasted_iota(jnp.int32, (t, rows), 0)
    col = lax.broadcasted_iota(jnp.int32, (t, rows), 1)
    dist = cache_len + qi - col
    valid = (dist >= 0) & (dist <= window) & ((dist & (dil - 1)) == 0)
    distf = dist.astype(F32)
    for h in range(DSA_HPG):
        sl = slice(h * DSA_HEAD_DIM, (h + 1) * DSA_HEAD_DIM)
        s = lax.dot_general(q[:, sl], k[:, sl], (((1,), (1,)), ((), ())), preferred_element_type=F32)
        s = s * (DSA_HEAD_DIM ** -0.5) - _alibi_slope(group * DSA_HPG + h) * distf
        s = jnp.where(valid, s, NEG)
        m = jnp.max(s, axis=-1, keepdims=True)
        p = jnp.exp(s - m)
        l = jnp.sum(p, axis=-1, keepdims=True)
        o = jnp.dot(p.astype(BF16), v[:, sl], preferred_element_type=F32) / l
        o_ref[0, :, sl] = o
        lse_ref[0, :, sl] = jnp.broadcast_to(m + jnp.log(l), (t, DSA_HEAD_DIM))


def dsa_sample_group(qkv, kv, cache, group):
    bsz, t, _ = qkv.shape
    cache_len = cache.shape[1]
    cview = cache.reshape(bsz, cache_len, 2 * DSA_GW)
    new_spec = lambda part: pl.BlockSpec((1, t, DSA_GW), lambda b: (b, 0, part * DSA_GROUPS + group))
    cache_spec = lambda part: pl.BlockSpec((1, cache_len, DSA_GW), lambda b: (b, 0, part))
    out_spec = pl.BlockSpec((1, t, DSA_GW), lambda b: (b, 0, 0))
    return pl.pallas_call(
        functools.partial(_dsa_sample_kernel, cache_len=cache_len, dil=DSA_DILATIONS[group],
                          window=DSA_WINDOWS[group], group=group),
        out_shape=(_sds((bsz, t, DSA_GW), F32), _sds((bsz, t, DSA_GW), F32)),
        grid=(bsz,),
        in_specs=[new_spec(0), cache_spec(0), cache_spec(1), new_spec(0), new_spec(1)],
        out_specs=(out_spec, out_spec),
        compiler_params=_cp("parallel"),
        name=f"dsa_sample_g{group}",
    )(qkv, cview, cview, kv, kv)


def _dsa_combine_kernel(o0, o1, o2, l0, l1, l2, out_ref):
    ls = (l0[...], l1[...], l2[...])
    m = jnp.maximum(jnp.maximum(ls[0], ls[1]), ls[2])
    es = [jnp.exp(x - m) for x in ls]
    den = es[0] + es[1] + es[2]
    num = es[0] * o0[...] + es[1] * o1[...] + es[2] * o2[...]
    out_ref[...] = (num / den).astype(out_ref.dtype)


def dsa_combine(outs, lses):
    m, n = outs[0].shape
    tm = min(m, 1024)
    spec = pl.BlockSpec((tm, n), lambda i: (i, 0))
    return pl.pallas_call(
        _dsa_combine_kernel, out_shape=_sds((m, n), BF16), grid=(m // tm,),
        in_specs=[spec] * 6, out_specs=spec, compiler_params=_cp("parallel"), name="dsa_combine",
    )(*outs, *lses)


def _split_bf16(x):
    hi = x.astype(BF16)
    lo = (x - hi.astype(F32)).astype(BF16)
    return hi, lo


def _tri_sum(x, tri):
    hi, lo = _split_bf16(x)
    return (jnp.dot(hi, tri, preferred_element_type=F32) + jnp.dot(lo, tri, preferred_element_type=F32))


def _cumsum_kernel(x_ref, o_ref, *, tn):
    j = pl.program_id(0)
    t = x_ref.shape[1]
    src = lax.broadcasted_iota(jnp.int32, (t, tn), 0)
    dst = lax.broadcasted_iota(jnp.int32, (t, tn), 1) + j * tn
    o_ref[...] = _tri_sum(x_ref[...], jnp.where(src <= dst, 1.0, 0.0).astype(BF16))


def cumsum_lanes(x):
    r, t = x.shape
    tn = min(t, 512)
    return pl.pallas_call(
        functools.partial(_cumsum_kernel, tn=tn), out_shape=_sds((r, t), F32), grid=(t // tn,),
        in_specs=[pl.BlockSpec((r, t), lambda j: (0, 0))],
        out_specs=pl.BlockSpec((r, tn), lambda j: (0, j)),
        compiler_params=_cp("arbitrary"), name="fox_cumsum",
    )(x)


def _fox_prompt_kernel(q_ref, k_ref, v_ref, fk_ref, o_ref, *, scale):
    i = pl.program_id(2)
    tq = q_ref.shape[1]
    for ib in range(k_ref.shape[1] // tq):
        @pl.when(i == ib)
        def _(ib=ib):
            nk = (ib + 1) * tq
            s = lax.dot_general(q_ref[0], k_ref[0, :nk, :], (((1,), (1,)), ((), ())),
                                preferred_element_type=F32) * scale - fk_ref[0, :, :nk]
            row = lax.broadcasted_iota(jnp.int32, (tq, nk), 0) + ib * tq
            col = lax.broadcasted_iota(jnp.int32, (tq, nk), 1)
            s = jnp.where(col <= row, s, NEG)
            m = jnp.max(s, axis=-1, keepdims=True)
            p = jnp.exp(s - m)
            l = jnp.sum(p, axis=-1, keepdims=True)
            o = jnp.dot(p.astype(BF16), v_ref[0, :nk, :], preferred_element_type=F32)
            o_ref[0] = (o / l).astype(o_ref.dtype)


def fox_prompt_attention(qkv, f_cum, *, tq=256):
    bsz, t, n = qkv.shape
    d = n // 3
    hd = d // FOX_HEADS
    return pl.pallas_call(
        functools.partial(_fox_prompt_kernel, scale=hd ** -0.5),
        out_shape=_sds((bsz, t, d), BF16),
        grid=(bsz, FOX_HEADS, t // tq),
        in_specs=[pl.BlockSpec((1, tq, hd), lambda b, h, i: (b, i, h)),
                  pl.BlockSpec((1, t, hd), lambda b, h, i: (b, 0, FOX_HEADS + h)),
                  pl.BlockSpec((1, t, hd), lambda b, h, i: (b, 0, 2 * FOX_HEADS + h)),
                  pl.BlockSpec((1, 1, t), lambda b, h, i: (b * FOX_HEADS + h, 0, 0))],
        out_specs=pl.BlockSpec((1, tq, hd), lambda b, h, i: (b, i, h)),
        compiler_params=_cp("parallel", "parallel", "arbitrary"),
        name="fox_prompt_attn",
    )(qkv, qkv, qkv, f_cum)


def _fox_sample_kernel(pt_ref, q_ref, nkv_ref, nlf_ref, *rest, scale, t_new, heads, pps):
    kv_refs, lf_refs = rest[:pps], rest[pps:2 * pps]
    o_ref, pad_sc, m_sc, l_sc, acc_sc, suf_sc = rest[2 * pps:]
    j = pl.program_id(1)
    page = nlf_ref.shape[2]
    hd = q_ref.shape[2] // heads
    r = heads * t_new
    key_rows = 2 * heads
    pitch = pad_sc.shape[1] // page
    src = lax.broadcasted_iota(jnp.int32, (page, page), 0)
    dst = lax.broadcasted_iota(jnp.int32, (page, page), 1)

    def attend(kv_ref, slot, bias, valid, first):
        for key in range(page):
            pad_sc[slot, key * pitch:key * pitch + key_rows, :] = kv_ref[0, key * key_rows:(key + 1) * key_rows, :]
        rows_ref, row_pitch = pad_sc.at[slot], pitch
        s_rows = []
        for h in range(heads):
            k_h = rows_ref[pl.ds(h, page, stride=row_pitch), :].astype(BF16)
            s_h = lax.dot_general(q_ref[0, :, h * hd:(h + 1) * hd], k_h, (((1,), (1,)), ((), ())),
                                  preferred_element_type=F32)
            s_rows.append(s_h * scale + bias[h:h + 1, :])
        s = jnp.concatenate(s_rows, axis=0)
        if valid is not None:
            s = jnp.where(valid, s, NEG)
        m_cur = jnp.max(s, axis=-1, keepdims=True)
        m_new = m_cur if first else jnp.maximum(m_sc[...], m_cur)
        p = jnp.exp(s - m_new)
        pv = jnp.concatenate(
            [jnp.dot(p[h * t_new:(h + 1) * t_new].astype(BF16),
                     rows_ref[pl.ds(heads + h, page, stride=row_pitch), :].astype(BF16),
                     preferred_element_type=F32) for h in range(heads)], axis=0)
        l_cur = jnp.sum(p, axis=-1, keepdims=True)
        if first:
            l_sc[...] = l_cur
            acc_sc[...] = pv
        else:
            c = jnp.exp(m_sc[...] - m_new)
            l_sc[...] = l_sc[...] * c + l_cur
            acc_sc[...] = acc_sc[...] * c + pv
        m_sc[...] = m_new

    @pl.when(j == 0)
    def _():
        fq = _tri_sum(nlf_ref[0], jnp.where(src <= dst, 1.0, 0.0).astype(BF16))
        tq = lax.broadcasted_iota(jnp.int32, (r, page), 0) % t_new
        tk = lax.broadcasted_iota(jnp.int32, (r, page), 1)
        attend(nkv_ref, 0, -fq, tk <= tq, True)
        suf_sc[...] = jnp.zeros_like(suf_sc)

    @pl.when(j > 0)
    def _():
        later_mask = jnp.where(src > dst, 1.0, 0.0).astype(BF16)
        for u in range(pps):
            lf = lf_refs[u][0]
            suf = suf_sc[...]
            attend(kv_refs[u], u, _tri_sum(lf, later_mask) + suf, None, False)
            suf_sc[...] = suf + jnp.sum(lf, axis=-1, keepdims=True)

    @pl.when(j == pl.num_programs(1) - 1)
    def _():
        inv = 1.0 / l_sc[...]
        for h in range(heads):
            rs = slice(h * t_new, (h + 1) * t_new)
            o_ref[0, :, h * hd:(h + 1) * hd] = (acc_sc[rs, :] * inv[rs]).astype(o_ref.dtype)


def fox_sample_attention(qkv, kv_new, logf, kv_pool, logf_pool, page_table):
    bsz, t, n = qkv.shape
    d = n // 3
    heads = FOX_HEADS
    hd = d // heads
    n_pool, page = kv_pool.shape[0], kv_pool.shape[1]
    n_pages = page_table.shape[1]
    pps = FOX_PAGES_PER_STEP
    n_steps = n_pages // pps
    key_rows = 2 * heads
    rows = page * key_rows
    pool = kv_pool.reshape(n_pool, rows, hd)
    pool_lf = logf_pool.transpose(0, 2, 1)
    new_kv = jnp.pad(kv_new.reshape(bsz, t, key_rows, hd), ((0, 0), (0, page - t), (0, 0), (0, 0)))
    new_kv = new_kv.reshape(bsz, rows, hd)
    new_lf = jnp.pad(logf.transpose(0, 2, 1), ((0, 0), (0, 0), (0, page - t)))

    def page_idx(u):
        def index_map(b, j, pt_ref):
            return (pt_ref[b * n_pages + n_pages - 1 - (jnp.maximum(j, 1) - 1) * pps - u], 0, 0)
        return index_map

    own = lambda b, j, pt_ref: (b, 0, 0)
    grid_spec = pltpu.PrefetchScalarGridSpec(
        num_scalar_prefetch=1,
        grid=(bsz, n_steps + 1),
        in_specs=[pl.BlockSpec((1, t, d), own),
                  pl.BlockSpec((1, rows, hd), own),
                  pl.BlockSpec((1, heads, page), own)]
                 + [pl.BlockSpec((1, rows, hd), page_idx(u)) for u in range(pps)]
                 + [pl.BlockSpec((1, heads, page), page_idx(u)) for u in range(pps)],
        out_specs=pl.BlockSpec((1, t, d), own),
        scratch_shapes=[pltpu.VMEM((pps, page * (key_rows + FOX_PITCH_PAD), hd), F32),
                        pltpu.VMEM((heads * t, 1), F32), pltpu.VMEM((heads * t, 1), F32),
                        pltpu.VMEM((heads * t, hd), F32), pltpu.VMEM((heads, 1), F32)],
    )
    return pl.pallas_call(
        functools.partial(_fox_sample_kernel, scale=hd ** -0.5, t_new=t, heads=heads, pps=pps),
        out_shape=_sds((bsz, t, d), BF16),
        grid_spec=grid_spec,
        compiler_params=_cp("parallel", "arbitrary"),
        name="fox_sample_attn",
    )(page_table.reshape(-1), qkv, new_kv, new_lf, *([pool] * pps), *([pool_lf] * pps))


def _flat(x):
    return x.reshape(-1, x.shape[-1])


def _last_rows(hist, new, n):
    t = new.shape[1]
    if t >= n:
        return new[:, t - n:]
    return jnp.concatenate([hist.astype(new.dtype)[:, t:], new], axis=1)


def _conv_tiles(bsz, t):
    return dict(tm=256, nb=1) if t >= 256 else dict(tm=t, nb=bsz)


def _mixer_a(x, gain, hist, w_in, conv_w, w_out, shape):
    bsz, t, d = shape
    m = bsz * t
    tm = min(1024, m)
    gb, u = matmul(x, w_in, tm=tm, tn=512, parts=3, epilogue=_ep_gate_split, norm_gain=gain,
                   out_shapes=(_sds((m, d), F32), _sds((m, d), F32)),
                   out_specs=(_tile_spec(tm, 512), _tile_spec(tm, 512)), name="a_in")
    u3 = u.reshape(bsz, t, d)
    a = causal_conv_fused(u3, hist, conv_w, taps=A_CONV, gate=gb.reshape(bsz, t, d), tc=512,
                          **_conv_tiles(bsz, t))
    x = matmul_residual(_flat(a), w_out, x, name="a_out")
    return x, _last_rows(hist, u3, A_CONV - 1)


def _dsa_kv_rows(kv3, group, n_last):
    kv3 = kv3[:, kv3.shape[1] - n_last:]
    bsz, t, _ = kv3.shape
    k0 = group * DSA_GW
    v0 = DSA_GROUPS * DSA_GW + group * DSA_GW
    k = kv3[:, :, k0:k0 + DSA_GW].reshape(bsz, t, DSA_HPG, DSA_HEAD_DIM)
    v = kv3[:, :, v0:v0 + DSA_GW].reshape(bsz, t, DSA_HPG, DSA_HEAD_DIM)
    return jnp.stack([k, v], axis=2)


def _mixer_b(x, gain, caches, w_qkv, qn, kn, w_out, shape):
    bsz, t, d = shape
    qkv, kv = matmul_qk_norm(x, gain, w_qkv, qn, kn, head_dim=DSA_HEAD_DIM, split_kv=True, name="b_qkv")
    qkv3 = qkv.reshape(bsz, t, -1)
    kv3 = kv.reshape(bsz, t, -1)
    if caches is None:
        outs, lses = zip(*[dsa_prompt_group(qkv3, g) for g in range(DSA_GROUPS)])
        bufs = [_dsa_kv_rows(kv3, g, min(DSA_WINDOWS[g], t)) for g in range(DSA_GROUPS)]
    else:
        outs, lses = zip(*[dsa_sample_group(qkv3, kv3, caches[g], g) for g in range(DSA_GROUPS)])
        bufs = [jnp.concatenate([caches[g][:, t:], _dsa_kv_rows(kv3, g, t)], axis=1) for g in range(DSA_GROUPS)]
    o = dsa_combine([_flat(a) for a in outs], [_flat(a) for a in lses])
    x = matmul_residual(o, w_out, x, name="b_out")
    return x, bufs


def _mixer_c(x, gain, hist, w_in, conv_w, ln_g, ln_b, w_out, shape):
    bsz, t, d = shape
    m = bsz * t
    tm = min(1024, m)
    u = matmul(x, w_in, tm=tm, tn=512, parts=2, epilogue=_ep_glu, norm_gain=gain,
               out_shapes=_sds((m, d), F32), out_specs=_tile_spec(tm, 512), name="c_in")
    u3 = u.reshape(bsz, t, d)
    tiles = _conv_tiles(bsz, t)
    a = causal_conv_fused(u3, hist, conv_w, taps=C_CONV, ln=(ln_g, ln_b), tm=min(tiles["tm"], 128), tc=d,
                          nb=tiles["nb"])
    x = matmul_residual(_flat(a), w_out, x, name="c_out")
    return x, _last_rows(hist, u3, C_CONV - 1)


def _mixer_d(x, gain, paged, w_qkv, w_f, b_f, qn, kn, w_out, shape):
    bsz, t, d = shape
    m = bsz * t
    tm = min(1024, m)
    qkv, kv = matmul_qk_norm(x, gain, w_qkv, qn, kn, head_dim=d // FOX_HEADS, split_kv=True, name="d_qkv")
    logf = matmul(x, w_f, tm=tm, tn=LANES, epilogue=_ep_logf, norm_gain=gain, out_shapes=_sds((m, LANES), F32),
                  out_specs=_tile_spec(tm, LANES), extras=(b_f,),
                  extra_specs=(pl.BlockSpec((1, LANES), lambda i, j: (0, 0)),), name="d_logf")
    logf = logf[:, :FOX_HEADS].reshape(bsz, t, FOX_HEADS)
    qkv3 = qkv.reshape(bsz, t, 3 * d)
    kv3 = kv.reshape(bsz, t, 2 * d)
    if paged is None:
        f_cum = cumsum_lanes(logf.transpose(0, 2, 1).reshape(bsz * FOX_HEADS, t))
        o = fox_prompt_attention(qkv3, f_cum.reshape(bsz * FOX_HEADS, 1, t))
    else:
        o = fox_sample_attention(qkv3, kv3, logf, *paged)
    x = matmul_residual(_flat(o), w_out, x, name="d_out")
    return x, kv3.reshape(bsz, t, 2, FOX_HEADS, d // FOX_HEADS), logf


def kernel(x_prompt, x_sample, state_conv_a, cache_dsa_g0, cache_dsa_g1, cache_dsa_g2, state_conv_c,
           cache_fox_kv, cache_fox_logf, state_ffn_conv, page_table, p_prompt, p_sample,
           norm_mix, norm_ffn, w_a_in, conv_a_w, w_a_out, w_b_qkv, qn_b, kn_b, w_b_out,
           w_c_in, conv_c_w, ln_c_g, ln_c_b, w_c_out, w_d_qkvf, b_f, qn_d, kn_d, w_d_out,
           w_ffn_up, conv_ffn_w, w_ffn_down, norm_ple, w_ple_gate, w_ple_proj):
    depth = norm_mix.shape[0]
    d = x_prompt.shape[-1]
    d_ff = w_ffn_down.shape[1]
    shapes = (x_prompt.shape, x_sample.shape)
    xs = [_flat(x_prompt), _flat(x_sample)]
    ps = tuple(p.reshape(depth, -1, p.shape[-1]) for p in (p_prompt, p_sample))
    bf = lambda w: w.astype(BF16)
    w_up_bf, w_down_bf, w_gate_bf, w_proj_bf = bf(w_ffn_up), bf(w_ffn_down), bf(w_ple_gate), bf(w_ple_proj)

    w_d_qkv = bf(w_d_qkvf[:, :3 * d])
    n_f = w_d_qkvf.shape[1] - 3 * d
    w_d_f = bf(jnp.pad(w_d_qkvf[:, 3 * d:], ((0, 0), (0, LANES - n_f))))
    b_f_pad = jnp.pad(b_f.astype(F32), (0, LANES - n_f)).reshape(1, LANES)
    caches = (cache_dsa_g0, cache_dsa_g1, cache_dsa_g2)
    paged = (cache_fox_kv, cache_fox_logf, page_table)

    mixer_out = [dict(), dict()]
    ffn_hist = [[], []]
    for i in range(depth):
        kind = i % 4
        for grp in range(2):
            shape = shapes[grp]
            bsz, t, _ = shape
            x = xs[grp]
            out = mixer_out[grp]
            gain = norm_mix[i]
            if kind == 0:
                hist = jnp.zeros((bsz, A_CONV - 1, d), F32) if grp == 0 else state_conv_a
                x, out["conv_a"] = _mixer_a(x, gain, hist, bf(w_a_in), conv_a_w, bf(w_a_out), shape)
            elif kind == 1:
                x, out["dsa"] = _mixer_b(x, gain, None if grp == 0 else caches, bf(w_b_qkv), qn_b, kn_b,
                                         bf(w_b_out), shape)
            elif kind == 2:
                hist = jnp.zeros((bsz, C_CONV - 1, d), F32) if grp == 0 else state_conv_c
                x, out["conv_c"] = _mixer_c(x, gain, hist, bf(w_c_in), conv_c_w, ln_c_g, ln_c_b, bf(w_c_out), shape)
            else:
                x, out["fox_kv"], out["fox_logf"] = _mixer_d(
                    x, gain, None if grp == 0 else paged, w_d_qkv, w_d_f, b_f_pad, qn_d, kn_d, bf(w_d_out), shape)
            hist = jnp.zeros((bsz, FFN_CONV - 1, d_ff), F32) if grp == 0 else state_ffn_conv[i]
            x, g_last = ffn_fused(x, norm_ffn[i], hist, w_up_bf, conv_ffn_w, w_down_bf, i, shape)
            ffn_hist[grp].append(_last_rows(hist, g_last[:, SUBLANES - min(t, SUBLANES):], FFN_CONV - 1))
            xs[grp] = per_layer_embed(x, norm_ple[i], ps[grp], w_gate_bf, w_proj_bf, i)

    mp, ms = mixer_out
    return (xs[0].reshape(shapes[0]), xs[1].reshape(shapes[1]),
            mp["conv_a"], ms["conv_a"],
            mp["dsa"][0], ms["dsa"][0], mp["dsa"][1], ms["dsa"][1], mp["dsa"][2], ms["dsa"][2],
            mp["conv_c"], ms["conv_c"],
            mp["fox_kv"], ms["fox_kv"], mp["fox_logf"], ms["fox_logf"],
            jnp.stack(ffn_hist[0]), jnp.stack(ffn_hist[1]))
```

```python
import functools

import jax
import jax.numpy as jnp
from jax import lax
from jax.experimental import pallas as pl
from jax.experimental.pallas import tpu as pltpu

F32 = jnp.float32
BF16 = jnp.bfloat16
EPS = 1e-6
NEG = -1e30

VMEM_LIMIT_BYTES = 56 * 1024 * 1024
LANES = 128
SUBLANES = 8
BF16_SUBLANES = 16
MXU_COLS = 256

A_CONV = 3
C_CONV = 31
FFN_CONV = 3
DSA_WINDOWS = (128, 512, 2048)
DSA_DILATIONS = (1, 4, 16)
DSA_GROUPS = 3
DSA_HPG = 8
DSA_HEAD_DIM = 64
DSA_HEADS = DSA_GROUPS * DSA_HPG
DSA_GW = DSA_HPG * DSA_HEAD_DIM
DSA_KEYS = 128
FOX_HEADS = 16
Q_BLOCK = 128
FOX_PAGES_PER_STEP = 4
FOX_PITCH_PAD = 8


def _cp(*sem):
    return pltpu.CompilerParams(dimension_semantics=sem, vmem_limit_bytes=VMEM_LIMIT_BYTES)


def _sds(shape, dtype):
    return jax.ShapeDtypeStruct(shape, dtype)


def _rmsnorm(x, g):
    ms = jnp.mean(x * x, axis=-1, keepdims=True)
    return x * lax.rsqrt(ms + EPS) * g


def _mm_body(*refs, parts, n_extra, epilogue, norm):
    if norm:
        x_ref, g_ref, a_sc = refs[0], refs[1], refs[-1]
        refs = refs[2:-1]

        @pl.when(pl.program_id(1) == 0)
        def _():
            a_sc[...] = _rmsnorm(x_ref[...], g_ref[...]).astype(BF16)

        a = a_sc[...]
    else:
        a = refs[0][...]
        refs = refs[1:]
    extras = refs[parts:parts + n_extra]
    outs = refs[parts + n_extra:]

    def dot(cols):
        return [jnp.dot(a, refs[p][:, cols], preferred_element_type=F32) for p in range(parts)]

    epilogue(dot, extras, outs)


def _col_chunks(tn):
    step = min(tn, MXU_COLS)
    return [slice(c0, c0 + step) for c0 in range(0, tn, step)]


def matmul(a, w, *, tm, tn, epilogue, out_shapes, out_specs, parts=1, extras=(), extra_specs=(),
           norm_gain=None, name):
    m, k = a.shape
    np_ = w.shape[1] // parts
    nj = np_ // tn
    tm = min(tm, m)
    norm = norm_gain is not None
    in_specs = [pl.BlockSpec((tm, k), lambda i, j: (i, 0))]
    args = [a]
    if norm:
        in_specs.append(pl.BlockSpec((1, k), lambda i, j: (0, 0)))
        args.append(norm_gain.reshape(1, k))
    in_specs += [pl.BlockSpec((k, tn), functools.partial(lambda i, j, p: (0, p * nj + j), p=p))
                 for p in range(parts)]
    in_specs += list(extra_specs)
    return pl.pallas_call(
        functools.partial(_mm_body, parts=parts, n_extra=len(extras), epilogue=epilogue, norm=norm),
        out_shape=out_shapes,
        grid=(m // tm, nj),
        in_specs=in_specs,
        out_specs=out_specs,
        scratch_shapes=[pltpu.VMEM((tm, k), BF16)] if norm else [],
        compiler_params=_cp("parallel", "arbitrary"),
        name=name,
    )(*args, *([w] * parts), *extras)


def _tile_spec(tm, tn):
    return pl.BlockSpec((tm, tn), lambda i, j: (i, j))


def _ep_residual(dot, extras, outs):
    for cols in _col_chunks(outs[0].shape[1]):
        outs[0][:, cols] = extras[0][:, cols] + dot(cols)[0]


def matmul_residual(a, w, res, *, tm=512, name):
    m, n = a.shape[0], w.shape[1]
    tm, tn = min(tm, m), n
    return matmul(a, w, tm=tm, tn=tn, epilogue=_ep_residual, out_shapes=_sds((m, n), F32),
                  out_specs=_tile_spec(tm, tn), extras=(res,), extra_specs=(_tile_spec(tm, tn),), name=name)


def _ep_gate_split(dot, extras, outs):
    for cols in _col_chunks(outs[0].shape[1]):
        gb, gc, v = dot(cols)
        outs[0][:, cols] = gb
        outs[1][:, cols] = gc * v


def _ep_glu(dot, extras, outs):
    for cols in _col_chunks(outs[0].shape[1]):
        a, g = dot(cols)
        outs[0][:, cols] = a * jax.nn.sigmoid(g)


def _ep_qk_norm(dot, extras, outs, *, head_dim, blocks_per_part, split_kv):
    j = pl.program_id(1)
    gain = extras[0][0]
    tn = outs[0].shape[1]

    def store(cols, val, with_kv):
        outs[0][:, cols] = val.astype(outs[0].dtype)
        if split_kv and with_kv:
            outs[1][:, cols] = val

    def tile(norm, with_kv):
        for chunk in _col_chunks(tn):
            acc = dot(chunk)[0]
            if not norm:
                store(chunk, acc, with_kv)
                continue
            for c0 in range(0, acc.shape[1], LANES):
                cols = slice(chunk.start + c0, chunk.start + c0 + LANES)
                x = acc[:, c0:c0 + LANES]
                sq = x * x
                if head_dim == LANES:
                    ms = jnp.sum(sq, axis=-1, keepdims=True) * (1.0 / head_dim)
                else:
                    lane = lax.broadcasted_iota(jnp.int32, sq.shape, 1)
                    first = lane < head_dim
                    s_lo = jnp.sum(jnp.where(first, sq, 0.0), axis=-1, keepdims=True)
                    s_hi = jnp.sum(jnp.where(first, 0.0, sq), axis=-1, keepdims=True)
                    ms = jnp.where(first, s_lo, s_hi) * (1.0 / head_dim)
                store(cols, x * lax.rsqrt(ms + EPS) * gain[:, cols], with_kv)

    @pl.when(j < blocks_per_part)
    def _():
        tile(True, False)

    @pl.when((j >= blocks_per_part) & (j < 2 * blocks_per_part))
    def _():
        tile(True, True)

    @pl.when(j >= 2 * blocks_per_part)
    def _():
        tile(False, True)


def matmul_qk_norm(x, norm_gain, w, qn, kn, *, head_dim, split_kv=False, tm=1024, tn=512, name):
    m, n = x.shape[0], w.shape[1]
    tm = min(tm, m)
    bpp = n // 3 // tn
    reps = tn // head_dim
    gains = jnp.stack([jnp.tile(qn, reps), jnp.tile(kn, reps), jnp.ones((tn,), F32)]).reshape(3, 1, tn)
    gain_spec = pl.BlockSpec((1, 1, tn), lambda i, j: (j // bpp, 0, 0))
    if split_kv:
        shapes = (_sds((m, n), BF16), _sds((m, n - n // 3), F32))
        specs = (_tile_spec(tm, tn), pl.BlockSpec((tm, tn), lambda i, j: (i, jnp.maximum(j - bpp, 0))))
    else:
        shapes, specs = (_sds((m, n), F32),), (_tile_spec(tm, tn),)
    return matmul(x, w, tm=tm, tn=tn, norm_gain=norm_gain,
                  epilogue=functools.partial(_ep_qk_norm, head_dim=head_dim, blocks_per_part=bpp, split_kv=split_kv),
                  out_shapes=shapes, out_specs=specs, extras=(gains,), extra_specs=(gain_spec,), name=name)


def _log_sigmoid(x):
    return jnp.minimum(x, 0.0) - jnp.log1p(jnp.exp(-jnp.abs(x)))


def _ep_logf(dot, extras, outs):
    outs[0][...] = _log_sigmoid(dot(slice(None))[0] + extras[0][...])


def _ple_body(x_ref, g_ref, wg_ref, p_ref, wp_ref, o_ref, h_sc, *, tn):
    j = pl.program_id(1)

    @pl.when(j == 0)
    def _():
        h_sc[...] = _rmsnorm(x_ref[...], g_ref[...]).astype(BF16)

    h = h_sc[...]
    p = p_ref[...].astype(BF16)
    for cols in _col_chunks(tn):
        gate = jax.nn.sigmoid(jnp.dot(h, wg_ref[:, cols], preferred_element_type=F32))
        proj = jnp.dot(p, wp_ref[:, cols], preferred_element_type=F32)
        x = x_ref[:, pl.ds(pl.multiple_of(j * tn + cols.start, cols.stop - cols.start), cols.stop - cols.start)]
        o_ref[:, cols] = x + gate * proj


def per_layer_embed(x, norm_gain, p, w_gate, w_proj, layer, *, tm=1024, tn=1024):
    m, d = x.shape
    tm = min(tm, m)
    pd = p.shape[2]
    return pl.pallas_call(
        functools.partial(_ple_body, tn=tn),
        out_shape=_sds((m, d), F32),
        grid=(m // tm, d // tn),
        in_specs=[pl.BlockSpec((tm, d), lambda i, j: (i, 0)),
                  pl.BlockSpec((1, d), lambda i, j: (0, 0)),
                  pl.BlockSpec((None, d, tn), lambda i, j: (layer, 0, j)),
                  pl.BlockSpec((None, tm, pd), lambda i, j: (layer, i, 0)),
                  pl.BlockSpec((None, pd, tn), lambda i, j: (layer, 0, j))],
        out_specs=_tile_spec(tm, tn),
        scratch_shapes=[pltpu.VMEM((tm, d), BF16)],
        compiler_params=_cp("parallel", "arbitrary"),
        name="per_layer_embed",
    )(x, norm_gain.reshape(1, d), w_gate, p, w_proj)


def _causal_taps(prev, cur, w, taps):
    p_rows, tm = prev.shape[0], cur.shape[0]
    ext = jnp.concatenate([prev, cur], axis=0)
    n = p_rows + tm
    acc = None
    for rot in range(SUBLANES):
        ks = [k for k in range(taps) if (p_rows - (taps - 1) + k) % SUBLANES == rot]
        if not ks:
            continue
        shifted = ext if rot == 0 else pltpu.roll(ext, n - rot, axis=0)
        for k in ks:
            base = p_rows - (taps - 1) + k - rot
            term = w[k:k + 1, :] * shifted[base:base + tm, :]
            acc = term if acc is None else acc + term
    return acc


def _conv_gate_kernel(cur_ref, prev_ref, hist_ref, w_ref, gate_ref, o_ref, *, taps, chunk):
    at_start = pl.program_id(1) == 0
    nb, tm, tc = cur_ref.shape
    for b in range(nb):
        for c0 in range(0, tc, chunk):
            cs = slice(c0, c0 + chunk)
            prev = jnp.where(at_start, hist_ref[b, :, cs], prev_ref[b, :, cs])
            y = _causal_taps(prev, cur_ref[b, :, cs], w_ref[:, cs], taps)
            o_ref[b, :, cs] = (gate_ref[b, :, cs] * y).astype(o_ref.dtype)


def _conv_ln_kernel(cur_ref, prev_ref, hist_ref, w_ref, lng_ref, lnb_ref, o_ref, c_sc, *, taps, chunk):
    at_start = pl.program_id(1) == 0
    nb, tm, tc = cur_ref.shape
    for b in range(nb):
        for c0 in range(0, tc, chunk):
            cs = slice(c0, c0 + chunk)
            prev = jnp.where(at_start, hist_ref[b, :, cs], prev_ref[b, :, cs])
            c_sc[:, cs] = _causal_taps(prev, cur_ref[b, :, cs], w_ref[:, cs], taps)
        c = c_sc[...]
        mu = jnp.mean(c, axis=-1, keepdims=True)
        xc = c - mu
        var = jnp.mean(xc * xc, axis=-1, keepdims=True)
        y = xc * lax.rsqrt(var + EPS) * lng_ref[...] + lnb_ref[...]
        o_ref[b] = (y * jax.nn.sigmoid(y)).astype(o_ref.dtype)


def _pad_hist(hist, p_rows):
    b, k1, c = hist.shape
    return jnp.concatenate([jnp.zeros((b, p_rows - k1, c), F32), hist.astype(F32)], axis=1)


def causal_conv_fused(src, hist, w, *, taps, gate=None, ln=None, tm, tc, nb):
    bsz, t, c = src.shape
    p_rows = SUBLANES if taps - 1 <= SUBLANES else 32
    hist_p = _pad_hist(hist, p_rows)
    tm = min(tm, t)
    n_t = t // tm
    chunk = min(tc, 256)
    if t >= p_rows:
        prev_src = src
        per = tm // p_rows
        prev_spec = pl.BlockSpec((nb, p_rows, tc), lambda b, i, j: (b, jnp.maximum(i * per - 1, 0), j))
    else:
        prev_src = hist_p
        prev_spec = pl.BlockSpec((nb, p_rows, tc), lambda b, i, j: (b, 0, j))
    cur_spec = pl.BlockSpec((nb, tm, tc), lambda b, i, j: (b, i, j))
    hist_spec = pl.BlockSpec((nb, p_rows, tc), lambda b, i, j: (b, 0, j))
    w_spec = pl.BlockSpec((taps, tc), lambda b, i, j: (0, j))
    grid = (bsz // nb, n_t, c // tc)
    if ln is not None:
        lng, lnb = ln
        vec_spec = pl.BlockSpec((1, tc), lambda b, i, j: (0, j))
        return pl.pallas_call(
            functools.partial(_conv_ln_kernel, taps=taps, chunk=chunk),
            out_shape=_sds((bsz, t, c), BF16), grid=grid,
            in_specs=[cur_spec, prev_spec, hist_spec, w_spec, vec_spec, vec_spec],
            out_specs=cur_spec,
            scratch_shapes=[pltpu.VMEM((tm, tc), F32)],
            compiler_params=_cp("parallel", "arbitrary", "arbitrary"),
            name="conv_ln",
        )(src, prev_src, hist_p, w, lng.reshape(1, c), lnb.reshape(1, c))
    return pl.pallas_call(
        functools.partial(_conv_gate_kernel, taps=taps, chunk=chunk),
        out_shape=_sds((bsz, t, c), BF16), grid=grid,
        in_specs=[cur_spec, prev_spec, hist_spec, w_spec, cur_spec],
        out_specs=cur_spec,
        compiler_params=_cp("parallel", "arbitrary", "arbitrary"),
        name="conv_gate",
    )(src, prev_src, hist_p, w, gate)


def _ffn_kernel(x_ref, halo_ref, hist_ref, gain_ref, wg_ref, wu_ref, cw_ref, wd_ref, o_ref, gh_ref, h_sc,
                *, n_seq, tiles_per_seq, sub):
    i, c = pl.program_id(0), pl.program_id(1)
    tm = x_ref.shape[0]
    halo = h_sc.shape[0] - tm

    @pl.when(c == 0)
    def _():
        x = x_ref[...]
        o_ref[...] = x
        h_sc[halo:, :] = _rmsnorm(x, gain_ref[...]).astype(BF16)
        if halo:
            h_sc[:halo, :] = _rmsnorm(halo_ref[...], gain_ref[...]).astype(BF16)

    d = None
    for c0 in range(0, wg_ref.shape[1], sub):
        cs = slice(c0, c0 + sub)
        g_ext = jnp.dot(h_sc[...], wg_ref[:, cs], preferred_element_type=F32)
        u = jnp.dot(h_sc[halo:, :], wu_ref[:, cs], preferred_element_type=F32)
        cw = cw_ref[:, cs]
        if halo:
            at_start = (i % tiles_per_seq) == 0
            prev = jnp.where(at_start, hist_ref[0, :, cs], g_ext[halo - SUBLANES:halo])
            g = g_ext[halo:]
            y = _causal_taps(prev, g, cw, FFN_CONV)
            gh_ref[0, :, cs] = g[tm - SUBLANES:]
        else:
            rows = tm // n_seq
            ys = []
            for b in range(n_seq):
                g = g_ext[b * rows:(b + 1) * rows]
                ys.append(_causal_taps(hist_ref[b, :, cs], g, cw, FFN_CONV))
                gh_ref[b, :, cs] = g[rows - SUBLANES:]
            y = jnp.concatenate(ys, axis=0)
        a = (y * jax.nn.sigmoid(y) * u).astype(BF16)
        part = jnp.dot(a, wd_ref[cs, :], preferred_element_type=F32)
        d = part if d is None else d + part

    o_ref[...] += d


def ffn_fused(x, norm_gain, hist, w_up, conv_w, w_down, layer, shape, *, tm=1024, tf=512):
    bsz, t, d = shape
    m = bsz * t
    d_ff = w_down.shape[1]
    nc = d_ff // tf
    hist_p = _pad_hist(hist, SUBLANES)
    if t >= tm:
        n_seq, tiles_per_seq, halo = 1, t // tm, BF16_SUBLANES
        seq_blk = lambda i: i // tiles_per_seq
    else:
        tm, n_seq, tiles_per_seq, halo = m, bsz, 1, 0
        seq_blk = lambda i: 0
    per = tm // BF16_SUBLANES
    hist_spec = pl.BlockSpec((n_seq, SUBLANES, tf), lambda i, c: (seq_blk(i), 0, c))
    x_new, g_tail = pl.pallas_call(
        functools.partial(_ffn_kernel, n_seq=n_seq, tiles_per_seq=tiles_per_seq, sub=min(tf, 256)),
        out_shape=(_sds((m, d), F32), _sds((bsz * tiles_per_seq, SUBLANES, d_ff), F32)),
        grid=(m // tm, nc),
        in_specs=[pl.BlockSpec((tm, d), lambda i, c: (i, 0), pipeline_mode=pl.Buffered(1)),
                  pl.BlockSpec((BF16_SUBLANES, d), lambda i, c: (jnp.maximum(i * per - 1, 0), 0)),
                  hist_spec,
                  pl.BlockSpec((1, d), lambda i, c: (0, 0)),
                  pl.BlockSpec((None, d, tf), lambda i, c: (layer, 0, c)),
                  pl.BlockSpec((None, d, tf), lambda i, c: (layer, 0, nc + c)),
                  pl.BlockSpec((None, FFN_CONV, tf), lambda i, c: (layer, 0, c)),
                  pl.BlockSpec((None, tf, d), lambda i, c: (layer, c, 0))],
        out_specs=(pl.BlockSpec((tm, d), lambda i, c: (i, 0)),
                   pl.BlockSpec((n_seq, SUBLANES, tf), lambda i, c: (i, 0, c))),
        scratch_shapes=[pltpu.VMEM((halo + tm, d), BF16)],
        compiler_params=_cp("parallel", "arbitrary"),
        name="ffn",
    )(x, x, hist_p, norm_gain.reshape(1, d), w_up, w_up, conv_w, w_down)
    return x_new, g_tail.reshape(bsz, tiles_per_seq, SUBLANES, d_ff)[:, -1]


def _alibi_slope(head):
    return 2.0 ** (-8.0 * (head + 1) / DSA_HEADS)


def _dsa_prompt_kernel(q_ref, kp_ref, kc_ref, vp_ref, vc_ref, o_ref, lse_ref, *, dil, group):
    i = pl.program_id(2)
    tq = Q_BLOCK
    q = q_ref[0].astype(BF16)
    k = jnp.concatenate([kp_ref[0], kc_ref[0]], axis=0).astype(BF16)
    v = jnp.concatenate([vp_ref[0], vc_ref[0]], axis=0).astype(BF16)
    row = lax.broadcasted_iota(jnp.int32, (tq, 2 * tq), 0)
    col = lax.broadcasted_iota(jnp.int32, (tq, 2 * tq), 1)
    dist = row + tq - col
    valid = (dist >= 0) & (dist <= DSA_KEYS) & ((col >= tq) | (i > 0))
    distf = dist.astype(F32) * float(dil)
    for h in range(DSA_HPG):
        sl = slice(h * DSA_HEAD_DIM, (h + 1) * DSA_HEAD_DIM)
        s = lax.dot_general(q[:, sl], k[:, sl], (((1,), (1,)), ((), ())), preferred_element_type=F32)
        s = s * (DSA_HEAD_DIM ** -0.5) - _alibi_slope(group * DSA_HPG + h) * distf
        s = jnp.where(valid, s, NEG)
        m = jnp.max(s, axis=-1, keepdims=True)
        p = jnp.exp(s - m)
        l = jnp.sum(p, axis=-1, keepdims=True)
        o = jnp.dot(p.astype(BF16), v[:, sl], preferred_element_type=F32) / l
        o_ref[0, :, sl] = o
        lse_ref[0, :, sl] = jnp.broadcast_to(m + jnp.log(l), (tq, DSA_HEAD_DIM))


def _dsa_dilated_kernel(q_ref, k_ref, v_ref, o_ref, lse_ref, *, dil, group):
    c = pl.program_id(1)
    tq = Q_BLOCK
    s_len = q_ref.shape[1] // dil
    heads_per_block = LANES // DSA_HEAD_DIM

    def geometry(n_keys):
        row = lax.broadcasted_iota(jnp.int32, (tq, n_keys), 0)
        col = lax.broadcasted_iota(jnp.int32, (tq, n_keys), 1)
        dist = row + (n_keys - tq) - col
        return (dist >= 0) & (dist <= DSA_KEYS), dist.astype(F32) * float(dil)

    first_block, later_block = geometry(tq), geometry(2 * tq)
    slopes = []
    for hh in range(heads_per_block):
        head = (group * DSA_HPG + c * heads_per_block + hh + 1).astype(F32)
        slopes.append(jnp.exp2(jnp.full((1, 1), head, F32) * (-8.0 / DSA_HEADS)))
    for r in range(dil):
        rows = pl.ds(r, s_len, stride=dil)
        q = q_ref[0, rows, :].astype(BF16)
        k = k_ref[0, rows, :].astype(BF16)
        v = v_ref[0, rows, :].astype(BF16)
        o_blocks, lse_blocks = [], []
        for qb in range(s_len // tq):
            lo = max(qb - 1, 0) * tq
            valid, distf = later_block if qb else first_block
            o_heads, lse_heads = [], []
            for hh in range(heads_per_block):
                sl = slice(hh * DSA_HEAD_DIM, (hh + 1) * DSA_HEAD_DIM)
                s = lax.dot_general(q[qb * tq:(qb + 1) * tq, sl], k[lo:(qb + 1) * tq, sl],
                                    (((1,), (1,)), ((), ())), preferred_element_type=F32)
                s = s * (DSA_HEAD_DIM ** -0.5) - slopes[hh] * distf
                s = jnp.where(valid, s, NEG)
                m = jnp.max(s, axis=-1, keepdims=True)
                p = jnp.exp(s - m)
                l = jnp.sum(p, axis=-1, keepdims=True)
                o_heads.append(jnp.dot(p.astype(BF16), v[lo:(qb + 1) * tq, sl], preferred_element_type=F32) / l)
                lse_heads.append(jnp.broadcast_to(m + jnp.log(l), (tq, DSA_HEAD_DIM)))
            o_blocks.append(jnp.concatenate(o_heads, axis=1))
            lse_blocks.append(jnp.concatenate(lse_heads, axis=1))
        o_ref[0, rows, :] = jnp.concatenate(o_blocks, axis=0)
        lse_ref[0, rows, :] = jnp.concatenate(lse_blocks, axis=0)


def dsa_dilated_group(qkv, group):
    bsz, t, n = qkv.shape
    blocks_per_part = n // 3 // LANES
    blocks_per_group = DSA_GW // LANES

    def spec(part):
        return pl.BlockSpec((1, t, LANES), lambda b, c: (b, 0, part * blocks_per_part + group * blocks_per_group + c))

    out_spec = pl.BlockSpec((1, t, LANES), lambda b, c: (b, 0, c))
    return pl.pallas_call(
        functools.partial(_dsa_dilated_kernel, dil=DSA_DILATIONS[group], group=group),
        out_shape=(_sds((bsz, t, DSA_GW), F32), _sds((bsz, t, DSA_GW), F32)),
        grid=(bsz, blocks_per_group),
        in_specs=[spec(0), spec(1), spec(2)],
        out_specs=(out_spec, out_spec),
        compiler_params=_cp("parallel", "parallel"),
        name=f"dsa_prompt_g{group}",
    )(qkv, qkv, qkv)


def dsa_prompt_group(qkv, group):
    bsz, t, n = qkv.shape
    dil = DSA_DILATIONS[group]
    if dil > 1:
        return dsa_dilated_group(qkv, group)
    s_len = t // dil
    nblk = n // DSA_GW
    view = qkv.reshape(bsz, s_len, dil * n)
    tq = Q_BLOCK

    def spec(part, prev):
        if prev:
            return pl.BlockSpec((1, tq, DSA_GW),
                                lambda b, r, i: (b, jnp.maximum(i - 1, 0), r * nblk + part * DSA_GROUPS + group))
        return pl.BlockSpec((1, tq, DSA_GW), lambda b, r, i: (b, i, r * nblk + part * DSA_GROUPS + group))

    out_spec = pl.BlockSpec((1, tq, DSA_GW), lambda b, r, i: (b, i, r))
    o, lse = pl.pallas_call(
        functools.partial(_dsa_prompt_kernel, dil=dil, group=group),
        out_shape=(_sds((bsz, s_len, dil * DSA_GW), F32), _sds((bsz, s_len, dil * DSA_GW), F32)),
        grid=(bsz, dil, s_len // tq),
        in_specs=[spec(0, False), spec(1, True), spec(1, False), spec(2, True), spec(2, False)],
        out_specs=(out_spec, out_spec),
        compiler_params=_cp("parallel", "parallel", "arbitrary"),
        name=f"dsa_prompt_g{group}",
    )(view, view, view, view, view)
    return o.reshape(bsz, t, DSA_GW), lse.reshape(bsz, t, DSA_GW)


def _dsa_sample_kernel(q_ref, kc_ref, vc_ref, kn_ref, vn_ref, o_ref, lse_ref, *, cache_len, dil, window, group):
    t = q_ref.shape[1]
    pad = jnp.zeros((LANES - t, DSA_GW), F32)
    q = q_ref[0].astype(BF16)
    k = jnp.concatenate([kc_ref[0], kn_ref[0], pad], axis=0).astype(BF16)
    v = jnp.concatenate([vc_ref[0], vn_ref[0], pad], axis=0).astype(BF16)
    rows = cache_len + LANES
    qi = lax.broadcasted_iota(jnp.int32, (t, rows), 0)
    col = lax.broadcasted_iota(jnp.int32, (t, rows), 1)
    dist = cache_len + qi - col
    valid = (dist >= 0) & (dist <= window) & ((dist & (dil - 1)) == 0)
    distf = dist.astype(F32)
    for h in range(DSA_HPG):
        sl = slice(h * DSA_HEAD_DIM, (h + 1) * DSA_HEAD_DIM)
        s = lax.dot_general(q[:, sl], k[:, sl], (((1,), (1,)), ((), ())), preferred_element_type=F32)
        s = s * (DSA_HEAD_DIM ** -0.5) - _alibi_slope(group * DSA_HPG + h) * distf
        s = jnp.where(valid, s, NEG)
        m = jnp.max(s, axis=-1, keepdims=True)
        p = jnp.exp(s - m)
        l = jnp.sum(p, axis=-1, keepdims=True)
        o = jnp.dot(p.astype(BF16), v[:, sl], preferred_element_type=F32) / l
        o_ref[0, :, sl] = o
        lse_ref[0, :, sl] = jnp.broadcast_to(m + jnp.log(l), (t, DSA_HEAD_DIM))


def dsa_sample_group(qkv, cache, group):
    bsz, t, _ = qkv.shape
    cache_len = cache.shape[1]
    cview = cache.reshape(bsz, cache_len, 2 * DSA_GW)
    new_spec = lambda part: pl.BlockSpec((1, t, DSA_GW), lambda b: (b, 0, part * DSA_GROUPS + group))
    cache_spec = lambda part: pl.BlockSpec((1, cache_len, DSA_GW), lambda b: (b, 0, part))
    out_spec = pl.BlockSpec((1, t, DSA_GW), lambda b: (b, 0, 0))
    return pl.pallas_call(
        functools.partial(_dsa_sample_kernel, cache_len=cache_len, dil=DSA_DILATIONS[group],
                          window=DSA_WINDOWS[group], group=group),
        out_shape=(_sds((bsz, t, DSA_GW), F32), _sds((bsz, t, DSA_GW), F32)),
        grid=(bsz,),
        in_specs=[new_spec(0), cache_spec(0), cache_spec(1), new_spec(1), new_spec(2)],
        out_specs=(out_spec, out_spec),
        compiler_params=_cp("parallel"),
        name=f"dsa_sample_g{group}",
    )(qkv, cview, cview, qkv, qkv)


def _dsa_combine_kernel(o0, o1, o2, l0, l1, l2, out_ref):
    ls = (l0[...], l1[...], l2[...])
    m = jnp.maximum(jnp.maximum(ls[0], ls[1]), ls[2])
    es = [jnp.exp(x - m) for x in ls]
    den = es[0] + es[1] + es[2]
    num = es[0] * o0[...] + es[1] * o1[...] + es[2] * o2[...]
    out_ref[...] = (num / den).astype(out_ref.dtype)


def dsa_combine(outs, lses):
    m, n = outs[0].shape
    tm = min(m, 1024)
    spec = pl.BlockSpec((tm, n), lambda i: (i, 0))
    return pl.pallas_call(
        _dsa_combine_kernel, out_shape=_sds((m, n), BF16), grid=(m // tm,),
        in_specs=[spec] * 6, out_specs=spec, compiler_params=_cp("parallel"), name="dsa_combine",
    )(*outs, *lses)


def _split_bf16(x):
    hi = x.astype(BF16)
    lo = (x - hi.astype(F32)).astype(BF16)
    return hi, lo


def _tri_sum(x, tri):
    hi, lo = _split_bf16(x)
    return (jnp.dot(hi, tri, preferred_element_type=F32) + jnp.dot(lo, tri, preferred_element_type=F32))


def _cumsum_kernel(x_ref, o_ref, *, tn):
    j = pl.program_id(0)
    t = x_ref.shape[1]
    src = lax.broadcasted_iota(jnp.int32, (t, tn), 0)
    dst = lax.broadcasted_iota(jnp.int32, (t, tn), 1) + j * tn
    o_ref[...] = _tri_sum(x_ref[...], jnp.where(src <= dst, 1.0, 0.0).astype(BF16))


def cumsum_lanes(x):
    r, t = x.shape
    tn = min(t, 512)
    return pl.pallas_call(
        functools.partial(_cumsum_kernel, tn=tn), out_shape=_sds((r, t), F32), grid=(t // tn,),
        in_specs=[pl.BlockSpec((r, t), lambda j: (0, 0))],
        out_specs=pl.BlockSpec((r, tn), lambda j: (0, j)),
        compiler_params=_cp("arbitrary"), name="fox_cumsum",
    )(x)


def _fox_prompt_kernel(q_ref, k_ref, v_ref, fk_ref, o_ref, *, scale):
    i = pl.program_id(2)
    tq = q_ref.shape[1]
    for ib in range(k_ref.shape[1] // tq):
        @pl.when(i == ib)
        def _(ib=ib):
            nk = (ib + 1) * tq
            s = lax.dot_general(q_ref[0], k_ref[0, :nk, :], (((1,), (1,)), ((), ())),
                                preferred_element_type=F32) * scale - fk_ref[0, :, :nk]
            row = lax.broadcasted_iota(jnp.int32, (tq, nk), 0) + ib * tq
            col = lax.broadcasted_iota(jnp.int32, (tq, nk), 1)
            s = jnp.where(col <= row, s, NEG)
            m = jnp.max(s, axis=-1, keepdims=True)
            p = jnp.exp(s - m)
            l = jnp.sum(p, axis=-1, keepdims=True)
            o = jnp.dot(p.astype(BF16), v_ref[0, :nk, :], preferred_element_type=F32)
            o_ref[0] = (o / l).astype(o_ref.dtype)


def fox_prompt_attention(qkv, f_cum, *, tq=256):
    bsz, t, n = qkv.shape
    d = n // 3
    hd = d // FOX_HEADS
    return pl.pallas_call(
        functools.partial(_fox_prompt_kernel, scale=hd ** -0.5),
        out_shape=_sds((bsz, t, d), BF16),
        grid=(bsz, FOX_HEADS, t // tq),
        in_specs=[pl.BlockSpec((1, tq, hd), lambda b, h, i: (b, i, h)),
                  pl.BlockSpec((1, t, hd), lambda b, h, i: (b, 0, FOX_HEADS + h)),
                  pl.BlockSpec((1, t, hd), lambda b, h, i: (b, 0, 2 * FOX_HEADS + h)),
                  pl.BlockSpec((1, 1, t), lambda b, h, i: (b * FOX_HEADS + h, 0, 0))],
        out_specs=pl.BlockSpec((1, tq, hd), lambda b, h, i: (b, i, h)),
        compiler_params=_cp("parallel", "parallel", "arbitrary"),
        name="fox_prompt_attn",
    )(qkv, qkv, qkv, f_cum)


def _fox_sample_kernel(pt_ref, q_ref, nkv_ref, nlf_ref, *rest, scale, t_new, heads, pps):
    kv_refs, lf_refs = rest[:pps], rest[pps:2 * pps]
    o_ref, pad_sc, m_sc, l_sc, acc_sc, suf_sc = rest[2 * pps:]
    j = pl.program_id(1)
    page = nlf_ref.shape[2]
    hd = q_ref.shape[2] // heads
    r = heads * t_new
    key_rows = 2 * heads
    pitch = pad_sc.shape[1] // page
    src = lax.broadcasted_iota(jnp.int32, (page, page), 0)
    dst = lax.broadcasted_iota(jnp.int32, (page, page), 1)

    def attend(kv_ref, slot, bias, valid, first):
        for key in range(page):
            pad_sc[slot, key * pitch:key * pitch + key_rows, :] = kv_ref[0, key * key_rows:(key + 1) * key_rows, :]
        rows_ref, row_pitch = pad_sc.at[slot], pitch
        s_rows = []
        for h in range(heads):
            k_h = rows_ref[pl.ds(h, page, stride=row_pitch), :].astype(BF16)
            s_h = lax.dot_general(q_ref[0, :, h * hd:(h + 1) * hd], k_h, (((1,), (1,)), ((), ())),
                                  preferred_element_type=F32)
            s_rows.append(s_h * scale + bias[h:h + 1, :])
        s = jnp.concatenate(s_rows, axis=0)
        if valid is not None:
            s = jnp.where(valid, s, NEG)
        m_cur = jnp.max(s, axis=-1, keepdims=True)
        m_new = m_cur if first else jnp.maximum(m_sc[...], m_cur)
        p = jnp.exp(s - m_new)
        pv = jnp.concatenate(
            [jnp.dot(p[h * t_new:(h + 1) * t_new].astype(BF16),
                     rows_ref[pl.ds(heads + h, page, stride=row_pitch), :].astype(BF16),
                     preferred_element_type=F32) for h in range(heads)], axis=0)
        l_cur = jnp.sum(p, axis=-1, keepdims=True)
        if first:
            l_sc[...] = l_cur
            acc_sc[...] = pv
        else:
            c = jnp.exp(m_sc[...] - m_new)
            l_sc[...] = l_sc[...] * c + l_cur
            acc_sc[...] = acc_sc[...] * c + pv
        m_sc[...] = m_new

    @pl.when(j == 0)
    def _():
        fq = _tri_sum(nlf_ref[0], jnp.where(src <= dst, 1.0, 0.0).astype(BF16))
        tq = lax.broadcasted_iota(jnp.int32, (r, page), 0) % t_new
        tk = lax.broadcasted_iota(jnp.int32, (r, page), 1)
        attend(nkv_ref, 0, -fq, tk <= tq, True)
        suf_sc[...] = jnp.zeros_like(suf_sc)

    @pl.when(j > 0)
    def _():
        later_mask = jnp.where(src > dst, 1.0, 0.0).astype(BF16)
        for u in range(pps):
            lf = lf_refs[u][0]
            suf = suf_sc[...]
            attend(kv_refs[u], u, _tri_sum(lf, later_mask) + suf, None, False)
            suf_sc[...] = suf + jnp.sum(lf, axis=-1, keepdims=True)

    @pl.when(j == pl.num_programs(1) - 1)
    def _():
        inv = 1.0 / l_sc[...]
        for h in range(heads):
            rs = slice(h * t_new, (h + 1) * t_new)
            o_ref[0, :, h * hd:(h + 1) * hd] = (acc_sc[rs, :] * inv[rs]).astype(o_ref.dtype)


def fox_sample_attention(qkv, kv_new, logf, kv_pool, logf_pool, page_table):
    bsz, t, n = qkv.shape
    d = n // 3
    heads = FOX_HEADS
    hd = d // heads
    n_pool, page = kv_pool.shape[0], kv_pool.shape[1]
    n_pages = page_table.shape[1]
    pps = FOX_PAGES_PER_STEP
    n_steps = n_pages // pps
    key_rows = 2 * heads
    rows = page * key_rows
    pool = kv_pool.reshape(n_pool, rows, hd)
    pool_lf = logf_pool.transpose(0, 2, 1)
    new_kv = jnp.pad(kv_new.reshape(bsz, t, key_rows, hd), ((0, 0), (0, page - t), (0, 0), (0, 0)))
    new_kv = new_kv.reshape(bsz, rows, hd)
    new_lf = jnp.pad(logf.transpose(0, 2, 1), ((0, 0), (0, 0), (0, page - t)))

    def page_idx(u):
        def index_map(b, j, pt_ref):
            return (pt_ref[b * n_pages + n_pages - 1 - (jnp.maximum(j, 1) - 1) * pps - u], 0, 0)
        return index_map

    own = lambda b, j, pt_ref: (b, 0, 0)
    grid_spec = pltpu.PrefetchScalarGridSpec(
        num_scalar_prefetch=1,
        grid=(bsz, n_steps + 1),
        in_specs=[pl.BlockSpec((1, t, d), own),
                  pl.BlockSpec((1, rows, hd), own),
                  pl.BlockSpec((1, heads, page), own)]
                 + [pl.BlockSpec((1, rows, hd), page_idx(u)) for u in range(pps)]
                 + [pl.BlockSpec((1, heads, page), page_idx(u)) for u in range(pps)],
        out_specs=pl.BlockSpec((1, t, d), own),
        scratch_shapes=[pltpu.VMEM((pps, page * (key_rows + FOX_PITCH_PAD), hd), F32),
                        pltpu.VMEM((heads * t, 1), F32), pltpu.VMEM((heads * t, 1), F32),
                        pltpu.VMEM((heads * t, hd), F32), pltpu.VMEM((heads, 1), F32)],
    )
    return pl.pallas_call(
        functools.partial(_fox_sample_kernel, scale=hd ** -0.5, t_new=t, heads=heads, pps=pps),
        out_shape=_sds((bsz, t, d), BF16),
        grid_spec=grid_spec,
        compiler_params=_cp("parallel", "arbitrary"),
        name="fox_sample_attn",
    )(page_table.reshape(-1), qkv, new_kv, new_lf, *([pool] * pps), *([pool_lf] * pps))


def _flat(x):
    return x.reshape(-1, x.shape[-1])


def _last_rows(hist, new, n):
    t = new.shape[1]
    if t >= n:
        return new[:, t - n:]
    return jnp.concatenate([hist.astype(new.dtype)[:, t:], new], axis=1)


def _conv_tiles(bsz, t):
    return dict(tm=256, nb=1) if t >= 256 else dict(tm=t, nb=bsz)


def _mixer_a(x, gain, hist, w_in, conv_w, w_out, shape):
    bsz, t, d = shape
    m = bsz * t
    tm = min(1024, m)
    gb, u = matmul(x, w_in, tm=tm, tn=512, parts=3, epilogue=_ep_gate_split, norm_gain=gain,
                   out_shapes=(_sds((m, d), F32), _sds((m, d), F32)),
                   out_specs=(_tile_spec(tm, 512), _tile_spec(tm, 512)), name="a_in")
    u3 = u.reshape(bsz, t, d)
    a = causal_conv_fused(u3, hist, conv_w, taps=A_CONV, gate=gb.reshape(bsz, t, d), tc=512,
                          **_conv_tiles(bsz, t))
    x = matmul_residual(_flat(a), w_out, x, name="a_out")
    return x, _last_rows(hist, u3, A_CONV - 1)


def _dsa_kv_rows(qkv3, group, n_last):
    qkv3 = qkv3[:, qkv3.shape[1] - n_last:]
    bsz, t, _ = qkv3.shape
    k0 = DSA_GROUPS * DSA_GW + group * DSA_GW
    v0 = 2 * DSA_GROUPS * DSA_GW + group * DSA_GW
    k = qkv3[:, :, k0:k0 + DSA_GW].reshape(bsz, t, DSA_HPG, DSA_HEAD_DIM)
    v = qkv3[:, :, v0:v0 + DSA_GW].reshape(bsz, t, DSA_HPG, DSA_HEAD_DIM)
    return jnp.stack([k, v], axis=2)


def _mixer_b(x, gain, caches, w_qkv, qn, kn, w_out, shape):
    bsz, t, d = shape
    (qkv,) = matmul_qk_norm(x, gain, w_qkv, qn, kn, head_dim=DSA_HEAD_DIM, name="b_qkv")
    qkv3 = qkv.reshape(bsz, t, -1)
    if caches is None:
        outs, lses = zip(*[dsa_prompt_group(qkv3, g) for g in range(DSA_GROUPS)])
        bufs = [_dsa_kv_rows(qkv3, g, min(DSA_WINDOWS[g], t)) for g in range(DSA_GROUPS)]
    else:
        outs, lses = zip(*[dsa_sample_group(qkv3, caches[g], g) for g in range(DSA_GROUPS)])
        bufs = [jnp.concatenate([caches[g][:, t:], _dsa_kv_rows(qkv3, g, t)], axis=1) for g in range(DSA_GROUPS)]
    o = dsa_combine([_flat(a) for a in outs], [_flat(a) for a in lses])
    x = matmul_residual(o, w_out, x, name="b_out")
    return x, bufs


def _mixer_c(x, gain, hist, w_in, conv_w, ln_g, ln_b, w_out, shape):
    bsz, t, d = shape
    m = bsz * t
    tm = min(1024, m)
    u = matmul(x, w_in, tm=tm, tn=512, parts=2, epilogue=_ep_glu, norm_gain=gain,
               out_shapes=_sds((m, d), F32), out_specs=_tile_spec(tm, 512), name="c_in")
    u3 = u.reshape(bsz, t, d)
    tiles = _conv_tiles(bsz, t)
    a = causal_conv_fused(u3, hist, conv_w, taps=C_CONV, ln=(ln_g, ln_b), tm=min(tiles["tm"], 128), tc=d,
                          nb=tiles["nb"])
    x = matmul_residual(_flat(a), w_out, x, name="c_out")
    return x, _last_rows(hist, u3, C_CONV - 1)


def _mixer_d(x, gain, paged, w_qkv, w_f, b_f, qn, kn, w_out, shape):
    bsz, t, d = shape
    m = bsz * t
    tm = min(1024, m)
    qkv, kv = matmul_qk_norm(x, gain, w_qkv, qn, kn, head_dim=d // FOX_HEADS, split_kv=True, name="d_qkv")
    logf = matmul(x, w_f, tm=tm, tn=LANES, epilogue=_ep_logf, norm_gain=gain, out_shapes=_sds((m, LANES), F32),
                  out_specs=_tile_spec(tm, LANES), extras=(b_f,),
                  extra_specs=(pl.BlockSpec((1, LANES), lambda i, j: (0, 0)),), name="d_logf")
    logf = logf[:, :FOX_HEADS].reshape(bsz, t, FOX_HEADS)
    qkv3 = qkv.reshape(bsz, t, 3 * d)
    kv3 = kv.reshape(bsz, t, 2 * d)
    if paged is None:
        f_cum = cumsum_lanes(logf.transpose(0, 2, 1).reshape(bsz * FOX_HEADS, t))
        o = fox_prompt_attention(qkv3, f_cum.reshape(bsz * FOX_HEADS, 1, t))
    else:
        o = fox_sample_attention(qkv3, kv3, logf, *paged)
    x = matmul_residual(_flat(o), w_out, x, name="d_out")
    return x, kv3.reshape(bsz, t, 2, FOX_HEADS, d // FOX_HEADS), logf


def kernel(x_prompt, x_sample, state_conv_a, cache_dsa_g0, cache_dsa_g1, cache_dsa_g2, state_conv_c,
           cache_fox_kv, cache_fox_logf, state_ffn_conv, page_table, p_prompt, p_sample,
           norm_mix, norm_ffn, w_a_in, conv_a_w, w_a_out, w_b_qkv, qn_b, kn_b, w_b_out,
           w_c_in, conv_c_w, ln_c_g, ln_c_b, w_c_out, w_d_qkvf, b_f, qn_d, kn_d, w_d_out,
           w_ffn_up, conv_ffn_w, w_ffn_down, norm_ple, w_ple_gate, w_ple_proj):
    depth = norm_mix.shape[0]
    d = x_prompt.shape[-1]
    d_ff = w_ffn_down.shape[1]
    shapes = (x_prompt.shape, x_sample.shape)
    xs = [_flat(x_prompt), _flat(x_sample)]
    ps = tuple(p.reshape(depth, -1, p.shape[-1]) for p in (p_prompt, p_sample))
    bf = lambda w: w.astype(BF16)
    w_up_bf, w_down_bf, w_gate_bf, w_proj_bf = bf(w_ffn_up), bf(w_ffn_down), bf(w_ple_gate), bf(w_ple_proj)

    w_d_qkv = bf(w_d_qkvf[:, :3 * d])
    n_f = w_d_qkvf.shape[1] - 3 * d
    w_d_f = bf(jnp.pad(w_d_qkvf[:, 3 * d:], ((0, 0), (0, LANES - n_f))))
    b_f_pad = jnp.pad(b_f.astype(F32), (0, LANES - n_f)).reshape(1, LANES)
    caches = (cache_dsa_g0, cache_dsa_g1, cache_dsa_g2)
    paged = (cache_fox_kv, cache_fox_logf, page_table)

    mixer_out = [dict(), dict()]
    ffn_hist = [[], []]
    for i in range(depth):
        kind = i % 4
        for grp in range(2):
            shape = shapes[grp]
            bsz, t, _ = shape
            x = xs[grp]
            out = mixer_out[grp]
            gain = norm_mix[i]
            if kind == 0:
                hist = jnp.zeros((bsz, A_CONV - 1, d), F32) if grp == 0 else state_conv_a
                x, out["conv_a"] = _mixer_a(x, gain, hist, bf(w_a_in), conv_a_w, bf(w_a_out), shape)
            elif kind == 1:
                x, out["dsa"] = _mixer_b(x, gain, None if grp == 0 else caches, bf(w_b_qkv), qn_b, kn_b,
                                         bf(w_b_out), shape)
            elif kind == 2:
                hist = jnp.zeros((bsz, C_CONV - 1, d), F32) if grp == 0 else state_conv_c
                x, out["conv_c"] = _mixer_c(x, gain, hist, bf(w_c_in), conv_c_w, ln_c_g, ln_c_b, bf(w_c_out), shape)
            else:
                x, out["fox_kv"], out["fox_logf"] = _mixer_d(
                    x, gain, None if grp == 0 else paged, w_d_qkv, w_d_f, b_f_pad, qn_d, kn_d, bf(w_d_out), shape)
            hist = jnp.zeros((bsz, FFN_CONV - 1, d_ff), F32) if grp == 0 else state_ffn_conv[i]
            x, g_last = ffn_fused(x, norm_ffn[i], hist, w_up_bf, conv_ffn_w, w_down_bf, i, shape)
            ffn_hist[grp].append(_last_rows(hist, g_last[:, SUBLANES - min(t, SUBLANES):], FFN_CONV - 1))
            xs[grp] = per_layer_embed(x, norm_ple[i], ps[grp], w_gate_bf, w_proj_bf, i)

    mp, ms = mixer_out
    return (xs[0].reshape(shapes[0]), xs[1].reshape(shapes[1]),
            mp["conv_a"], ms["conv_a"],
            mp["dsa"][0], ms["dsa"][0], mp["dsa"][1], ms["dsa"][1], mp["dsa"][2], ms["dsa"][2],
            mp["conv_c"], ms["conv_c"],
            mp["fox_kv"], ms["fox_kv"], mp["fox_logf"], ms["fox_logf"],
            jnp.stack(ffn_hist[0]), jnp.stack(ffn_hist[1]))
```

```python
import functools

import jax
import jax.numpy as jnp
from jax import lax
from jax.experimental import pallas as pl
from jax.experimental.pallas import tpu as pltpu

F32 = jnp.float32
BF16 = jnp.bfloat16
EPS = 1e-6
NEG = -1e30

VMEM_LIMIT_BYTES = 56 * 1024 * 1024
FFN_VMEM_LIMIT_BYTES = 58 * 1024 * 1024
LANES = 128
SUBLANES = 8
BF16_SUBLANES = 16
MXU_COLS = 256

A_CONV = 3
C_CONV = 31
FFN_CONV = 3
DSA_WINDOWS = (128, 512, 2048)
DSA_DILATIONS = (1, 4, 16)
DSA_GROUPS = 3
DSA_HPG = 8
DSA_HEAD_DIM = 64
DSA_HEADS = DSA_GROUPS * DSA_HPG
DSA_GW = DSA_HPG * DSA_HEAD_DIM
DSA_KEYS = 128
FOX_HEADS = 16
Q_BLOCK = 128
FOX_PAGES_PER_STEP = 4
FOX_PITCH_PAD = 8


def _cp(*sem, vmem_limit_bytes=VMEM_LIMIT_BYTES):
    return pltpu.CompilerParams(dimension_semantics=sem, vmem_limit_bytes=vmem_limit_bytes)


def _sds(shape, dtype):
    return jax.ShapeDtypeStruct(shape, dtype)


def _rmsnorm(x, g):
    ms = jnp.mean(x * x, axis=-1, keepdims=True)
    return x * lax.rsqrt(ms + EPS) * g


def _mm_body(*refs, parts, n_extra, epilogue, norm):
    if norm:
        x_ref, g_ref, a_sc = refs[0], refs[1], refs[-1]
        refs = refs[2:-1]

        @pl.when(pl.program_id(1) == 0)
        def _():
            a_sc[...] = _rmsnorm(x_ref[...], g_ref[...]).astype(BF16)

        a = a_sc[...]
    else:
        a = refs[0][...]
        refs = refs[1:]
    extras = refs[parts:parts + n_extra]
    outs = refs[parts + n_extra:]

    def dot(cols):
        return [jnp.dot(a, refs[p][:, cols], preferred_element_type=F32) for p in range(parts)]

    epilogue(dot, extras, outs)


def _col_chunks(tn):
    step = min(tn, MXU_COLS)
    return [slice(c0, c0 + step) for c0 in range(0, tn, step)]


def matmul(a, w, *, tm, tn, epilogue, out_shapes, out_specs, parts=1, extras=(), extra_specs=(),
           norm_gain=None, name):
    m, k = a.shape
    np_ = w.shape[1] // parts
    nj = np_ // tn
    tm = min(tm, m)
    norm = norm_gain is not None
    in_specs = [pl.BlockSpec((tm, k), lambda i, j: (i, 0))]
    args = [a]
    if norm:
        in_specs.append(pl.BlockSpec((1, k), lambda i, j: (0, 0)))
        args.append(norm_gain.reshape(1, k))
    in_specs += [pl.BlockSpec((k, tn), functools.partial(lambda i, j, p: (0, p * nj + j), p=p))
                 for p in range(parts)]
    in_specs += list(extra_specs)
    return pl.pallas_call(
        functools.partial(_mm_body, parts=parts, n_extra=len(extras), epilogue=epilogue, norm=norm),
        out_shape=out_shapes,
        grid=(m // tm, nj),
        in_specs=in_specs,
        out_specs=out_specs,
        scratch_shapes=[pltpu.VMEM((tm, k), BF16)] if norm else [],
        compiler_params=_cp("parallel", "arbitrary"),
        name=name,
    )(*args, *([w] * parts), *extras)


def _tile_spec(tm, tn):
    return pl.BlockSpec((tm, tn), lambda i, j: (i, j))


def _ep_residual(dot, extras, outs):
    for cols in _col_chunks(outs[0].shape[1]):
        outs[0][:, cols] = extras[0][:, cols] + dot(cols)[0]


def matmul_residual(a, w, res, *, tm=512, name):
    m, n = a.shape[0], w.shape[1]
    tm, tn = min(tm, m), n
    return matmul(a, w, tm=tm, tn=tn, epilogue=_ep_residual, out_shapes=_sds((m, n), F32),
                  out_specs=_tile_spec(tm, tn), extras=(res,), extra_specs=(_tile_spec(tm, tn),), name=name)


def _ep_gate_split(dot, extras, outs):
    for cols in _col_chunks(outs[0].shape[1]):
        gb, gc, v = dot(cols)
        outs[0][:, cols] = gb
        outs[1][:, cols] = gc * v


def _ep_glu(dot, extras, outs):
    for cols in _col_chunks(outs[0].shape[1]):
        a, g = dot(cols)
        outs[0][:, cols] = a * jax.nn.sigmoid(g)


def _ep_qk_norm(dot, extras, outs, *, head_dim, blocks_per_part, split_kv):
    j = pl.program_id(1)
    gain = extras[0][0]
    tn = outs[0].shape[1]

    def store(cols, val, with_kv):
        outs[0][:, cols] = val.astype(outs[0].dtype)
        if split_kv and with_kv:
            outs[1][:, cols] = val

    def tile(norm, with_kv):
        for chunk in _col_chunks(tn):
            acc = dot(chunk)[0]
            if not norm:
                store(chunk, acc, with_kv)
                continue
            for c0 in range(0, acc.shape[1], LANES):
                cols = slice(chunk.start + c0, chunk.start + c0 + LANES)
                x = acc[:, c0:c0 + LANES]
                sq = x * x
                if head_dim == LANES:
                    ms = jnp.sum(sq, axis=-1, keepdims=True) * (1.0 / head_dim)
                else:
                    lane = lax.broadcasted_iota(jnp.int32, sq.shape, 1)
                    first = lane < head_dim
                    s_lo = jnp.sum(jnp.where(first, sq, 0.0), axis=-1, keepdims=True)
                    s_hi = jnp.sum(jnp.where(first, 0.0, sq), axis=-1, keepdims=True)
                    ms = jnp.where(first, s_lo, s_hi) * (1.0 / head_dim)
                store(cols, x * lax.rsqrt(ms + EPS) * gain[:, cols], with_kv)

    @pl.when(j < blocks_per_part)
    def _():
        tile(True, False)

    @pl.when((j >= blocks_per_part) & (j < 2 * blocks_per_part))
    def _():
        tile(True, True)

    @pl.when(j >= 2 * blocks_per_part)
    def _():
        tile(False, True)


def matmul_qk_norm(x, norm_gain, w, qn, kn, *, head_dim, split_kv=False, tm=1024, tn=512, name):
    m, n = x.shape[0], w.shape[1]
    tm = min(tm, m)
    bpp = n // 3 // tn
    reps = tn // head_dim
    gains = jnp.stack([jnp.tile(qn, reps), jnp.tile(kn, reps), jnp.ones((tn,), F32)]).reshape(3, 1, tn)
    gain_spec = pl.BlockSpec((1, 1, tn), lambda i, j: (j // bpp, 0, 0))
    if split_kv:
        shapes = (_sds((m, n), BF16), _sds((m, n - n // 3), F32))
        specs = (_tile_spec(tm, tn), pl.BlockSpec((tm, tn), lambda i, j: (i, jnp.maximum(j - bpp, 0))))
    else:
        shapes, specs = (_sds((m, n), F32),), (_tile_spec(tm, tn),)
    return matmul(x, w, tm=tm, tn=tn, norm_gain=norm_gain,
                  epilogue=functools.partial(_ep_qk_norm, head_dim=head_dim, blocks_per_part=bpp, split_kv=split_kv),
                  out_shapes=shapes, out_specs=specs, extras=(gains,), extra_specs=(gain_spec,), name=name)


def _log_sigmoid(x):
    return jnp.minimum(x, 0.0) - jnp.log1p(jnp.exp(-jnp.abs(x)))


def _ep_logf(dot, extras, outs):
    outs[0][...] = _log_sigmoid(dot(slice(None))[0] + extras[0][...])


def _ple_body(x_ref, g_ref, wg_ref, p_ref, wp_ref, o_ref, h_sc, *, tn):
    j = pl.program_id(1)

    @pl.when(j == 0)
    def _():
        h_sc[...] = _rmsnorm(x_ref[...], g_ref[...]).astype(BF16)

    h = h_sc[...]
    p = p_ref[...].astype(BF16)
    for cols in _col_chunks(tn):
        gate = jax.nn.sigmoid(jnp.dot(h, wg_ref[:, cols], preferred_element_type=F32))
        proj = jnp.dot(p, wp_ref[:, cols], preferred_element_type=F32)
        x = x_ref[:, pl.ds(pl.multiple_of(j * tn + cols.start, cols.stop - cols.start), cols.stop - cols.start)]
        o_ref[:, cols] = x + gate * proj


def per_layer_embed(x, norm_gain, p, w_gate, w_proj, layer, *, tm=1024, tn=1024):
    m, d = x.shape
    tm = min(tm, m)
    pd = p.shape[2]
    return pl.pallas_call(
        functools.partial(_ple_body, tn=tn),
        out_shape=_sds((m, d), F32),
        grid=(m // tm, d // tn),
        in_specs=[pl.BlockSpec((tm, d), lambda i, j: (i, 0)),
                  pl.BlockSpec((1, d), lambda i, j: (0, 0)),
                  pl.BlockSpec((None, d, tn), lambda i, j: (layer, 0, j)),
                  pl.BlockSpec((None, tm, pd), lambda i, j: (layer, i, 0)),
                  pl.BlockSpec((None, pd, tn), lambda i, j: (layer, 0, j))],
        out_specs=_tile_spec(tm, tn),
        scratch_shapes=[pltpu.VMEM((tm, d), BF16)],
        compiler_params=_cp("parallel", "arbitrary"),
        name="per_layer_embed",
    )(x, norm_gain.reshape(1, d), w_gate, p, w_proj)


def _causal_taps(prev, cur, w, taps):
    p_rows, tm = prev.shape[0], cur.shape[0]
    ext = jnp.concatenate([prev, cur], axis=0)
    n = p_rows + tm
    acc = None
    for rot in range(SUBLANES):
        ks = [k for k in range(taps) if (p_rows - (taps - 1) + k) % SUBLANES == rot]
        if not ks:
            continue
        shifted = ext if rot == 0 else pltpu.roll(ext, n - rot, axis=0)
        for k in ks:
            base = p_rows - (taps - 1) + k - rot
            term = w[k:k + 1, :] * shifted[base:base + tm, :]
            acc = term if acc is None else acc + term
    return acc


def _conv_gate_kernel(cur_ref, prev_ref, hist_ref, w_ref, gate_ref, o_ref, *, taps, chunk):
    at_start = pl.program_id(1) == 0
    nb, tm, tc = cur_ref.shape
    for b in range(nb):
        for c0 in range(0, tc, chunk):
            cs = slice(c0, c0 + chunk)
            prev = jnp.where(at_start, hist_ref[b, :, cs], prev_ref[b, :, cs])
            y = _causal_taps(prev, cur_ref[b, :, cs], w_ref[:, cs], taps)
            o_ref[b, :, cs] = (gate_ref[b, :, cs] * y).astype(o_ref.dtype)


def _conv_ln_kernel(cur_ref, prev_ref, hist_ref, w_ref, lng_ref, lnb_ref, o_ref, c_sc, *, taps, chunk):
    at_start = pl.program_id(1) == 0
    nb, tm, tc = cur_ref.shape
    for b in range(nb):
        for c0 in range(0, tc, chunk):
            cs = slice(c0, c0 + chunk)
            prev = jnp.where(at_start, hist_ref[b, :, cs], prev_ref[b, :, cs])
            c_sc[:, cs] = _causal_taps(prev, cur_ref[b, :, cs], w_ref[:, cs], taps)
        c = c_sc[...]
        mu = jnp.mean(c, axis=-1, keepdims=True)
        xc = c - mu
        var = jnp.mean(xc * xc, axis=-1, keepdims=True)
        y = xc * lax.rsqrt(var + EPS) * lng_ref[...] + lnb_ref[...]
        o_ref[b] = (y * jax.nn.sigmoid(y)).astype(o_ref.dtype)


def _pad_hist(hist, p_rows):
    b, k1, c = hist.shape
    return jnp.concatenate([jnp.zeros((b, p_rows - k1, c), F32), hist.astype(F32)], axis=1)


def causal_conv_fused(src, hist, w, *, taps, gate=None, ln=None, tm, tc, nb):
    bsz, t, c = src.shape
    p_rows = SUBLANES if taps - 1 <= SUBLANES else 32
    hist_p = _pad_hist(hist, p_rows)
    tm = min(tm, t)
    n_t = t // tm
    chunk = min(tc, 256)
    if t >= p_rows:
        prev_src = src
        per = tm // p_rows
        prev_spec = pl.BlockSpec((nb, p_rows, tc), lambda b, i, j: (b, jnp.maximum(i * per - 1, 0), j))
    else:
        prev_src = hist_p
        prev_spec = pl.BlockSpec((nb, p_rows, tc), lambda b, i, j: (b, 0, j))
    cur_spec = pl.BlockSpec((nb, tm, tc), lambda b, i, j: (b, i, j))
    hist_spec = pl.BlockSpec((nb, p_rows, tc), lambda b, i, j: (b, 0, j))
    w_spec = pl.BlockSpec((taps, tc), lambda b, i, j: (0, j))
    grid = (bsz // nb, n_t, c // tc)
    if ln is not None:
        lng, lnb = ln
        vec_spec = pl.BlockSpec((1, tc), lambda b, i, j: (0, j))
        return pl.pallas_call(
            functools.partial(_conv_ln_kernel, taps=taps, chunk=chunk),
            out_shape=_sds((bsz, t, c), BF16), grid=grid,
            in_specs=[cur_spec, prev_spec, hist_spec, w_spec, vec_spec, vec_spec],
            out_specs=cur_spec,
            scratch_shapes=[pltpu.VMEM((tm, tc), F32)],
            compiler_params=_cp("parallel", "arbitrary", "arbitrary"),
            name="conv_ln",
        )(src, prev_src, hist_p, w, lng.reshape(1, c), lnb.reshape(1, c))
    return pl.pallas_call(
        functools.partial(_conv_gate_kernel, taps=taps, chunk=chunk),
        out_shape=_sds((bsz, t, c), BF16), grid=grid,
        in_specs=[cur_spec, prev_spec, hist_spec, w_spec, cur_spec],
        out_specs=cur_spec,
        compiler_params=_cp("parallel", "arbitrary", "arbitrary"),
        name="conv_gate",
    )(src, prev_src, hist_p, w, gate)


def _ffn_kernel(x_ref, halo_ref, hist_ref, gain_ref, wg_ref, wu_ref, cw_ref, wd_ref, o_ref, gh_ref, h_sc,
                *, n_seq, tiles_per_seq, sub):
    i, c = pl.program_id(0), pl.program_id(1)
    tm = x_ref.shape[0]
    halo = h_sc.shape[0] - tm

    @pl.when(c == 0)
    def _():
        x = x_ref[...]
        o_ref[...] = x
        h_sc[halo:, :] = _rmsnorm(x, gain_ref[...]).astype(BF16)
        if halo:
            h_sc[:halo, :] = _rmsnorm(halo_ref[...], gain_ref[...]).astype(BF16)

    d = None
    for c0 in range(0, wg_ref.shape[1], sub):
        cs = slice(c0, c0 + sub)
        g_ext = jnp.dot(h_sc[...], wg_ref[:, cs], preferred_element_type=F32)
        u = jnp.dot(h_sc[halo:, :], wu_ref[:, cs], preferred_element_type=F32)
        cw = cw_ref[:, cs]
        if halo:
            at_start = (i % tiles_per_seq) == 0
            prev = jnp.where(at_start, hist_ref[0, :, cs], g_ext[halo - SUBLANES:halo])
            g = g_ext[halo:]
            y = _causal_taps(prev, g, cw, FFN_CONV)
            gh_ref[0, :, cs] = g[tm - SUBLANES:]
        else:
            rows = tm // n_seq
            ys = []
            for b in range(n_seq):
                g = g_ext[b * rows:(b + 1) * rows]
                ys.append(_causal_taps(hist_ref[b, :, cs], g, cw, FFN_CONV))
                gh_ref[b, :, cs] = g[rows - SUBLANES:]
            y = jnp.concatenate(ys, axis=0)
        a = (y * jax.nn.sigmoid(y) * u).astype(BF16)
        part = jnp.dot(a, wd_ref[cs, :], preferred_element_type=F32)
        d = part if d is None else d + part

    o_ref[...] += d


def ffn_fused(x, norm_gain, hist, w_up, conv_w, w_down, layer, shape, *, tm=1024, tf=512):
    bsz, t, d = shape
    m = bsz * t
    d_ff = w_down.shape[1]
    nc = d_ff // tf
    hist_p = _pad_hist(hist, SUBLANES)
    if t >= tm:
        n_seq, tiles_per_seq, halo = 1, t // tm, BF16_SUBLANES
        seq_blk = lambda i: i // tiles_per_seq
    else:
        tm, n_seq, tiles_per_seq, halo = m, bsz, 1, 0
        seq_blk = lambda i: 0
    per = tm // BF16_SUBLANES
    hist_spec = pl.BlockSpec((n_seq, SUBLANES, tf), lambda i, c: (seq_blk(i), 0, c))
    x_new, g_tail = pl.pallas_call(
        functools.partial(_ffn_kernel, n_seq=n_seq, tiles_per_seq=tiles_per_seq, sub=min(tf, 256)),
        out_shape=(_sds((m, d), F32), _sds((bsz * tiles_per_seq, SUBLANES, d_ff), F32)),
        grid=(m // tm, nc),
        in_specs=[pl.BlockSpec((tm, d), lambda i, c: (i, 0)),
                  pl.BlockSpec((BF16_SUBLANES, d), lambda i, c: (jnp.maximum(i * per - 1, 0), 0)),
                  hist_spec,
                  pl.BlockSpec((1, d), lambda i, c: (0, 0)),
                  pl.BlockSpec((None, d, tf), lambda i, c: (layer, 0, c)),
                  pl.BlockSpec((None, d, tf), lambda i, c: (layer, 0, nc + c)),
                  pl.BlockSpec((None, FFN_CONV, tf), lambda i, c: (layer, 0, c)),
                  pl.BlockSpec((None, tf, d), lambda i, c: (layer, c, 0))],
        out_specs=(pl.BlockSpec((tm, d), lambda i, c: (i, 0)),
                   pl.BlockSpec((n_seq, SUBLANES, tf), lambda i, c: (i, 0, c))),
        scratch_shapes=[pltpu.VMEM((halo + tm, d), BF16)],
        compiler_params=_cp("parallel", "arbitrary", vmem_limit_bytes=FFN_VMEM_LIMIT_BYTES),
        name="ffn",
    )(x, x, hist_p, norm_gain.reshape(1, d), w_up, w_up, conv_w, w_down)
    return x_new, g_tail.reshape(bsz, tiles_per_seq, SUBLANES, d_ff)[:, -1]


def _alibi_slope(head):
    return 2.0 ** (-8.0 * (head + 1) / DSA_HEADS)


def _dsa_prompt_kernel(q_ref, kp_ref, kc_ref, vp_ref, vc_ref, o_ref, lse_ref, *, dil, group):
    i = pl.program_id(2)
    tq = Q_BLOCK
    q = q_ref[0].astype(BF16)
    k = jnp.concatenate([kp_ref[0], kc_ref[0]], axis=0).astype(BF16)
    v = jnp.concatenate([vp_ref[0], vc_ref[0]], axis=0).astype(BF16)
    row = lax.broadcasted_iota(jnp.int32, (tq, 2 * tq), 0)
    col = lax.broadcasted_iota(jnp.int32, (tq, 2 * tq), 1)
    dist = row + tq - col
    valid = (dist >= 0) & (dist <= DSA_KEYS) & ((col >= tq) | (i > 0))
    distf = dist.astype(F32) * float(dil)
    for h in range(DSA_HPG):
        sl = slice(h * DSA_HEAD_DIM, (h + 1) * DSA_HEAD_DIM)
        s = lax.dot_general(q[:, sl], k[:, sl], (((1,), (1,)), ((), ())), preferred_element_type=F32)
        s = s * (DSA_HEAD_DIM ** -0.5) - _alibi_slope(group * DSA_HPG + h) * distf
        s = jnp.where(valid, s, NEG)
        m = jnp.max(s, axis=-1, keepdims=True)
        p = jnp.exp(s - m)
        l = jnp.sum(p, axis=-1, keepdims=True)
        o = jnp.dot(p.astype(BF16), v[:, sl], preferred_element_type=F32) / l
        o_ref[0, :, sl] = o
        lse_ref[0, :, sl] = jnp.broadcast_to(m + jnp.log(l), (tq, DSA_HEAD_DIM))


def _dsa_dilated_kernel(q_ref, k_ref, v_ref, o_ref, lse_ref, *, dil, group):
    c = pl.program_id(1)
    tq = Q_BLOCK
    s_len = q_ref.shape[1] // dil
    heads_per_block = LANES // DSA_HEAD_DIM

    def geometry(n_keys):
        row = lax.broadcasted_iota(jnp.int32, (tq, n_keys), 0)
        col = lax.broadcasted_iota(jnp.int32, (tq, n_keys), 1)
        dist = row + (n_keys - tq) - col
        return (dist >= 0) & (dist <= DSA_KEYS), dist.astype(F32) * float(dil)

    first_block, later_block = geometry(tq), geometry(2 * tq)
    slopes = []
    for hh in range(heads_per_block):
        head = (group * DSA_HPG + c * heads_per_block + hh + 1).astype(F32)
        slopes.append(jnp.exp2(jnp.full((1, 1), head, F32) * (-8.0 / DSA_HEADS)))
    for r in range(dil):
        rows = pl.ds(r, s_len, stride=dil)
        q = q_ref[0, rows, :].astype(BF16)
        k = k_ref[0, rows, :].astype(BF16)
        v = v_ref[0, rows, :].astype(BF16)
        o_blocks, lse_blocks = [], []
        for qb in range(s_len // tq):
            lo = max(qb - 1, 0) * tq
            valid, distf = later_block if qb else first_block
            o_heads, lse_heads = [], []
            for hh in range(heads_per_block):
                sl = slice(hh * DSA_HEAD_DIM, (hh + 1) * DSA_HEAD_DIM)
                s = lax.dot_general(q[qb * tq:(qb + 1) * tq, sl], k[lo:(qb + 1) * tq, sl],
                                    (((1,), (1,)), ((), ())), preferred_element_type=F32)
                s = s * (DSA_HEAD_DIM ** -0.5) - slopes[hh] * distf
                s = jnp.where(valid, s, NEG)
                m = jnp.max(s, axis=-1, keepdims=True)
                p = jnp.exp(s - m)
                l = jnp.sum(p, axis=-1, keepdims=True)
                o_heads.append(jnp.dot(p.astype(BF16), v[lo:(qb + 1) * tq, sl], preferred_element_type=F32) / l)
                lse_heads.append(jnp.broadcast_to(m + jnp.log(l), (tq, DSA_HEAD_DIM)))
            o_blocks.append(jnp.concatenate(o_heads, axis=1))
            lse_blocks.append(jnp.concatenate(lse_heads, axis=1))
        o_ref[0, rows, :] = jnp.concatenate(o_blocks, axis=0)
        lse_ref[0, rows, :] = jnp.concatenate(lse_blocks, axis=0)


def dsa_dilated_group(qkv, group):
    bsz, t, n = qkv.shape
    blocks_per_part = n // 3 // LANES
    blocks_per_group = DSA_GW // LANES

    def spec(part):
        return pl.BlockSpec((1, t, LANES), lambda b, c: (b, 0, part * blocks_per_part + group * blocks_per_group + c))

    out_spec = pl.BlockSpec((1, t, LANES), lambda b, c: (b, 0, c))
    return pl.pallas_call(
        functools.partial(_dsa_dilated_kernel, dil=DSA_DILATIONS[group], group=group),
        out_shape=(_sds((bsz, t, DSA_GW), F32), _sds((bsz, t, DSA_GW), F32)),
        grid=(bsz, blocks_per_group),
        in_specs=[spec(0), spec(1), spec(2)],
        out_specs=(out_spec, out_spec),
        compiler_params=_cp("parallel", "parallel"),
        name=f"dsa_prompt_g{group}",
    )(qkv, qkv, qkv)


def dsa_prompt_group(qkv, group):
    bsz, t, n = qkv.shape
    dil = DSA_DILATIONS[group]
    if dil > 1:
        return dsa_dilated_group(qkv, group)
    s_len = t // dil
    nblk = n // DSA_GW
    view = qkv.reshape(bsz, s_len, dil * n)
    tq = Q_BLOCK

    def spec(part, prev):
        if prev:
            return pl.BlockSpec((1, tq, DSA_GW),
                                lambda b, r, i: (b, jnp.maximum(i - 1, 0), r * nblk + part * DSA_GROUPS + group))
        return pl.BlockSpec((1, tq, DSA_GW), lambda b, r, i: (b, i, r * nblk + part * DSA_GROUPS + group))

    out_spec = pl.BlockSpec((1, tq, DSA_GW), lambda b, r, i: (b, i, r))
    o, lse = pl.pallas_call(
        functools.partial(_dsa_prompt_kernel, dil=dil, group=group),
        out_shape=(_sds((bsz, s_len, dil * DSA_GW), F32), _sds((bsz, s_len, dil * DSA_GW), F32)),
        grid=(bsz, dil, s_len // tq),
        in_specs=[spec(0, False), spec(1, True), spec(1, False), spec(2, True), spec(2, False)],
        out_specs=(out_spec, out_spec),
        compiler_params=_cp("parallel", "parallel", "arbitrary"),
        name=f"dsa_prompt_g{group}",
    )(view, view, view, view, view)
    return o.reshape(bsz, t, DSA_GW), lse.reshape(bsz, t, DSA_GW)


def _dsa_sample_kernel(q_ref, kc_ref, vc_ref, kn_ref, vn_ref, o_ref, lse_ref, *, cache_len, dil, window, group):
    t = q_ref.shape[1]
    pad = jnp.zeros((LANES - t, DSA_GW), F32)
    q = q_ref[0].astype(BF16)
    k = jnp.concatenate([kc_ref[0], kn_ref[0], pad], axis=0).astype(BF16)
    v = jnp.concatenate([vc_ref[0], vn_ref[0], pad], axis=0).astype(BF16)
    rows = cache_len + LANES
    qi = lax.broadcasted_iota(jnp.int32, (t, rows), 0)
    col = lax.broadcasted_iota(jnp.int32, (t, rows), 1)
    dist = cache_len + qi - col
    valid = (dist >= 0) & (dist <= window) & ((dist & (dil - 1)) == 0)
    distf = dist.astype(F32)
    for h in range(DSA_HPG):
        sl = slice(h * DSA_HEAD_DIM, (h + 1) * DSA_HEAD_DIM)
        s = lax.dot_general(q[:, sl], k[:, sl], (((1,), (1,)), ((), ())), preferred_element_type=F32)
        s = s * (DSA_HEAD_DIM ** -0.5) - _alibi_slope(group * DSA_HPG + h) * distf
        s = jnp.where(valid, s, NEG)
        m = jnp.max(s, axis=-1, keepdims=True)
        p = jnp.exp(s - m)
        l = jnp.sum(p, axis=-1, keepdims=True)
        o = jnp.dot(p.astype(BF16), v[:, sl], preferred_element_type=F32) / l
        o_ref[0, :, sl] = o
        lse_ref[0, :, sl] = jnp.broadcast_to(m + jnp.log(l), (t, DSA_HEAD_DIM))


def dsa_sample_group(qkv, cache, group):
    bsz, t, _ = qkv.shape
    cache_len = cache.shape[1]
    cview = cache.reshape(bsz, cache_len, 2 * DSA_GW)
    new_spec = lambda part: pl.BlockSpec((1, t, DSA_GW), lambda b: (b, 0, part * DSA_GROUPS + group))
    cache_spec = lambda part: pl.BlockSpec((1, cache_len, DSA_GW), lambda b: (b, 0, part))
    out_spec = pl.BlockSpec((1, t, DSA_GW), lambda b: (b, 0, 0))
    return pl.pallas_call(
        functools.partial(_dsa_sample_kernel, cache_len=cache_len, dil=DSA_DILATIONS[group],
                          window=DSA_WINDOWS[group], group=group),
        out_shape=(_sds((bsz, t, DSA_GW), F32), _sds((bsz, t, DSA_GW), F32)),
        grid=(bsz,),
        in_specs=[new_spec(0), cache_spec(0), cache_spec(1), new_spec(1), new_spec(2)],
        out_specs=(out_spec, out_spec),
        compiler_params=_cp("parallel"),
        name=f"dsa_sample_g{group}",
    )(qkv, cview, cview, qkv, qkv)


def _dsa_combine_kernel(o0, o1, o2, l0, l1, l2, out_ref):
    ls = (l0[...], l1[...], l2[...])
    m = jnp.maximum(jnp.maximum(ls[0], ls[1]), ls[2])
    es = [jnp.exp(x - m) for x in ls]
    den = es[0] + es[1] + es[2]
    num = es[0] * o0[...] + es[1] * o1[...] + es[2] * o2[...]
    out_ref[...] = (num / den).astype(out_ref.dtype)


def dsa_combine(outs, lses):
    m, n = outs[0].shape
    tm = min(m, 1024)
    spec = pl.BlockSpec((tm, n), lambda i: (i, 0))
    return pl.pallas_call(
        _dsa_combine_kernel, out_shape=_sds((m, n), BF16), grid=(m // tm,),
        in_specs=[spec] * 6, out_specs=spec, compiler_params=_cp("parallel"), name="dsa_combine",
    )(*outs, *lses)


def _split_bf16(x):
    hi = x.astype(BF16)
    lo = (x - hi.astype(F32)).astype(BF16)
    return hi, lo


def _tri_sum(x, tri):
    hi, lo = _split_bf16(x)
    return (jnp.dot(hi, tri, preferred_element_type=F32) + jnp.dot(lo, tri, preferred_element_type=F32))


def _cumsum_kernel(x_ref, o_ref, *, tn):
    j = pl.program_id(0)
    t = x_ref.shape[1]
    src = lax.broadcasted_iota(jnp.int32, (t, tn), 0)
    dst = lax.broadcasted_iota(jnp.int32, (t, tn), 1) + j * tn
    o_ref[...] = _tri_sum(x_ref[...], jnp.where(src <= dst, 1.0, 0.0).astype(BF16))


def cumsum_lanes(x):
    r, t = x.shape
    tn = min(t, 512)
    return pl.pallas_call(
        functools.partial(_cumsum_kernel, tn=tn), out_shape=_sds((r, t), F32), grid=(t // tn,),
        in_specs=[pl.BlockSpec((r, t), lambda j: (0, 0))],
        out_specs=pl.BlockSpec((r, tn), lambda j: (0, j)),
        compiler_params=_cp("arbitrary"), name="fox_cumsum",
    )(x)


def _fox_prompt_kernel(q_ref, k_ref, v_ref, fk_ref, o_ref, *, scale, n_heads):
    i = pl.program_id(2)
    tq = q_ref.shape[1]
    hd = q_ref.shape[2] // n_heads
    nt = (((1,), (1,)), ((), ()))
    for ib in range(k_ref.shape[1] // tq):
        @pl.when(i == ib)
        def _(ib=ib):
            n_off = ib * tq
            diag = slice(n_off, n_off + tq)
            causal = (lax.broadcasted_iota(jnp.int32, (tq, tq), 1)
                      <= lax.broadcasted_iota(jnp.int32, (tq, tq), 0))
            for hh in range(n_heads):
                ls = slice(hh * hd, (hh + 1) * hd)
                q = q_ref[0, :, ls]
                fk = fk_ref[hh]
                s_d = lax.dot_general(q, k_ref[0, diag, ls], nt, preferred_element_type=F32) * scale - fk[:, diag]
                s_d = jnp.where(causal, s_d, NEG)
                m = jnp.max(s_d, axis=-1, keepdims=True)
                if ib:
                    s_o = (lax.dot_general(q, k_ref[0, :n_off, ls], nt, preferred_element_type=F32) * scale
                           - fk[:, :n_off])
                    m = jnp.maximum(m, jnp.max(s_o, axis=-1, keepdims=True))
                p_d = jnp.exp(s_d - m)
                l = jnp.sum(p_d, axis=-1, keepdims=True)
                o = jnp.dot(p_d.astype(BF16), v_ref[0, diag, ls], preferred_element_type=F32)
                if ib:
                    p_o = jnp.exp(s_o - m)
                    l = l + jnp.sum(p_o, axis=-1, keepdims=True)
                    o = o + jnp.dot(p_o.astype(BF16), v_ref[0, :n_off, ls], preferred_element_type=F32)
                o_ref[0, :, ls] = (o / l).astype(o_ref.dtype)


def fox_prompt_attention(qkv, f_cum, *, tq=256, n_heads=4):
    bsz, t, n = qkv.shape
    d = n // 3
    hw = n_heads * (d // FOX_HEADS)
    nhb = FOX_HEADS // n_heads
    return pl.pallas_call(
        functools.partial(_fox_prompt_kernel, scale=(d // FOX_HEADS) ** -0.5, n_heads=n_heads),
        out_shape=_sds((bsz, t, d), BF16),
        grid=(bsz, nhb, t // tq),
        in_specs=[pl.BlockSpec((1, tq, hw), lambda b, h, i: (b, i, h)),
                  pl.BlockSpec((1, t, hw), lambda b, h, i: (b, 0, nhb + h)),
                  pl.BlockSpec((1, t, hw), lambda b, h, i: (b, 0, 2 * nhb + h)),
                  pl.BlockSpec((n_heads, 1, t), lambda b, h, i: (b * nhb + h, 0, 0))],
        out_specs=pl.BlockSpec((1, tq, hw), lambda b, h, i: (b, i, h)),
        compiler_params=_cp("parallel", "parallel", "arbitrary"),
        name="fox_prompt_attn",
    )(qkv, qkv, qkv, f_cum)


def _fox_sample_kernel(pt_ref, q_ref, nkv_ref, nlf_ref, *rest, scale, t_new, heads, pps):
    kv_refs, lf_refs = rest[:pps], rest[pps:2 * pps]
    o_ref, pad_sc, m_sc, l_sc, acc_sc, suf_sc = rest[2 * pps:]
    j = pl.program_id(1)
    page = nlf_ref.shape[2]
    hd = q_ref.shape[2] // heads
    r = heads * t_new
    key_rows = 2 * heads
    pitch = pad_sc.shape[1] // page
    src = lax.broadcasted_iota(jnp.int32, (page, page), 0)
    dst = lax.broadcasted_iota(jnp.int32, (page, page), 1)

    def attend(kv_ref, slot, bias, valid, first):
        for key in range(page):
            pad_sc[slot, key * pitch:key * pitch + key_rows, :] = kv_ref[0, key * key_rows:(key + 1) * key_rows, :]
        rows_ref, row_pitch = pad_sc.at[slot], pitch
        s_rows = []
        for h in range(heads):
            k_h = rows_ref[pl.ds(h, page, stride=row_pitch), :].astype(BF16)
            s_h = lax.dot_general(q_ref[0, :, h * hd:(h + 1) * hd], k_h, (((1,), (1,)), ((), ())),
                                  preferred_element_type=F32)
            s_rows.append(s_h * scale + bias[h:h + 1, :])
        s = jnp.concatenate(s_rows, axis=0)
        if valid is not None:
            s = jnp.where(valid, s, NEG)
        m_cur = jnp.max(s, axis=-1, keepdims=True)
        m_new = m_cur if first else jnp.maximum(m_sc[...], m_cur)
        p = jnp.exp(s - m_new)
        pv = jnp.concatenate(
            [jnp.dot(p[h * t_new:(h + 1) * t_new].astype(BF16),
                     rows_ref[pl.ds(heads + h, page, stride=row_pitch), :].astype(BF16),
                     preferred_element_type=F32) for h in range(heads)], axis=0)
        l_cur = jnp.sum(p, axis=-1, keepdims=True)
        if first:
            l_sc[...] = l_cur
            acc_sc[...] = pv
        else:
            c = jnp.exp(m_sc[...] - m_new)
            l_sc[...] = l_sc[...] * c + l_cur
            acc_sc[...] = acc_sc[...] * c + pv
        m_sc[...] = m_new

    @pl.when(j == 0)
    def _():
        fq = _tri_sum(nlf_ref[0], jnp.where(src <= dst, 1.0, 0.0).astype(BF16))
        tq = lax.broadcasted_iota(jnp.int32, (r, page), 0) % t_new
        tk = lax.broadcasted_iota(jnp.int32, (r, page), 1)
        attend(nkv_ref, 0, -fq, tk <= tq, True)
        suf_sc[...] = jnp.zeros_like(suf_sc)

    @pl.when(j > 0)
    def _():
        later_mask = jnp.where(src > dst, 1.0, 0.0).astype(BF16)
        for u in range(pps):
            lf = lf_refs[u][0]
            suf = suf_sc[...]
            attend(kv_refs[u], u, _tri_sum(lf, later_mask) + suf, None, False)
            suf_sc[...] = suf + jnp.sum(lf, axis=-1, keepdims=True)

    @pl.when(j == pl.num_programs(1) - 1)
    def _():
        inv = 1.0 / l_sc[...]
        for h in range(heads):
            rs = slice(h * t_new, (h + 1) * t_new)
            o_ref[0, :, h * hd:(h + 1) * hd] = (acc_sc[rs, :] * inv[rs]).astype(o_ref.dtype)


def fox_sample_attention(qkv, kv_new, logf, kv_pool, logf_pool, page_table):
    bsz, t, n = qkv.shape
    d = n // 3
    heads = FOX_HEADS
    hd = d // heads
    n_pool, page = kv_pool.shape[0], kv_pool.shape[1]
    n_pages = page_table.shape[1]
    pps = FOX_PAGES_PER_STEP
    n_steps = n_pages // pps
    key_rows = 2 * heads
    rows = page * key_rows
    pool = kv_pool.reshape(n_pool, rows, hd)
    pool_lf = logf_pool.transpose(0, 2, 1)
    new_kv = jnp.pad(kv_new.reshape(bsz, t, key_rows, hd), ((0, 0), (0, page - t), (0, 0), (0, 0)))
    new_kv = new_kv.reshape(bsz, rows, hd)
    new_lf = jnp.pad(logf.transpose(0, 2, 1), ((0, 0), (0, 0), (0, page - t)))

    def page_idx(u):
        def index_map(b, j, pt_ref):
            return (pt_ref[b * n_pages + n_pages - 1 - (jnp.maximum(j, 1) - 1) * pps - u], 0, 0)
        return index_map

    own = lambda b, j, pt_ref: (b, 0, 0)
    grid_spec = pltpu.PrefetchScalarGridSpec(
        num_scalar_prefetch=1,
        grid=(bsz, n_steps + 1),
        in_specs=[pl.BlockSpec((1, t, d), own),
                  pl.BlockSpec((1, rows, hd), own),
                  pl.BlockSpec((1, heads, page), own)]
                 + [pl.BlockSpec((1, rows, hd), page_idx(u)) for u in range(pps)]
                 + [pl.BlockSpec((1, heads, page), page_idx(u)) for u in range(pps)],
        out_specs=pl.BlockSpec((1, t, d), own),
        scratch_shapes=[pltpu.VMEM((pps, page * (key_rows + FOX_PITCH_PAD), hd), F32),
                        pltpu.VMEM((heads * t, 1), F32), pltpu.VMEM((heads * t, 1), F32),
                        pltpu.VMEM((heads * t, hd), F32), pltpu.VMEM((heads, 1), F32)],
    )
    return pl.pallas_call(
        functools.partial(_fox_sample_kernel, scale=hd ** -0.5, t_new=t, heads=heads, pps=pps),
        out_shape=_sds((bsz, t, d), BF16),
        grid_spec=grid_spec,
        compiler_params=_cp("parallel", "arbitrary"),
        name="fox_sample_attn",
    )(page_table.reshape(-1), qkv, new_kv, new_lf, *([pool] * pps), *([pool_lf] * pps))


def _flat(x):
    return x.reshape(-1, x.shape[-1])


def _last_rows(hist, new, n):
    t = new.shape[1]
    if t >= n:
        return new[:, t - n:]
    return jnp.concatenate([hist.astype(new.dtype)[:, t:], new], axis=1)


def _conv_tiles(bsz, t):
    return dict(tm=256, nb=1) if t >= 256 else dict(tm=t, nb=bsz)


def _mixer_a(x, gain, hist, w_in, conv_w, w_out, shape):
    bsz, t, d = shape
    m = bsz * t
    tm = min(1024, m)
    gb, u = matmul(x, w_in, tm=tm, tn=512, parts=3, epilogue=_ep_gate_split, norm_gain=gain,
                   out_shapes=(_sds((m, d), F32), _sds((m, d), F32)),
                   out_specs=(_tile_spec(tm, 512), _tile_spec(tm, 512)), name="a_in")
    u3 = u.reshape(bsz, t, d)
    a = causal_conv_fused(u3, hist, conv_w, taps=A_CONV, gate=gb.reshape(bsz, t, d), tc=512,
                          **_conv_tiles(bsz, t))
    x = matmul_residual(_flat(a), w_out, x, name="a_out")
    return x, _last_rows(hist, u3, A_CONV - 1)


def _dsa_kv_rows(qkv3, group, n_last):
    qkv3 = qkv3[:, qkv3.shape[1] - n_last:]
    bsz, t, _ = qkv3.shape
    k0 = DSA_GROUPS * DSA_GW + group * DSA_GW
    v0 = 2 * DSA_GROUPS * DSA_GW + group * DSA_GW
    k = qkv3[:, :, k0:k0 + DSA_GW].reshape(bsz, t, DSA_HPG, DSA_HEAD_DIM)
    v = qkv3[:, :, v0:v0 + DSA_GW].reshape(bsz, t, DSA_HPG, DSA_HEAD_DIM)
    return jnp.stack([k, v], axis=2)


def _mixer_b(x, gain, caches, w_qkv, qn, kn, w_out, shape):
    bsz, t, d = shape
    (qkv,) = matmul_qk_norm(x, gain, w_qkv, qn, kn, head_dim=DSA_HEAD_DIM, name="b_qkv")
    qkv3 = qkv.reshape(bsz, t, -1)
    if caches is None:
        outs, lses = zip(*[dsa_prompt_group(qkv3, g) for g in range(DSA_GROUPS)])
        bufs = [_dsa_kv_rows(qkv3, g, min(DSA_WINDOWS[g], t)) for g in range(DSA_GROUPS)]
    else:
        outs, lses = zip(*[dsa_sample_group(qkv3, caches[g], g) for g in range(DSA_GROUPS)])
        bufs = [jnp.concatenate([caches[g][:, t:], _dsa_kv_rows(qkv3, g, t)], axis=1) for g in range(DSA_GROUPS)]
    o = dsa_combine([_flat(a) for a in outs], [_flat(a) for a in lses])
    x = matmul_residual(o, w_out, x, name="b_out")
    return x, bufs


def _mixer_c(x, gain, hist, w_in, conv_w, ln_g, ln_b, w_out, shape):
    bsz, t, d = shape
    m = bsz * t
    tm = min(1024, m)
    u = matmul(x, w_in, tm=tm, tn=512, parts=2, epilogue=_ep_glu, norm_gain=gain,
               out_shapes=_sds((m, d), F32), out_specs=_tile_spec(tm, 512), name="c_in")
    u3 = u.reshape(bsz, t, d)
    tiles = _conv_tiles(bsz, t)
    a = causal_conv_fused(u3, hist, conv_w, taps=C_CONV, ln=(ln_g, ln_b), tm=min(tiles["tm"], 128), tc=d,
                          nb=tiles["nb"])
    x = matmul_residual(_flat(a), w_out, x, name="c_out")
    return x, _last_rows(hist, u3, C_CONV - 1)


def _mixer_d(x, gain, paged, w_qkv, w_f, b_f, qn, kn, w_out, shape):
    bsz, t, d = shape
    m = bsz * t
    tm = min(1024, m)
    qkv, kv = matmul_qk_norm(x, gain, w_qkv, qn, kn, head_dim=d // FOX_HEADS, split_kv=True, name="d_qkv")
    logf = matmul(x, w_f, tm=tm, tn=LANES, epilogue=_ep_logf, norm_gain=gain, out_shapes=_sds((m, LANES), F32),
                  out_specs=_tile_spec(tm, LANES), extras=(b_f,),
                  extra_specs=(pl.BlockSpec((1, LANES), lambda i, j: (0, 0)),), name="d_logf")
    logf = logf[:, :FOX_HEADS].reshape(bsz, t, FOX_HEADS)
    qkv3 = qkv.reshape(bsz, t, 3 * d)
    kv3 = kv.reshape(bsz, t, 2 * d)
    if paged is None:
        f_cum = cumsum_lanes(logf.transpose(0, 2, 1).reshape(bsz * FOX_HEADS, t))
        o = fox_prompt_attention(qkv3, f_cum.reshape(bsz * FOX_HEADS, 1, t))
    else:
        o = fox_sample_attention(qkv3, kv3, logf, *paged)
    x = matmul_residual(_flat(o), w_out, x, name="d_out")
    return x, kv3.reshape(bsz, t, 2, FOX_HEADS, d // FOX_HEADS), logf


def kernel(x_prompt, x_sample, state_conv_a, cache_dsa_g0, cache_dsa_g1, cache_dsa_g2, state_conv_c,
           cache_fox_kv, cache_fox_logf, state_ffn_conv, page_table, p_prompt, p_sample,
           norm_mix, norm_ffn, w_a_in, conv_a_w, w_a_out, w_b_qkv, qn_b, kn_b, w_b_out,
           w_c_in, conv_c_w, ln_c_g, ln_c_b, w_c_out, w_d_qkvf, b_f, qn_d, kn_d, w_d_out,
           w_ffn_up, conv_ffn_w, w_ffn_down, norm_ple, w_ple_gate, w_ple_proj):
    depth = norm_mix.shape[0]
    d = x_prompt.shape[-1]
    d_ff = w_ffn_down.shape[1]
    shapes = (x_prompt.shape, x_sample.shape)
    xs = [_flat(x_prompt), _flat(x_sample)]
    ps = tuple(p.reshape(depth, -1, p.shape[-1]) for p in (p_prompt, p_sample))
    bf = lambda w: w.astype(BF16)
    w_up_bf, w_down_bf, w_gate_bf, w_proj_bf = bf(w_ffn_up), bf(w_ffn_down), bf(w_ple_gate), bf(w_ple_proj)

    w_d_qkv = bf(w_d_qkvf[:, :3 * d])
    n_f = w_d_qkvf.shape[1] - 3 * d
    w_d_f = bf(jnp.pad(w_d_qkvf[:, 3 * d:], ((0, 0), (0, LANES - n_f))))
    b_f_pad = jnp.pad(b_f.astype(F32), (0, LANES - n_f)).reshape(1, LANES)
    caches = (cache_dsa_g0, cache_dsa_g1, cache_dsa_g2)
    paged = (cache_fox_kv, cache_fox_logf, page_table)

    mixer_out = [dict(), dict()]
    ffn_hist = [[], []]
    for i in range(depth):
        kind = i % 4
        for grp in range(2):
            shape = shapes[grp]
            bsz, t, _ = shape
            x = xs[grp]
            out = mixer_out[grp]
            gain = norm_mix[i]
            if kind == 0:
                hist = jnp.zeros((bsz, A_CONV - 1, d), F32) if grp == 0 else state_conv_a
                x, out["conv_a"] = _mixer_a(x, gain, hist, bf(w_a_in), conv_a_w, bf(w_a_out), shape)
            elif kind == 1:
                x, out["dsa"] = _mixer_b(x, gain, None if grp == 0 else caches, bf(w_b_qkv), qn_b, kn_b,
                                         bf(w_b_out), shape)
            elif kind == 2:
                hist = jnp.zeros((bsz, C_CONV - 1, d), F32) if grp == 0 else state_conv_c
                x, out["conv_c"] = _mixer_c(x, gain, hist, bf(w_c_in), conv_c_w, ln_c_g, ln_c_b, bf(w_c_out), shape)
            else:
                x, out["fox_kv"], out["fox_logf"] = _mixer_d(
                    x, gain, None if grp == 0 else paged, w_d_qkv, w_d_f, b_f_pad, qn_d, kn_d, bf(w_d_out), shape)
            hist = jnp.zeros((bsz, FFN_CONV - 1, d_ff), F32) if grp == 0 else state_ffn_conv[i]
            x, g_last = ffn_fused(x, norm_ffn[i], hist, w_up_bf, conv_ffn_w, w_down_bf, i, shape)
            ffn_hist[grp].append(_last_rows(hist, g_last[:, SUBLANES - min(t, SUBLANES):], FFN_CONV - 1))
            xs[grp] = per_layer_embed(x, norm_ple[i], ps[grp], w_gate_bf, w_proj_bf, i)

    mp, ms = mixer_out
    return (xs[0].reshape(shapes[0]), xs[1].reshape(shapes[1]),
            mp["conv_a"], ms["conv_a"],
            mp["dsa"][0], ms["dsa"][0], mp["dsa"][1], ms["dsa"][1], mp["dsa"][2], ms["dsa"][2],
            mp["conv_c"], ms["conv_c"],
            mp["fox_kv"], ms["fox_kv"], mp["fox_logf"], ms["fox_logf"],
            jnp.stack(ffn_hist[0]), jnp.stack(ffn_hist[1]))
```

```python
import functools

import jax
import jax.numpy as jnp
from jax import lax
from jax.experimental import pallas as pl
from jax.experimental.pallas import tpu as pltpu

F32 = jnp.float32
BF16 = jnp.bfloat16
EPS = 1e-6
NEG = -1e30

VMEM_LIMIT_BYTES = 56 * 1024 * 1024
FFN_VMEM_LIMIT_BYTES = 58 * 1024 * 1024
LANES = 128
SUBLANES = 8
BF16_SUBLANES = 16
MXU_COLS = 256

A_CONV = 3
C_CONV = 31
FFN_CONV = 3
DSA_WINDOWS = (128, 512, 2048)
DSA_DILATIONS = (1, 4, 16)
DSA_GROUPS = 3
DSA_HPG = 8
DSA_HEAD_DIM = 64
DSA_HEADS = DSA_GROUPS * DSA_HPG
DSA_GW = DSA_HPG * DSA_HEAD_DIM
DSA_KEYS = 128
FOX_HEADS = 16
Q_BLOCK = 128
FOX_PAGES_PER_STEP = 4
FOX_PITCH_PAD = 8


def _cp(*sem, vmem_limit_bytes=VMEM_LIMIT_BYTES):
    return pltpu.CompilerParams(dimension_semantics=sem, vmem_limit_bytes=vmem_limit_bytes)


def _sds(shape, dtype):
    return jax.ShapeDtypeStruct(shape, dtype)


def _rmsnorm(x, g):
    ms = jnp.mean(x * x, axis=-1, keepdims=True)
    return x * lax.rsqrt(ms + EPS) * g


def _mm_body(*refs, parts, n_extra, epilogue, norm):
    if norm:
        x_ref, g_ref, a_sc = refs[0], refs[1], refs[-1]
        refs = refs[2:-1]

        @pl.when(pl.program_id(1) == 0)
        def _():
            a_sc[...] = _rmsnorm(x_ref[...], g_ref[...]).astype(BF16)

        a = a_sc[...]
    else:
        a = refs[0][...]
        refs = refs[1:]
    extras = refs[parts:parts + n_extra]
    outs = refs[parts + n_extra:]

    def dot(cols):
        return [jnp.dot(a, refs[p][:, cols], preferred_element_type=F32) for p in range(parts)]

    epilogue(dot, extras, outs)


def _col_chunks(tn):
    step = min(tn, MXU_COLS)
    return [slice(c0, c0 + step) for c0 in range(0, tn, step)]


def matmul(a, w, *, tm, tn, epilogue, out_shapes, out_specs, parts=1, extras=(), extra_specs=(),
           norm_gain=None, name):
    m, k = a.shape
    np_ = w.shape[1] // parts
    nj = np_ // tn
    tm = min(tm, m)
    norm = norm_gain is not None
    in_specs = [pl.BlockSpec((tm, k), lambda i, j: (i, 0))]
    args = [a]
    if norm:
        in_specs.append(pl.BlockSpec((1, k), lambda i, j: (0, 0)))
        args.append(norm_gain.reshape(1, k))
    in_specs += [pl.BlockSpec((k, tn), functools.partial(lambda i, j, p: (0, p * nj + j), p=p))
                 for p in range(parts)]
    in_specs += list(extra_specs)
    return pl.pallas_call(
        functools.partial(_mm_body, parts=parts, n_extra=len(extras), epilogue=epilogue, norm=norm),
        out_shape=out_shapes,
        grid=(m // tm, nj),
        in_specs=in_specs,
        out_specs=out_specs,
        scratch_shapes=[pltpu.VMEM((tm, k), BF16)] if norm else [],
        compiler_params=_cp("parallel", "arbitrary"),
        name=name,
    )(*args, *([w] * parts), *extras)


def _tile_spec(tm, tn):
    return pl.BlockSpec((tm, tn), lambda i, j: (i, j))


def _ep_residual(dot, extras, outs):
    for cols in _col_chunks(outs[0].shape[1]):
        outs[0][:, cols] = extras[0][:, cols] + dot(cols)[0]


def matmul_residual(a, w, res, *, tm=512, name):
    m, n = a.shape[0], w.shape[1]
    tm, tn = min(tm, m), n
    return matmul(a, w, tm=tm, tn=tn, epilogue=_ep_residual, out_shapes=_sds((m, n), F32),
                  out_specs=_tile_spec(tm, tn), extras=(res,), extra_specs=(_tile_spec(tm, tn),), name=name)


def _ep_gate_split(dot, extras, outs):
    for cols in _col_chunks(outs[0].shape[1]):
        gb, gc, v = dot(cols)
        outs[0][:, cols] = gb
        outs[1][:, cols] = gc * v


def _ep_glu(dot, extras, outs):
    for cols in _col_chunks(outs[0].shape[1]):
        a, g = dot(cols)
        outs[0][:, cols] = a * jax.nn.sigmoid(g)


def _ep_qk_norm(dot, extras, outs, *, head_dim, blocks_per_part, split_kv):
    j = pl.program_id(1)
    gain = extras[0][0]
    tn = outs[0].shape[1]

    def store(cols, val, with_kv):
        outs[0][:, cols] = val.astype(outs[0].dtype)
        if split_kv and with_kv:
            outs[1][:, cols] = val

    def tile(norm, with_kv):
        for chunk in _col_chunks(tn):
            acc = dot(chunk)[0]
            if not norm:
                store(chunk, acc, with_kv)
                continue
            for c0 in range(0, acc.shape[1], LANES):
                cols = slice(chunk.start + c0, chunk.start + c0 + LANES)
                x = acc[:, c0:c0 + LANES]
                sq = x * x
                if head_dim == LANES:
                    ms = jnp.sum(sq, axis=-1, keepdims=True) * (1.0 / head_dim)
                else:
                    lane = lax.broadcasted_iota(jnp.int32, sq.shape, 1)
                    first = lane < head_dim
                    s_lo = jnp.sum(jnp.where(first, sq, 0.0), axis=-1, keepdims=True)
                    s_hi = jnp.sum(jnp.where(first, 0.0, sq), axis=-1, keepdims=True)
                    ms = jnp.where(first, s_lo, s_hi) * (1.0 / head_dim)
                store(cols, x * lax.rsqrt(ms + EPS) * gain[:, cols], with_kv)

    @pl.when(j < blocks_per_part)
    def _():
        tile(True, False)

    @pl.when((j >= blocks_per_part) & (j < 2 * blocks_per_part))
    def _():
        tile(True, True)

    @pl.when(j >= 2 * blocks_per_part)
    def _():
        tile(False, True)


def matmul_qk_norm(x, norm_gain, w, qn, kn, *, head_dim, n_cols=None, split_kv=False, tm=1024, tn=512, name):
    m, n = x.shape[0], n_cols or w.shape[1]
    tm = min(tm, m)
    bpp = n // 3 // tn
    reps = tn // head_dim
    gains = jnp.stack([jnp.tile(qn, reps), jnp.tile(kn, reps), jnp.ones((tn,), F32)]).reshape(3, 1, tn)
    gain_spec = pl.BlockSpec((1, 1, tn), lambda i, j: (j // bpp, 0, 0))
    if split_kv:
        shapes = (_sds((m, n), BF16), _sds((m, n - n // 3), F32))
        specs = (_tile_spec(tm, tn), pl.BlockSpec((tm, tn), lambda i, j: (i, jnp.maximum(j - bpp, 0))))
    else:
        shapes, specs = (_sds((m, n), F32),), (_tile_spec(tm, tn),)
    return matmul(x, w, tm=tm, tn=tn, norm_gain=norm_gain,
                  epilogue=functools.partial(_ep_qk_norm, head_dim=head_dim, blocks_per_part=bpp, split_kv=split_kv),
                  out_shapes=shapes, out_specs=specs, extras=(gains,), extra_specs=(gain_spec,), name=name)


def _log_sigmoid(x):
    return jnp.minimum(x, 0.0) - jnp.log1p(jnp.exp(-jnp.abs(x)))


def _ep_logf(dot, extras, outs):
    outs[0][...] = _log_sigmoid(dot(slice(None))[0] + extras[0][...])


def _ple_body(x_ref, g_ref, wg_ref, p_ref, wp_ref, o_ref, h_sc, *, tn):
    j = pl.program_id(1)

    @pl.when(j == 0)
    def _():
        h_sc[...] = _rmsnorm(x_ref[...], g_ref[...]).astype(BF16)

    h = h_sc[...]
    p = p_ref[...].astype(BF16)
    for cols in _col_chunks(tn):
        gate = jax.nn.sigmoid(jnp.dot(h, wg_ref[:, cols], preferred_element_type=F32))
        proj = jnp.dot(p, wp_ref[:, cols], preferred_element_type=F32)
        x = x_ref[:, pl.ds(pl.multiple_of(j * tn + cols.start, cols.stop - cols.start), cols.stop - cols.start)]
        o_ref[:, cols] = x + gate * proj


def per_layer_embed(x, norm_gain, p, w_gate, w_proj, layer, *, tm=1024, tn=1024):
    m, d = x.shape
    tm = min(tm, m)
    pd = p.shape[2]
    return pl.pallas_call(
        functools.partial(_ple_body, tn=tn),
        out_shape=_sds((m, d), F32),
        grid=(m // tm, d // tn),
        in_specs=[pl.BlockSpec((tm, d), lambda i, j: (i, 0)),
                  pl.BlockSpec((1, d), lambda i, j: (0, 0)),
                  pl.BlockSpec((None, d, tn), lambda i, j: (layer, 0, j)),
                  pl.BlockSpec((None, tm, pd), lambda i, j: (layer, i, 0)),
                  pl.BlockSpec((None, pd, tn), lambda i, j: (layer, 0, j))],
        out_specs=_tile_spec(tm, tn),
        scratch_shapes=[pltpu.VMEM((tm, d), BF16)],
        compiler_params=_cp("parallel", "arbitrary"),
        name="per_layer_embed",
    )(x, norm_gain.reshape(1, d), w_gate, p, w_proj)


def _causal_taps(prev, cur, w, taps):
    p_rows, tm = prev.shape[0], cur.shape[0]
    ext = jnp.concatenate([prev, cur], axis=0)
    n = p_rows + tm
    acc = None
    for rot in range(SUBLANES):
        ks = [k for k in range(taps) if (p_rows - (taps - 1) + k) % SUBLANES == rot]
        if not ks:
            continue
        shifted = ext if rot == 0 else pltpu.roll(ext, n - rot, axis=0)
        for k in ks:
            base = p_rows - (taps - 1) + k - rot
            term = w[k:k + 1, :] * shifted[base:base + tm, :]
            acc = term if acc is None else acc + term
    return acc


def _conv_gate_kernel(cur_ref, prev_ref, hist_ref, w_ref, gate_ref, o_ref, *, taps, chunk):
    at_start = pl.program_id(1) == 0
    nb, tm, tc = cur_ref.shape
    for b in range(nb):
        for c0 in range(0, tc, chunk):
            cs = slice(c0, c0 + chunk)
            prev = jnp.where(at_start, hist_ref[b, :, cs], prev_ref[b, :, cs])
            y = _causal_taps(prev, cur_ref[b, :, cs], w_ref[:, cs], taps)
            o_ref[b, :, cs] = (gate_ref[b, :, cs] * y).astype(o_ref.dtype)


def _conv_ln_kernel(cur_ref, prev_ref, hist_ref, w_ref, lng_ref, lnb_ref, o_ref, c_sc, *, taps, chunk):
    at_start = pl.program_id(1) == 0
    nb, tm, tc = cur_ref.shape
    for b in range(nb):
        for c0 in range(0, tc, chunk):
            cs = slice(c0, c0 + chunk)
            prev = jnp.where(at_start, hist_ref[b, :, cs], prev_ref[b, :, cs])
            c_sc[:, cs] = _causal_taps(prev, cur_ref[b, :, cs], w_ref[:, cs], taps)
        c = c_sc[...]
        mu = jnp.mean(c, axis=-1, keepdims=True)
        xc = c - mu
        var = jnp.mean(xc * xc, axis=-1, keepdims=True)
        y = xc * lax.rsqrt(var + EPS) * lng_ref[...] + lnb_ref[...]
        o_ref[b] = (y * jax.nn.sigmoid(y)).astype(o_ref.dtype)


def _pad_hist(hist, p_rows):
    b, k1, c = hist.shape
    return jnp.concatenate([jnp.zeros((b, p_rows - k1, c), F32), hist.astype(F32)], axis=1)


def causal_conv_fused(src, hist, w, *, taps, gate=None, ln=None, tm, tc, nb):
    bsz, t, c = src.shape
    p_rows = SUBLANES if taps - 1 <= SUBLANES else 32
    hist_p = _pad_hist(hist, p_rows)
    tm = min(tm, t)
    n_t = t // tm
    chunk = min(tc, 256)
    if t >= p_rows:
        prev_src = src
        per = tm // p_rows
        prev_spec = pl.BlockSpec((nb, p_rows, tc), lambda b, i, j: (b, jnp.maximum(i * per - 1, 0), j))
    else:
        prev_src = hist_p
        prev_spec = pl.BlockSpec((nb, p_rows, tc), lambda b, i, j: (b, 0, j))
    cur_spec = pl.BlockSpec((nb, tm, tc), lambda b, i, j: (b, i, j))
    hist_spec = pl.BlockSpec((nb, p_rows, tc), lambda b, i, j: (b, 0, j))
    w_spec = pl.BlockSpec((taps, tc), lambda b, i, j: (0, j))
    grid = (bsz // nb, n_t, c // tc)
    if ln is not None:
        lng, lnb = ln
        vec_spec = pl.BlockSpec((1, tc), lambda b, i, j: (0, j))
        return pl.pallas_call(
            functools.partial(_conv_ln_kernel, taps=taps, chunk=chunk),
            out_shape=_sds((bsz, t, c), BF16), grid=grid,
            in_specs=[cur_spec, prev_spec, hist_spec, w_spec, vec_spec, vec_spec],
            out_specs=cur_spec,
            scratch_shapes=[pltpu.VMEM((tm, tc), F32)],
            compiler_params=_cp("parallel", "arbitrary", "arbitrary"),
            name="conv_ln",
        )(src, prev_src, hist_p, w, lng.reshape(1, c), lnb.reshape(1, c))
    return pl.pallas_call(
        functools.partial(_conv_gate_kernel, taps=taps, chunk=chunk),
        out_shape=_sds((bsz, t, c), BF16), grid=grid,
        in_specs=[cur_spec, prev_spec, hist_spec, w_spec, cur_spec],
        out_specs=cur_spec,
        compiler_params=_cp("parallel", "arbitrary", "arbitrary"),
        name="conv_gate",
    )(src, prev_src, hist_p, w, gate)


def _ffn_kernel(x_ref, halo_ref, hist_ref, gain_ref, wg_ref, wu_ref, cw_ref, wd_ref, o_ref, gh_ref, h_sc,
                *, n_seq, tiles_per_seq, sub):
    i, c = pl.program_id(0), pl.program_id(1)
    tm = x_ref.shape[0]
    halo = h_sc.shape[0] - tm

    @pl.when(c == 0)
    def _():
        x = x_ref[...]
        o_ref[...] = x
        h_sc[halo:, :] = _rmsnorm(x, gain_ref[...]).astype(BF16)
        if halo:
            h_sc[:halo, :] = _rmsnorm(halo_ref[...], gain_ref[...]).astype(BF16)

    d = None
    for c0 in range(0, wg_ref.shape[1], sub):
        cs = slice(c0, c0 + sub)
        g_ext = jnp.dot(h_sc[...], wg_ref[:, cs], preferred_element_type=F32)
        u = jnp.dot(h_sc[halo:, :], wu_ref[:, cs], preferred_element_type=F32)
        cw = cw_ref[:, cs]
        if halo:
            at_start = (i % tiles_per_seq) == 0
            prev = jnp.where(at_start, hist_ref[0, :, cs], g_ext[halo - SUBLANES:halo])
            g = g_ext[halo:]
            y = _causal_taps(prev, g, cw, FFN_CONV)
            gh_ref[0, :, cs] = g[tm - SUBLANES:]
        else:
            rows = tm // n_seq
            ys = []
            for b in range(n_seq):
                g = g_ext[b * rows:(b + 1) * rows]
                ys.append(_causal_taps(hist_ref[b, :, cs], g, cw, FFN_CONV))
                gh_ref[b, :, cs] = g[rows - SUBLANES:]
            y = jnp.concatenate(ys, axis=0)
        a = (y * jax.nn.sigmoid(y) * u).astype(BF16)
        part = jnp.dot(a, wd_ref[cs, :], preferred_element_type=F32)
        d = part if d is None else d + part

    o_ref[...] += d


def ffn_fused(x, norm_gain, hist, w_up, conv_w, w_down, layer, shape, *, tm=1024, tf=512):
    bsz, t, d = shape
    m = bsz * t
    d_ff = w_down.shape[1]
    nc = d_ff // tf
    hist_p = _pad_hist(hist, SUBLANES)
    if t >= tm:
        n_seq, tiles_per_seq, halo = 1, t // tm, BF16_SUBLANES
        seq_blk = lambda i: i // tiles_per_seq
    else:
        tm, n_seq, tiles_per_seq, halo = m, bsz, 1, 0
        seq_blk = lambda i: 0
    per = tm // BF16_SUBLANES
    hist_spec = pl.BlockSpec((n_seq, SUBLANES, tf), lambda i, c: (seq_blk(i), 0, c))
    x_new, g_tail = pl.pallas_call(
        functools.partial(_ffn_kernel, n_seq=n_seq, tiles_per_seq=tiles_per_seq, sub=tf),
        out_shape=(_sds((m, d), F32), _sds((bsz * tiles_per_seq, SUBLANES, d_ff), F32)),
        grid=(m // tm, nc),
        in_specs=[pl.BlockSpec((tm, d), lambda i, c: (i, 0)),
                  pl.BlockSpec((BF16_SUBLANES, d), lambda i, c: (jnp.maximum(i * per - 1, 0), 0)),
                  hist_spec,
                  pl.BlockSpec((1, d), lambda i, c: (0, 0)),
                  pl.BlockSpec((None, d, tf), lambda i, c: (layer, 0, c)),
                  pl.BlockSpec((None, d, tf), lambda i, c: (layer, 0, nc + c)),
                  pl.BlockSpec((None, FFN_CONV, tf), lambda i, c: (layer, 0, c)),
                  pl.BlockSpec((None, tf, d), lambda i, c: (layer, c, 0))],
        out_specs=(pl.BlockSpec((tm, d), lambda i, c: (i, 0)),
                   pl.BlockSpec((n_seq, SUBLANES, tf), lambda i, c: (i, 0, c))),
        scratch_shapes=[pltpu.VMEM((halo + tm, d), BF16)],
        compiler_params=_cp("parallel", "arbitrary", vmem_limit_bytes=FFN_VMEM_LIMIT_BYTES),
        name="ffn",
    )(x, x, hist_p, norm_gain.reshape(1, d), w_up, w_up, conv_w, w_down)
    return x_new, g_tail.reshape(bsz, tiles_per_seq, SUBLANES, d_ff)[:, -1]


def _alibi_slope(head):
    return 2.0 ** (-8.0 * (head + 1) / DSA_HEADS)


def _dsa_prompt_kernel(q_ref, kp_ref, kc_ref, vp_ref, vc_ref, o_ref, lse_ref, *, dil, group):
    i = pl.program_id(2)
    tq = Q_BLOCK
    q = q_ref[0].astype(BF16)
    k = jnp.concatenate([kp_ref[0], kc_ref[0]], axis=0).astype(BF16)
    v = jnp.concatenate([vp_ref[0], vc_ref[0]], axis=0).astype(BF16)
    row = lax.broadcasted_iota(jnp.int32, (tq, 2 * tq), 0)
    col = lax.broadcasted_iota(jnp.int32, (tq, 2 * tq), 1)
    dist = row + tq - col
    valid = (dist >= 0) & (dist <= DSA_KEYS) & ((col >= tq) | (i > 0))
    distf = dist.astype(F32) * float(dil)
    for h in range(DSA_HPG):
        sl = slice(h * DSA_HEAD_DIM, (h + 1) * DSA_HEAD_DIM)
        s = lax.dot_general(q[:, sl], k[:, sl], (((1,), (1,)), ((), ())), preferred_element_type=F32)
        s = s * (DSA_HEAD_DIM ** -0.5) - _alibi_slope(group * DSA_HPG + h) * distf
        s = jnp.where(valid, s, NEG)
        m = jnp.max(s, axis=-1, keepdims=True)
        p = jnp.exp(s - m)
        l = jnp.sum(p, axis=-1, keepdims=True)
        o = jnp.dot(p.astype(BF16), v[:, sl], preferred_element_type=F32) / l
        o_ref[0, :, sl] = o
        lse_ref[0, :, sl] = jnp.broadcast_to(m + jnp.log(l), (tq, DSA_HEAD_DIM))


def _dsa_dilated_kernel(q_ref, k_ref, v_ref, o_ref, lse_ref, *, dil, group):
    c = pl.program_id(1)
    tq = Q_BLOCK
    s_len = q_ref.shape[1] // dil
    heads_per_block = LANES // DSA_HEAD_DIM

    def geometry(n_keys):
        row = lax.broadcasted_iota(jnp.int32, (tq, n_keys), 0)
        col = lax.broadcasted_iota(jnp.int32, (tq, n_keys), 1)
        dist = row + (n_keys - tq) - col
        return (dist >= 0) & (dist <= DSA_KEYS), dist.astype(F32) * float(dil)

    first_block, later_block = geometry(tq), geometry(2 * tq)
    slopes = []
    for hh in range(heads_per_block):
        head = (group * DSA_HPG + c * heads_per_block + hh + 1).astype(F32)
        slopes.append(jnp.exp2(jnp.full((1, 1), head, F32) * (-8.0 / DSA_HEADS)))
    for r in range(dil):
        rows = pl.ds(r, s_len, stride=dil)
        q = q_ref[0, rows, :].astype(BF16)
        k = k_ref[0, rows, :].astype(BF16)
        v = v_ref[0, rows, :].astype(BF16)
        o_blocks, lse_blocks = [], []
        for qb in range(s_len // tq):
            lo = max(qb - 1, 0) * tq
            valid, distf = later_block if qb else first_block
            o_heads, lse_heads = [], []
            for hh in range(heads_per_block):
                sl = slice(hh * DSA_HEAD_DIM, (hh + 1) * DSA_HEAD_DIM)
                s = lax.dot_general(q[qb * tq:(qb + 1) * tq, sl], k[lo:(qb + 1) * tq, sl],
                                    (((1,), (1,)), ((), ())), preferred_element_type=F32)
                s = s * (DSA_HEAD_DIM ** -0.5) - slopes[hh] * distf
                s = jnp.where(valid, s, NEG)
                m = jnp.max(s, axis=-1, keepdims=True)
                p = jnp.exp(s - m)
                l = jnp.sum(p, axis=-1, keepdims=True)
                o_heads.append(jnp.dot(p.astype(BF16), v[lo:(qb + 1) * tq, sl], preferred_element_type=F32) / l)
                lse_heads.append(jnp.broadcast_to(m + jnp.log(l), (tq, DSA_HEAD_DIM)))
            o_blocks.append(jnp.concatenate(o_heads, axis=1))
            lse_blocks.append(jnp.concatenate(lse_heads, axis=1))
        o_ref[0, rows, :] = jnp.concatenate(o_blocks, axis=0)
        lse_ref[0, rows, :] = jnp.concatenate(lse_blocks, axis=0)


def dsa_dilated_group(qkv, group):
    bsz, t, n = qkv.shape
    blocks_per_part = n // 3 // LANES
    blocks_per_group = DSA_GW // LANES

    def spec(part):
        return pl.BlockSpec((1, t, LANES), lambda b, c: (b, 0, part * blocks_per_part + group * blocks_per_group + c))

    out_spec = pl.BlockSpec((1, t, LANES), lambda b, c: (b, 0, c))
    return pl.pallas_call(
        functools.partial(_dsa_dilated_kernel, dil=DSA_DILATIONS[group], group=group),
        out_shape=(_sds((bsz, t, DSA_GW), F32), _sds((bsz, t, DSA_GW), F32)),
        grid=(bsz, blocks_per_group),
        in_specs=[spec(0), spec(1), spec(2)],
        out_specs=(out_spec, out_spec),
        compiler_params=_cp("parallel", "parallel"),
        name=f"dsa_prompt_g{group}",
    )(qkv, qkv, qkv)


def dsa_prompt_group(qkv, group):
    bsz, t, n = qkv.shape
    dil = DSA_DILATIONS[group]
    if dil > 1:
        return dsa_dilated_group(qkv, group)
    s_len = t // dil
    nblk = n // DSA_GW
    view = qkv.reshape(bsz, s_len, dil * n)
    tq = Q_BLOCK

    def spec(part, prev):
        if prev:
            return pl.BlockSpec((1, tq, DSA_GW),
                                lambda b, r, i: (b, jnp.maximum(i - 1, 0), r * nblk + part * DSA_GROUPS + group))
        return pl.BlockSpec((1, tq, DSA_GW), lambda b, r, i: (b, i, r * nblk + part * DSA_GROUPS + group))

    out_spec = pl.BlockSpec((1, tq, DSA_GW), lambda b, r, i: (b, i, r))
    o, lse = pl.pallas_call(
        functools.partial(_dsa_prompt_kernel, dil=dil, group=group),
        out_shape=(_sds((bsz, s_len, dil * DSA_GW), F32), _sds((bsz, s_len, dil * DSA_GW), F32)),
        grid=(bsz, dil, s_len // tq),
        in_specs=[spec(0, False), spec(1, True), spec(1, False), spec(2, True), spec(2, False)],
        out_specs=(out_spec, out_spec),
        compiler_params=_cp("parallel", "parallel", "arbitrary"),
        name=f"dsa_prompt_g{group}",
    )(view, view, view, view, view)
    return o.reshape(bsz, t, DSA_GW), lse.reshape(bsz, t, DSA_GW)


def _dsa_sample_kernel(q_ref, kc_ref, vc_ref, kn_ref, vn_ref, o_ref, lse_ref, *, cache_len, dil, window, group):
    t = q_ref.shape[1]
    pad = jnp.zeros((LANES - t, DSA_GW), F32)
    q = q_ref[0].astype(BF16)
    k = jnp.concatenate([kc_ref[0], kn_ref[0], pad], axis=0).astype(BF16)
    v = jnp.concatenate([vc_ref[0], vn_ref[0], pad], axis=0).astype(BF16)
    rows = cache_len + LANES
    qi = lax.broadcasted_iota(jnp.int32, (t, rows), 0)
    col = lax.broadcasted_iota(jnp.int32, (t, rows), 1)
    dist = cache_len + qi - col
    valid = (dist >= 0) & (dist <= window) & ((dist & (dil - 1)) == 0)
    distf = dist.astype(F32)
    for h in range(DSA_HPG):
        sl = slice(h * DSA_HEAD_DIM, (h + 1) * DSA_HEAD_DIM)
        s = lax.dot_general(q[:, sl], k[:, sl], (((1,), (1,)), ((), ())), preferred_element_type=F32)
        s = s * (DSA_HEAD_DIM ** -0.5) - _alibi_slope(group * DSA_HPG + h) * distf
        s = jnp.where(valid, s, NEG)
        m = jnp.max(s, axis=-1, keepdims=True)
        p = jnp.exp(s - m)
        l = jnp.sum(p, axis=-1, keepdims=True)
        o = jnp.dot(p.astype(BF16), v[:, sl], preferred_element_type=F32) / l
        o_ref[0, :, sl] = o
        lse_ref[0, :, sl] = jnp.broadcast_to(m + jnp.log(l), (t, DSA_HEAD_DIM))


def dsa_sample_group(qkv, cache, group):
    bsz, t, _ = qkv.shape
    cache_len = cache.shape[1]
    cview = cache.reshape(bsz, cache_len, 2 * DSA_GW)
    new_spec = lambda part: pl.BlockSpec((1, t, DSA_GW), lambda b: (b, 0, part * DSA_GROUPS + group))
    cache_spec = lambda part: pl.BlockSpec((1, cache_len, DSA_GW), lambda b: (b, 0, part))
    out_spec = pl.BlockSpec((1, t, DSA_GW), lambda b: (b, 0, 0))
    return pl.pallas_call(
        functools.partial(_dsa_sample_kernel, cache_len=cache_len, dil=DSA_DILATIONS[group],
                          window=DSA_WINDOWS[group], group=group),
        out_shape=(_sds((bsz, t, DSA_GW), F32), _sds((bsz, t, DSA_GW), F32)),
        grid=(bsz,),
        in_specs=[new_spec(0), cache_spec(0), cache_spec(1), new_spec(1), new_spec(2)],
        out_specs=(out_spec, out_spec),
        compiler_params=_cp("parallel"),
        name=f"dsa_sample_g{group}",
    )(qkv, cview, cview, qkv, qkv)


def _dsa_combine_kernel(o0, o1, o2, l0, l1, l2, out_ref):
    ls = (l0[...], l1[...], l2[...])
    m = jnp.maximum(jnp.maximum(ls[0], ls[1]), ls[2])
    es = [jnp.exp(x - m) for x in ls]
    den = es[0] + es[1] + es[2]
    num = es[0] * o0[...] + es[1] * o1[...] + es[2] * o2[...]
    out_ref[...] = (num / den).astype(out_ref.dtype)


def dsa_combine(outs, lses):
    m, n = outs[0].shape
    tm = min(m, 1024)
    spec = pl.BlockSpec((tm, n), lambda i: (i, 0))
    return pl.pallas_call(
        _dsa_combine_kernel, out_shape=_sds((m, n), BF16), grid=(m // tm,),
        in_specs=[spec] * 6, out_specs=spec, compiler_params=_cp("parallel"), name="dsa_combine",
    )(*outs, *lses)


def _split_bf16(x):
    hi = x.astype(BF16)
    lo = (x - hi.astype(F32)).astype(BF16)
    return hi, lo


def _tri_sum(x, tri):
    hi, lo = _split_bf16(x)
    return (jnp.dot(hi, tri, preferred_element_type=F32) + jnp.dot(lo, tri, preferred_element_type=F32))


def _cumsum_kernel(x_ref, o_ref, *, tn):
    j = pl.program_id(0)
    t = x_ref.shape[1]
    src = lax.broadcasted_iota(jnp.int32, (t, tn), 0)
    dst = lax.broadcasted_iota(jnp.int32, (t, tn), 1) + j * tn
    o_ref[...] = _tri_sum(x_ref[...], jnp.where(src <= dst, 1.0, 0.0).astype(BF16))


def cumsum_lanes(x):
    r, t = x.shape
    tn = min(t, 512)
    return pl.pallas_call(
        functools.partial(_cumsum_kernel, tn=tn), out_shape=_sds((r, t), F32), grid=(t // tn,),
        in_specs=[pl.BlockSpec((r, t), lambda j: (0, 0))],
        out_specs=pl.BlockSpec((r, tn), lambda j: (0, j)),
        compiler_params=_cp("arbitrary"), name="fox_cumsum",
    )(x)


def _fox_prompt_kernel(q_ref, k_ref, v_ref, fk_ref, o_ref, *, scale, n_heads):
    i = pl.program_id(2)
    tq = q_ref.shape[1]
    hd = q_ref.shape[2] // n_heads
    nt = (((1,), (1,)), ((), ()))
    for ib in range(k_ref.shape[1] // tq):
        @pl.when(i == ib)
        def _(ib=ib):
            n_off = ib * tq
            diag = slice(n_off, n_off + tq)
            causal = (lax.broadcasted_iota(jnp.int32, (tq, tq), 1)
                      <= lax.broadcasted_iota(jnp.int32, (tq, tq), 0))
            for hh in range(n_heads):
                ls = slice(hh * hd, (hh + 1) * hd)
                q = q_ref[0, :, ls]
                fk = fk_ref[hh]
                s_d = lax.dot_general(q, k_ref[0, diag, ls], nt, preferred_element_type=F32) * scale - fk[:, diag]
                s_d = jnp.where(causal, s_d, NEG)
                m = jnp.max(s_d, axis=-1, keepdims=True)
                if ib:
                    s_o = (lax.dot_general(q, k_ref[0, :n_off, ls], nt, preferred_element_type=F32) * scale
                           - fk[:, :n_off])
                    m = jnp.maximum(m, jnp.max(s_o, axis=-1, keepdims=True))
                p_d = jnp.exp(s_d - m)
                l = jnp.sum(p_d, axis=-1, keepdims=True)
                o = jnp.dot(p_d.astype(BF16), v_ref[0, diag, ls], preferred_element_type=F32)
                if ib:
                    p_o = jnp.exp(s_o - m)
                    l = l + jnp.sum(p_o, axis=-1, keepdims=True)
                    o = o + jnp.dot(p_o.astype(BF16), v_ref[0, :n_off, ls], preferred_element_type=F32)
                o_ref[0, :, ls] = (o / l).astype(o_ref.dtype)


def fox_prompt_attention(qkv, f_cum, *, tq=256, n_heads=4):
    bsz, t, n = qkv.shape
    d = n // 3
    hw = n_heads * (d // FOX_HEADS)
    nhb = FOX_HEADS // n_heads
    return pl.pallas_call(
        functools.partial(_fox_prompt_kernel, scale=(d // FOX_HEADS) ** -0.5, n_heads=n_heads),
        out_shape=_sds((bsz, t, d), BF16),
        grid=(bsz, nhb, t // tq),
        in_specs=[pl.BlockSpec((1, tq, hw), lambda b, h, i: (b, i, h)),
                  pl.BlockSpec((1, t, hw), lambda b, h, i: (b, 0, nhb + h)),
                  pl.BlockSpec((1, t, hw), lambda b, h, i: (b, 0, 2 * nhb + h)),
                  pl.BlockSpec((n_heads, 1, t), lambda b, h, i: (b * nhb + h, 0, 0))],
        out_specs=pl.BlockSpec((1, tq, hw), lambda b, h, i: (b, i, h)),
        compiler_params=_cp("parallel", "parallel", "arbitrary"),
        name="fox_prompt_attn",
    )(qkv, qkv, qkv, f_cum)


def _fox_sample_kernel(pt_ref, q_ref, nkv_ref, nlf_ref, *rest, scale, t_new, heads, pps):
    kv_refs, lf_refs = rest[:pps], rest[pps:2 * pps]
    o_ref, pad_sc, m_sc, l_sc, acc_sc, suf_sc = rest[2 * pps:]
    j = pl.program_id(1)
    page = nlf_ref.shape[2]
    hd = q_ref.shape[2] // heads
    r = heads * t_new
    key_rows = 2 * heads
    pitch = pad_sc.shape[1] // page
    src = lax.broadcasted_iota(jnp.int32, (page, page), 0)
    dst = lax.broadcasted_iota(jnp.int32, (page, page), 1)

    def attend(kv_ref, slot, bias, valid, first):
        for key in range(page):
            pad_sc[slot, key * pitch:key * pitch + key_rows, :] = kv_ref[0, key * key_rows:(key + 1) * key_rows, :]
        rows_ref, row_pitch = pad_sc.at[slot], pitch
        s_rows = []
        for h in range(heads):
            k_h = rows_ref[pl.ds(h, page, stride=row_pitch), :].astype(BF16)
            s_h = lax.dot_general(q_ref[0, :, h * hd:(h + 1) * hd], k_h, (((1,), (1,)), ((), ())),
                                  preferred_element_type=F32)
            s_rows.append(s_h * scale + bias[h:h + 1, :])
        s = jnp.concatenate(s_rows, axis=0)
        if valid is not None:
            s = jnp.where(valid, s, NEG)
        m_cur = jnp.max(s, axis=-1, keepdims=True)
        m_new = m_cur if first else jnp.maximum(m_sc[...], m_cur)
        p = jnp.exp(s - m_new)
        pv = jnp.concatenate(
            [jnp.dot(p[h * t_new:(h + 1) * t_new].astype(BF16),
                     rows_ref[pl.ds(heads + h, page, stride=row_pitch), :].astype(BF16),
                     preferred_element_type=F32) for h in range(heads)], axis=0)
        l_cur = jnp.sum(p, axis=-1, keepdims=True)
        if first:
            l_sc[...] = l_cur
            acc_sc[...] = pv
        else:
            c = jnp.exp(m_sc[...] - m_new)
            l_sc[...] = l_sc[...] * c + l_cur
            acc_sc[...] = acc_sc[...] * c + pv
        m_sc[...] = m_new

    @pl.when(j == 0)
    def _():
        fq = _tri_sum(nlf_ref[0], jnp.where(src <= dst, 1.0, 0.0).astype(BF16))
        tq = lax.broadcasted_iota(jnp.int32, (r, page), 0) % t_new
        tk = lax.broadcasted_iota(jnp.int32, (r, page), 1)
        attend(nkv_ref, 0, -fq, tk <= tq, True)
        suf_sc[...] = jnp.zeros_like(suf_sc)

    @pl.when(j > 0)
    def _():
        later_mask = jnp.where(src > dst, 1.0, 0.0).astype(BF16)
        for u in range(pps):
            lf = lf_refs[u][0]
            suf = suf_sc[...]
            attend(kv_refs[u], u, _tri_sum(lf, later_mask) + suf, None, False)
            suf_sc[...] = suf + jnp.sum(lf, axis=-1, keepdims=True)

    @pl.when(j == pl.num_programs(1) - 1)
    def _():
        inv = 1.0 / l_sc[...]
        for h in range(heads):
            rs = slice(h * t_new, (h + 1) * t_new)
            o_ref[0, :, h * hd:(h + 1) * hd] = (acc_sc[rs, :] * inv[rs]).astype(o_ref.dtype)


def fox_sample_attention(qkv, kv_new, logf, kv_pool, logf_pool, page_table):
    bsz, t, n = qkv.shape
    d = n // 3
    heads = FOX_HEADS
    hd = d // heads
    n_pool, page = kv_pool.shape[0], kv_pool.shape[1]
    n_pages = page_table.shape[1]
    pps = FOX_PAGES_PER_STEP
    n_steps = n_pages // pps
    key_rows = 2 * heads
    rows = page * key_rows
    pool = kv_pool.reshape(n_pool, rows, hd)
    pool_lf = logf_pool.transpose(0, 2, 1)
    new_kv = jnp.pad(kv_new.reshape(bsz, t, key_rows, hd), ((0, 0), (0, page - t), (0, 0), (0, 0)))
    new_kv = new_kv.reshape(bsz, rows, hd)
    new_lf = jnp.pad(logf.transpose(0, 2, 1), ((0, 0), (0, 0), (0, page - t)))

    def page_idx(u):
        def index_map(b, j, pt_ref):
            return (pt_ref[b * n_pages + n_pages - 1 - (jnp.maximum(j, 1) - 1) * pps - u], 0, 0)
        return index_map

    own = lambda b, j, pt_ref: (b, 0, 0)
    grid_spec = pltpu.PrefetchScalarGridSpec(
        num_scalar_prefetch=1,
        grid=(bsz, n_steps + 1),
        in_specs=[pl.BlockSpec((1, t, d), own),
                  pl.BlockSpec((1, rows, hd), own),
                  pl.BlockSpec((1, heads, page), own)]
                 + [pl.BlockSpec((1, rows, hd), page_idx(u)) for u in range(pps)]
                 + [pl.BlockSpec((1, heads, page), page_idx(u)) for u in range(pps)],
        out_specs=pl.BlockSpec((1, t, d), own),
        scratch_shapes=[pltpu.VMEM((pps, page * (key_rows + FOX_PITCH_PAD), hd), F32),
                        pltpu.VMEM((heads * t, 1), F32), pltpu.VMEM((heads * t, 1), F32),
                        pltpu.VMEM((heads * t, hd), F32), pltpu.VMEM((heads, 1), F32)],
    )
    return pl.pallas_call(
        functools.partial(_fox_sample_kernel, scale=hd ** -0.5, t_new=t, heads=heads, pps=pps),
        out_shape=_sds((bsz, t, d), BF16),
        grid_spec=grid_spec,
        compiler_params=_cp("parallel", "arbitrary"),
        name="fox_sample_attn",
    )(page_table.reshape(-1), qkv, new_kv, new_lf, *([pool] * pps), *([pool_lf] * pps))


def _flat(x):
    return x.reshape(-1, x.shape[-1])


def _last_rows(hist, new, n):
    t = new.shape[1]
    if t >= n:
        return new[:, t - n:]
    return jnp.concatenate([hist.astype(new.dtype)[:, t:], new], axis=1)


def _conv_tiles(bsz, t):
    return dict(tm=256, nb=1) if t >= 256 else dict(tm=t, nb=bsz)


def _mixer_a(x, gain, hist, w_in, conv_w, w_out, shape):
    bsz, t, d = shape
    m = bsz * t
    tm = min(1024, m)
    gb, u = matmul(x, w_in, tm=tm, tn=512, parts=3, epilogue=_ep_gate_split, norm_gain=gain,
                   out_shapes=(_sds((m, d), F32), _sds((m, d), F32)),
                   out_specs=(_tile_spec(tm, 512), _tile_spec(tm, 512)), name="a_in")
    u3 = u.reshape(bsz, t, d)
    a = causal_conv_fused(u3, hist, conv_w, taps=A_CONV, gate=gb.reshape(bsz, t, d), tc=512,
                          **_conv_tiles(bsz, t))
    x = matmul_residual(_flat(a), w_out, x, name="a_out")
    return x, _last_rows(hist, u3, A_CONV - 1)


def _dsa_kv_rows(qkv3, group, n_last):
    qkv3 = qkv3[:, qkv3.shape[1] - n_last:]
    bsz, t, _ = qkv3.shape
    k0 = DSA_GROUPS * DSA_GW + group * DSA_GW
    v0 = 2 * DSA_GROUPS * DSA_GW + group * DSA_GW
    k = qkv3[:, :, k0:k0 + DSA_GW].reshape(bsz, t, DSA_HPG, DSA_HEAD_DIM)
    v = qkv3[:, :, v0:v0 + DSA_GW].reshape(bsz, t, DSA_HPG, DSA_HEAD_DIM)
    return jnp.stack([k, v], axis=2)


def _mixer_b(x, gain, caches, w_qkv, qn, kn, w_out, shape):
    bsz, t, d = shape
    (qkv,) = matmul_qk_norm(x, gain, w_qkv, qn, kn, head_dim=DSA_HEAD_DIM, name="b_qkv")
    qkv3 = qkv.reshape(bsz, t, -1)
    if caches is None:
        outs, lses = zip(*[dsa_prompt_group(qkv3, g) for g in range(DSA_GROUPS)])
        bufs = [_dsa_kv_rows(qkv3, g, min(DSA_WINDOWS[g], t)) for g in range(DSA_GROUPS)]
    else:
        outs, lses = zip(*[dsa_sample_group(qkv3, caches[g], g) for g in range(DSA_GROUPS)])
        bufs = [jnp.concatenate([caches[g][:, t:], _dsa_kv_rows(qkv3, g, t)], axis=1) for g in range(DSA_GROUPS)]
    o = dsa_combine([_flat(a) for a in outs], [_flat(a) for a in lses])
    x = matmul_residual(o, w_out, x, name="b_out")
    return x, bufs


def _mixer_c(x, gain, hist, w_in, conv_w, ln_g, ln_b, w_out, shape):
    bsz, t, d = shape
    m = bsz * t
    tm = min(1024, m)
    u = matmul(x, w_in, tm=tm, tn=512, parts=2, epilogue=_ep_glu, norm_gain=gain,
               out_shapes=_sds((m, d), F32), out_specs=_tile_spec(tm, 512), name="c_in")
    u3 = u.reshape(bsz, t, d)
    tiles = _conv_tiles(bsz, t)
    a = causal_conv_fused(u3, hist, conv_w, taps=C_CONV, ln=(ln_g, ln_b), tm=min(tiles["tm"], 128), tc=d,
                          nb=tiles["nb"])
    x = matmul_residual(_flat(a), w_out, x, name="c_out")
    return x, _last_rows(hist, u3, C_CONV - 1)


def _mixer_d(x, gain, paged, w_qkv, w_f, b_f, qn, kn, w_out, shape):
    bsz, t, d = shape
    m = bsz * t
    tm = min(1024, m)
    qkv, kv = matmul_qk_norm(x, gain, w_qkv, qn, kn, head_dim=d // FOX_HEADS, n_cols=3 * d, split_kv=True,
                             name="d_qkv")
    logf = matmul(x, w_f, tm=tm, tn=LANES, epilogue=_ep_logf, norm_gain=gain, out_shapes=_sds((m, LANES), F32),
                  out_specs=_tile_spec(tm, LANES), extras=(b_f,),
                  extra_specs=(pl.BlockSpec((1, LANES), lambda i, j: (0, 0)),), name="d_logf")
    logf = logf[:, :FOX_HEADS].reshape(bsz, t, FOX_HEADS)
    qkv3 = qkv.reshape(bsz, t, 3 * d)
    kv3 = kv.reshape(bsz, t, 2 * d)
    if paged is None:
        f_cum = cumsum_lanes(logf.transpose(0, 2, 1).reshape(bsz * FOX_HEADS, t))
        o = fox_prompt_attention(qkv3, f_cum.reshape(bsz * FOX_HEADS, 1, t))
    else:
        o = fox_sample_attention(qkv3, kv3, logf, *paged)
    x = matmul_residual(_flat(o), w_out, x, name="d_out")
    return x, kv3.reshape(bsz, t, 2, FOX_HEADS, d // FOX_HEADS), logf


def kernel(x_prompt, x_sample, state_conv_a, cache_dsa_g0, cache_dsa_g1, cache_dsa_g2, state_conv_c,
           cache_fox_kv, cache_fox_logf, state_ffn_conv, page_table, p_prompt, p_sample,
           norm_mix, norm_ffn, w_a_in, conv_a_w, w_a_out, w_b_qkv, qn_b, kn_b, w_b_out,
           w_c_in, conv_c_w, ln_c_g, ln_c_b, w_c_out, w_d_qkvf, b_f, qn_d, kn_d, w_d_out,
           w_ffn_up, conv_ffn_w, w_ffn_down, norm_ple, w_ple_gate, w_ple_proj):
    depth = norm_mix.shape[0]
    d = x_prompt.shape[-1]
    d_ff = w_ffn_down.shape[1]
    shapes = (x_prompt.shape, x_sample.shape)
    xs = [_flat(x_prompt), _flat(x_sample)]
    ps = tuple(p.reshape(depth, -1, p.shape[-1]) for p in (p_prompt, p_sample))
    bf = lambda w: w.astype(BF16)
    w_up_bf, w_down_bf, w_gate_bf, w_proj_bf = bf(w_ffn_up), bf(w_ffn_down), bf(w_ple_gate), bf(w_ple_proj)

    w_d_qkv = bf(w_d_qkvf)
    n_f = w_d_qkvf.shape[1] - 3 * d
    w_d_f = bf(jnp.pad(w_d_qkvf[:, 3 * d:], ((0, 0), (0, LANES - n_f))))
    b_f_pad = jnp.pad(b_f.astype(F32), (0, LANES - n_f)).reshape(1, LANES)
    caches = (cache_dsa_g0, cache_dsa_g1, cache_dsa_g2)
    paged = (cache_fox_kv, cache_fox_logf, page_table)

    mixer_out = [dict(), dict()]
    ffn_hist = [[], []]
    for i in range(depth):
        kind = i % 4
        for grp in range(2):
            shape = shapes[grp]
            bsz, t, _ = shape
            x = xs[grp]
            out = mixer_out[grp]
            gain = norm_mix[i]
            if kind == 0:
                hist = jnp.zeros((bsz, A_CONV - 1, d), F32) if grp == 0 else state_conv_a
                x, out["conv_a"] = _mixer_a(x, gain, hist, bf(w_a_in), conv_a_w, bf(w_a_out), shape)
            elif kind == 1:
                x, out["dsa"] = _mixer_b(x, gain, None if grp == 0 else caches, bf(w_b_qkv), qn_b, kn_b,
                                         bf(w_b_out), shape)
            elif kind == 2:
                hist = jnp.zeros((bsz, C_CONV - 1, d), F32) if grp == 0 else state_conv_c
                x, out["conv_c"] = _mixer_c(x, gain, hist, bf(w_c_in), conv_c_w, ln_c_g, ln_c_b, bf(w_c_out), shape)
            else:
                x, out["fox_kv"], out["fox_logf"] = _mixer_d(
                    x, gain, None if grp == 0 else paged, w_d_qkv, w_d_f, b_f_pad, qn_d, kn_d, bf(w_d_out), shape)
            hist = jnp.zeros((bsz, FFN_CONV - 1, d_ff), F32) if grp == 0 else state_ffn_conv[i]
            x, g_last = ffn_fused(x, norm_ffn[i], hist, w_up_bf, conv_ffn_w, w_down_bf, i, shape)
            ffn_hist[grp].append(_last_rows(hist, g_last[:, SUBLANES - min(t, SUBLANES):], FFN_CONV - 1))
            xs[grp] = per_layer_embed(x, norm_ple[i], ps[grp], w_gate_bf, w_proj_bf, i)

    mp, ms = mixer_out
    return (xs[0].reshape(shapes[0]), xs[1].reshape(shapes[1]),
            mp["conv_a"], ms["conv_a"],
            mp["dsa"][0], ms["dsa"][0], mp["dsa"][1], ms["dsa"][1], mp["dsa"][2], ms["dsa"][2],
            mp["conv_c"], ms["conv_c"],
            mp["fox_kv"], ms["fox_kv"], mp["fox_logf"], ms["fox_logf"],
            jnp.stack(ffn_hist[0]), jnp.stack(ffn_hist[1]))
```

```python
import functools

import jax
import jax.numpy as jnp
from jax import lax
from jax.experimental import pallas as pl
from jax.experimental.pallas import tpu as pltpu

F32 = jnp.float32
BF16 = jnp.bfloat16
EPS = 1e-6
NEG = -1e30

VMEM_LIMIT_BYTES = 56 * 1024 * 1024
FFN_VMEM_LIMIT_BYTES = 58 * 1024 * 1024
LANES = 128
SUBLANES = 8
BF16_SUBLANES = 16
MXU_COLS = 256

A_CONV = 3
C_CONV = 31
FFN_CONV = 3
DSA_WINDOWS = (128, 512, 2048)
DSA_DILATIONS = (1, 4, 16)
DSA_GROUPS = 3
DSA_HPG = 8
DSA_HEAD_DIM = 64
DSA_HEADS = DSA_GROUPS * DSA_HPG
DSA_GW = DSA_HPG * DSA_HEAD_DIM
DSA_KEYS = 128
FOX_HEADS = 16
Q_BLOCK = 128
FOX_PAGES_PER_STEP = 4
FOX_RING_SLOTS = 3
FOX_PITCH_PAD = 8


def _cp(*sem, vmem_limit_bytes=VMEM_LIMIT_BYTES):
    return pltpu.CompilerParams(dimension_semantics=sem, vmem_limit_bytes=vmem_limit_bytes)


def _sds(shape, dtype):
    return jax.ShapeDtypeStruct(shape, dtype)


def _rmsnorm(x, g):
    ms = jnp.mean(x * x, axis=-1, keepdims=True)
    return x * lax.rsqrt(ms + EPS) * g


def _mm_body(*refs, parts, n_extra, epilogue, norm):
    if norm:
        x_ref, g_ref, a_sc = refs[0], refs[1], refs[-1]
        refs = refs[2:-1]

        @pl.when(pl.program_id(1) == 0)
        def _():
            a_sc[...] = _rmsnorm(x_ref[...], g_ref[...]).astype(BF16)

        a = a_sc[...]
    else:
        a = refs[0][...]
        refs = refs[1:]
    extras = refs[parts:parts + n_extra]
    outs = refs[parts + n_extra:]

    def dot(cols):
        return [jnp.dot(a, refs[p][:, cols], preferred_element_type=F32) for p in range(parts)]

    epilogue(dot, extras, outs)


def _col_chunks(tn):
    step = min(tn, MXU_COLS)
    return [slice(c0, c0 + step) for c0 in range(0, tn, step)]


def matmul(a, w, *, tm, tn, epilogue, out_shapes, out_specs, parts=1, extras=(), extra_specs=(),
           norm_gain=None, name):
    m, k = a.shape
    np_ = w.shape[1] // parts
    nj = np_ // tn
    tm = min(tm, m)
    norm = norm_gain is not None
    in_specs = [pl.BlockSpec((tm, k), lambda i, j: (i, 0))]
    args = [a]
    if norm:
        in_specs.append(pl.BlockSpec((1, k), lambda i, j: (0, 0)))
        args.append(norm_gain.reshape(1, k))
    in_specs += [pl.BlockSpec((k, tn), functools.partial(lambda i, j, p: (0, p * nj + j), p=p))
                 for p in range(parts)]
    in_specs += list(extra_specs)
    return pl.pallas_call(
        functools.partial(_mm_body, parts=parts, n_extra=len(extras), epilogue=epilogue, norm=norm),
        out_shape=out_shapes,
        grid=(m // tm, nj),
        in_specs=in_specs,
        out_specs=out_specs,
        scratch_shapes=[pltpu.VMEM((tm, k), BF16)] if norm else [],
        compiler_params=_cp("parallel", "arbitrary"),
        name=name,
    )(*args, *([w] * parts), *extras)


def _tile_spec(tm, tn):
    return pl.BlockSpec((tm, tn), lambda i, j: (i, j))


def _ep_residual(dot, extras, outs):
    for cols in _col_chunks(outs[0].shape[1]):
        outs[0][:, cols] = extras[0][:, cols] + dot(cols)[0]


def matmul_residual(a, w, res, *, tm=512, name):
    m, n = a.shape[0], w.shape[1]
    tm, tn = min(tm, m), n
    return matmul(a, w, tm=tm, tn=tn, epilogue=_ep_residual, out_shapes=_sds((m, n), F32),
                  out_specs=_tile_spec(tm, tn), extras=(res,), extra_specs=(_tile_spec(tm, tn),), name=name)


def _ep_gate_split(dot, extras, outs):
    for cols in _col_chunks(outs[0].shape[1]):
        gb, gc, v = dot(cols)
        outs[0][:, cols] = gb
        outs[1][:, cols] = gc * v


def _ep_glu(dot, extras, outs):
    for cols in _col_chunks(outs[0].shape[1]):
        a, g = dot(cols)
        outs[0][:, cols] = a * jax.nn.sigmoid(g)


def _ep_qk_norm(dot, extras, outs, *, head_dim, blocks_per_part, split_kv):
    j = pl.program_id(1)
    gain = extras[0][0]
    tn = outs[0].shape[1]

    def store(cols, val, with_kv):
        outs[0][:, cols] = val.astype(outs[0].dtype)
        if split_kv and with_kv:
            outs[1][:, cols] = val

    def tile(norm, with_kv):
        for chunk in _col_chunks(tn):
            acc = dot(chunk)[0]
            if not norm:
                store(chunk, acc, with_kv)
                continue
            for c0 in range(0, acc.shape[1], LANES):
                cols = slice(chunk.start + c0, chunk.start + c0 + LANES)
                x = acc[:, c0:c0 + LANES]
                sq = x * x
                if head_dim == LANES:
                    ms = jnp.sum(sq, axis=-1, keepdims=True) * (1.0 / head_dim)
                else:
                    lane = lax.broadcasted_iota(jnp.int32, sq.shape, 1)
                    first = lane < head_dim
                    s_lo = jnp.sum(jnp.where(first, sq, 0.0), axis=-1, keepdims=True)
                    s_hi = jnp.sum(jnp.where(first, 0.0, sq), axis=-1, keepdims=True)
                    ms = jnp.where(first, s_lo, s_hi) * (1.0 / head_dim)
                store(cols, x * lax.rsqrt(ms + EPS) * gain[:, cols], with_kv)

    @pl.when(j < blocks_per_part)
    def _():
        tile(True, False)

    @pl.when((j >= blocks_per_part) & (j < 2 * blocks_per_part))
    def _():
        tile(True, True)

    @pl.when(j >= 2 * blocks_per_part)
    def _():
        tile(False, True)


def matmul_qk_norm(x, norm_gain, w, qn, kn, *, head_dim, n_cols=None, split_kv=False, tm=1024, tn=512, name):
    m, n = x.shape[0], n_cols or w.shape[1]
    tm = min(tm, m)
    bpp = n // 3 // tn
    reps = tn // head_dim
    gains = jnp.stack([jnp.tile(qn, reps), jnp.tile(kn, reps), jnp.ones((tn,), F32)]).reshape(3, 1, tn)
    gain_spec = pl.BlockSpec((1, 1, tn), lambda i, j: (j // bpp, 0, 0))
    if split_kv:
        shapes = (_sds((m, n), BF16), _sds((m, n - n // 3), F32))
        specs = (_tile_spec(tm, tn), pl.BlockSpec((tm, tn), lambda i, j: (i, jnp.maximum(j - bpp, 0))))
    else:
        shapes, specs = (_sds((m, n), F32),), (_tile_spec(tm, tn),)
    return matmul(x, w, tm=tm, tn=tn, norm_gain=norm_gain,
                  epilogue=functools.partial(_ep_qk_norm, head_dim=head_dim, blocks_per_part=bpp, split_kv=split_kv),
                  out_shapes=shapes, out_specs=specs, extras=(gains,), extra_specs=(gain_spec,), name=name)


def _log_sigmoid(x):
    return jnp.minimum(x, 0.0) - jnp.log1p(jnp.exp(-jnp.abs(x)))


def _ep_logf(dot, extras, outs):
    outs[0][...] = _log_sigmoid(dot(slice(None))[0] + extras[0][...])


def _ple_body(x_ref, g_ref, wg_ref, p_ref, wp_ref, o_ref, h_sc, *, tn):
    j = pl.program_id(1)

    @pl.when(j == 0)
    def _():
        h_sc[...] = _rmsnorm(x_ref[...], g_ref[...]).astype(BF16)

    h = h_sc[...]
    p = p_ref[...].astype(BF16)
    for cols in _col_chunks(tn):
        gate = jax.nn.sigmoid(jnp.dot(h, wg_ref[:, cols], preferred_element_type=F32))
        proj = jnp.dot(p, wp_ref[:, cols], preferred_element_type=F32)
        x = x_ref[:, pl.ds(pl.multiple_of(j * tn + cols.start, cols.stop - cols.start), cols.stop - cols.start)]
        o_ref[:, cols] = x + gate * proj


def per_layer_embed(x, norm_gain, p, w_gate, w_proj, layer, *, tm=1024, tn=1024):
    m, d = x.shape
    tm = min(tm, m)
    pd = p.shape[2]
    return pl.pallas_call(
        functools.partial(_ple_body, tn=tn),
        out_shape=_sds((m, d), F32),
        grid=(m // tm, d // tn),
        in_specs=[pl.BlockSpec((tm, d), lambda i, j: (i, 0)),
                  pl.BlockSpec((1, d), lambda i, j: (0, 0)),
                  pl.BlockSpec((None, d, tn), lambda i, j: (layer, 0, j)),
                  pl.BlockSpec((None, tm, pd), lambda i, j: (layer, i, 0)),
                  pl.BlockSpec((None, pd, tn), lambda i, j: (layer, 0, j))],
        out_specs=_tile_spec(tm, tn),
        scratch_shapes=[pltpu.VMEM((tm, d), BF16)],
        compiler_params=_cp("parallel", "arbitrary"),
        name="per_layer_embed",
    )(x, norm_gain.reshape(1, d), w_gate, p, w_proj)


def _causal_taps(prev, cur, w, taps):
    p_rows, tm = prev.shape[0], cur.shape[0]
    ext = jnp.concatenate([prev, cur], axis=0)
    n = p_rows + tm
    acc = None
    for rot in range(SUBLANES):
        ks = [k for k in range(taps) if (p_rows - (taps - 1) + k) % SUBLANES == rot]
        if not ks:
            continue
        shifted = ext if rot == 0 else pltpu.roll(ext, n - rot, axis=0)
        for k in ks:
            base = p_rows - (taps - 1) + k - rot
            term = w[k:k + 1, :] * shifted[base:base + tm, :]
            acc = term if acc is None else acc + term
    return acc


def _conv_gate_kernel(cur_ref, prev_ref, hist_ref, w_ref, gate_ref, o_ref, *, taps, chunk):
    at_start = pl.program_id(1) == 0
    nb, tm, tc = cur_ref.shape
    for b in range(nb):
        for c0 in range(0, tc, chunk):
            cs = slice(c0, c0 + chunk)
            prev = jnp.where(at_start, hist_ref[b, :, cs], prev_ref[b, :, cs])
            y = _causal_taps(prev, cur_ref[b, :, cs], w_ref[:, cs], taps)
            o_ref[b, :, cs] = (gate_ref[b, :, cs] * y).astype(o_ref.dtype)


def _conv_ln_kernel(cur_ref, prev_ref, hist_ref, w_ref, lng_ref, lnb_ref, o_ref, c_sc, *, taps, chunk):
    at_start = pl.program_id(1) == 0
    nb, tm, tc = cur_ref.shape
    for b in range(nb):
        for c0 in range(0, tc, chunk):
            cs = slice(c0, c0 + chunk)
            prev = jnp.where(at_start, hist_ref[b, :, cs], prev_ref[b, :, cs])
            c_sc[:, cs] = _causal_taps(prev, cur_ref[b, :, cs], w_ref[:, cs], taps)
        c = c_sc[...]
        mu = jnp.mean(c, axis=-1, keepdims=True)
        xc = c - mu
        var = jnp.mean(xc * xc, axis=-1, keepdims=True)
        y = xc * lax.rsqrt(var + EPS) * lng_ref[...] + lnb_ref[...]
        o_ref[b] = (y * jax.nn.sigmoid(y)).astype(o_ref.dtype)


def _pad_hist(hist, p_rows):
    b, k1, c = hist.shape
    return jnp.concatenate([jnp.zeros((b, p_rows - k1, c), F32), hist.astype(F32)], axis=1)


def causal_conv_fused(src, hist, w, *, taps, gate=None, ln=None, tm, tc, nb):
    bsz, t, c = src.shape
    p_rows = SUBLANES if taps - 1 <= SUBLANES else 32
    hist_p = _pad_hist(hist, p_rows)
    tm = min(tm, t)
    n_t = t // tm
    chunk = min(tc, 256)
    if t >= p_rows:
        prev_src = src
        per = tm // p_rows
        prev_spec = pl.BlockSpec((nb, p_rows, tc), lambda b, i, j: (b, jnp.maximum(i * per - 1, 0), j))
    else:
        prev_src = hist_p
        prev_spec = pl.BlockSpec((nb, p_rows, tc), lambda b, i, j: (b, 0, j))
    cur_spec = pl.BlockSpec((nb, tm, tc), lambda b, i, j: (b, i, j))
    hist_spec = pl.BlockSpec((nb, p_rows, tc), lambda b, i, j: (b, 0, j))
    w_spec = pl.BlockSpec((taps, tc), lambda b, i, j: (0, j))
    grid = (bsz // nb, n_t, c // tc)
    if ln is not None:
        lng, lnb = ln
        vec_spec = pl.BlockSpec((1, tc), lambda b, i, j: (0, j))
        return pl.pallas_call(
            functools.partial(_conv_ln_kernel, taps=taps, chunk=chunk),
            out_shape=_sds((bsz, t, c), BF16), grid=grid,
            in_specs=[cur_spec, prev_spec, hist_spec, w_spec, vec_spec, vec_spec],
            out_specs=cur_spec,
            scratch_shapes=[pltpu.VMEM((tm, tc), F32)],
            compiler_params=_cp("parallel", "arbitrary", "arbitrary"),
            name="conv_ln",
        )(src, prev_src, hist_p, w, lng.reshape(1, c), lnb.reshape(1, c))
    return pl.pallas_call(
        functools.partial(_conv_gate_kernel, taps=taps, chunk=chunk),
        out_shape=_sds((bsz, t, c), BF16), grid=grid,
        in_specs=[cur_spec, prev_spec, hist_spec, w_spec, cur_spec],
        out_specs=cur_spec,
        compiler_params=_cp("parallel", "arbitrary", "arbitrary"),
        name="conv_gate",
    )(src, prev_src, hist_p, w, gate)


def _ffn_kernel(x_ref, halo_ref, hist_ref, gain_ref, wg_ref, wu_ref, cw_ref, wd_ref, o_ref, gh_ref, h_sc,
                *, n_seq, tiles_per_seq, sub):
    i, c = pl.program_id(0), pl.program_id(1)
    tm = x_ref.shape[0]
    halo = h_sc.shape[0] - tm

    @pl.when(c == 0)
    def _():
        x = x_ref[...]
        o_ref[...] = x
        h_sc[halo:, :] = _rmsnorm(x, gain_ref[...]).astype(BF16)
        if halo:
            h_sc[:halo, :] = _rmsnorm(halo_ref[...], gain_ref[...]).astype(BF16)

    d = None
    for c0 in range(0, wg_ref.shape[1], sub):
        cs = slice(c0, c0 + sub)
        g_ext = jnp.dot(h_sc[...], wg_ref[:, cs], preferred_element_type=F32)
        u = jnp.dot(h_sc[halo:, :], wu_ref[:, cs], preferred_element_type=F32)
        cw = cw_ref[:, cs]
        if halo:
            at_start = (i % tiles_per_seq) == 0
            prev = jnp.where(at_start, hist_ref[0, :, cs], g_ext[halo - SUBLANES:halo])
            g = g_ext[halo:]
            y = _causal_taps(prev, g, cw, FFN_CONV)
            gh_ref[0, :, cs] = g[tm - SUBLANES:]
        else:
            rows = tm // n_seq
            ys = []
            for b in range(n_seq):
                g = g_ext[b * rows:(b + 1) * rows]
                ys.append(_causal_taps(hist_ref[b, :, cs], g, cw, FFN_CONV))
                gh_ref[b, :, cs] = g[rows - SUBLANES:]
            y = jnp.concatenate(ys, axis=0)
        a = (y * jax.nn.sigmoid(y) * u).astype(BF16)
        part = jnp.dot(a, wd_ref[cs, :], preferred_element_type=F32)
        d = part if d is None else d + part

    o_ref[...] += d


def ffn_fused(x, norm_gain, hist, w_up, conv_w, w_down, layer, shape, *, tm=1024, tf=512):
    bsz, t, d = shape
    m = bsz * t
    d_ff = w_down.shape[1]
    nc = d_ff // tf
    hist_p = _pad_hist(hist, SUBLANES)
    if t >= tm:
        n_seq, tiles_per_seq, halo = 1, t // tm, BF16_SUBLANES
        seq_blk = lambda i: i // tiles_per_seq
    else:
        tm, n_seq, tiles_per_seq, halo = m, bsz, 1, 0
        seq_blk = lambda i: 0
    per = tm // BF16_SUBLANES
    hist_spec = pl.BlockSpec((n_seq, SUBLANES, tf), lambda i, c: (seq_blk(i), 0, c))
    x_new, g_tail = pl.pallas_call(
        functools.partial(_ffn_kernel, n_seq=n_seq, tiles_per_seq=tiles_per_seq, sub=tf),
        out_shape=(_sds((m, d), F32), _sds((bsz * tiles_per_seq, SUBLANES, d_ff), F32)),
        grid=(m // tm, nc),
        in_specs=[pl.BlockSpec((tm, d), lambda i, c: (i, 0)),
                  pl.BlockSpec((BF16_SUBLANES, d), lambda i, c: (jnp.maximum(i * per - 1, 0), 0)),
                  hist_spec,
                  pl.BlockSpec((1, d), lambda i, c: (0, 0)),
                  pl.BlockSpec((None, d, tf), lambda i, c: (layer, 0, c)),
                  pl.BlockSpec((None, d, tf), lambda i, c: (layer, 0, nc + c)),
                  pl.BlockSpec((None, FFN_CONV, tf), lambda i, c: (layer, 0, c)),
                  pl.BlockSpec((None, tf, d), lambda i, c: (layer, c, 0))],
        out_specs=(pl.BlockSpec((tm, d), lambda i, c: (i, 0)),
                   pl.BlockSpec((n_seq, SUBLANES, tf), lambda i, c: (i, 0, c))),
        scratch_shapes=[pltpu.VMEM((halo + tm, d), BF16)],
        compiler_params=_cp("parallel", "arbitrary", vmem_limit_bytes=FFN_VMEM_LIMIT_BYTES),
        name="ffn",
    )(x, x, hist_p, norm_gain.reshape(1, d), w_up, w_up, conv_w, w_down)
    return x_new, g_tail.reshape(bsz, tiles_per_seq, SUBLANES, d_ff)[:, -1]


def _alibi_slope(head):
    return 2.0 ** (-8.0 * (head + 1) / DSA_HEADS)


def _dsa_prompt_kernel(q_ref, kp_ref, kc_ref, vp_ref, vc_ref, o_ref, lse_ref, *, dil, group):
    i = pl.program_id(2)
    tq = Q_BLOCK
    q = q_ref[0].astype(BF16)
    k = jnp.concatenate([kp_ref[0], kc_ref[0]], axis=0).astype(BF16)
    v = jnp.concatenate([vp_ref[0], vc_ref[0]], axis=0).astype(BF16)
    row = lax.broadcasted_iota(jnp.int32, (tq, 2 * tq), 0)
    col = lax.broadcasted_iota(jnp.int32, (tq, 2 * tq), 1)
    dist = row + tq - col
    valid = (dist >= 0) & (dist <= DSA_KEYS) & ((col >= tq) | (i > 0))
    distf = dist.astype(F32) * float(dil)
    for h in range(DSA_HPG):
        sl = slice(h * DSA_HEAD_DIM, (h + 1) * DSA_HEAD_DIM)
        s = lax.dot_general(q[:, sl], k[:, sl], (((1,), (1,)), ((), ())), preferred_element_type=F32)
        s = s * (DSA_HEAD_DIM ** -0.5) - _alibi_slope(group * DSA_HPG + h) * distf
        s = jnp.where(valid, s, NEG)
        m = jnp.max(s, axis=-1, keepdims=True)
        p = jnp.exp(s - m)
        l = jnp.sum(p, axis=-1, keepdims=True)
        o = jnp.dot(p.astype(BF16), v[:, sl], preferred_element_type=F32) / l
        o_ref[0, :, sl] = o
        lse_ref[0, :, sl] = jnp.broadcast_to(m + jnp.log(l), (tq, DSA_HEAD_DIM))


def _dsa_dilated_kernel(q_ref, k_ref, v_ref, o_ref, lse_ref, *, dil, group):
    c = pl.program_id(1)
    tq = Q_BLOCK
    s_len = q_ref.shape[1] // dil
    heads_per_block = LANES // DSA_HEAD_DIM

    def geometry(n_keys):
        row = lax.broadcasted_iota(jnp.int32, (tq, n_keys), 0)
        col = lax.broadcasted_iota(jnp.int32, (tq, n_keys), 1)
        dist = row + (n_keys - tq) - col
        return (dist >= 0) & (dist <= DSA_KEYS), dist.astype(F32) * float(dil)

    first_block, later_block = geometry(tq), geometry(2 * tq)
    slopes = []
    for hh in range(heads_per_block):
        head = (group * DSA_HPG + c * heads_per_block + hh + 1).astype(F32)
        slopes.append(jnp.exp2(jnp.full((1, 1), head, F32) * (-8.0 / DSA_HEADS)))
    for r in range(dil):
        rows = pl.ds(r, s_len, stride=dil)
        q = q_ref[0, rows, :].astype(BF16)
        k = k_ref[0, rows, :].astype(BF16)
        v = v_ref[0, rows, :].astype(BF16)
        o_blocks, lse_blocks = [], []
        for qb in range(s_len // tq):
            lo = max(qb - 1, 0) * tq
            valid, distf = later_block if qb else first_block
            o_heads, lse_heads = [], []
            for hh in range(heads_per_block):
                sl = slice(hh * DSA_HEAD_DIM, (hh + 1) * DSA_HEAD_DIM)
                s = lax.dot_general(q[qb * tq:(qb + 1) * tq, sl], k[lo:(qb + 1) * tq, sl],
                                    (((1,), (1,)), ((), ())), preferred_element_type=F32)
                s = s * (DSA_HEAD_DIM ** -0.5) - slopes[hh] * distf
                s = jnp.where(valid, s, NEG)
                m = jnp.max(s, axis=-1, keepdims=True)
                p = jnp.exp(s - m)
                l = jnp.sum(p, axis=-1, keepdims=True)
                o_heads.append(jnp.dot(p.astype(BF16), v[lo:(qb + 1) * tq, sl], preferred_element_type=F32) / l)
                lse_heads.append(jnp.broadcast_to(m + jnp.log(l), (tq, DSA_HEAD_DIM)))
            o_blocks.append(jnp.concatenate(o_heads, axis=1))
            lse_blocks.append(jnp.concatenate(lse_heads, axis=1))
        o_ref[0, rows, :] = jnp.concatenate(o_blocks, axis=0)
        lse_ref[0, rows, :] = jnp.concatenate(lse_blocks, axis=0)


def dsa_dilated_group(qkv, group):
    bsz, t, n = qkv.shape
    blocks_per_part = n // 3 // LANES
    blocks_per_group = DSA_GW // LANES

    def spec(part):
        return pl.BlockSpec((1, t, LANES), lambda b, c: (b, 0, part * blocks_per_part + group * blocks_per_group + c))

    out_spec = pl.BlockSpec((1, t, LANES), lambda b, c: (b, 0, c))
    return pl.pallas_call(
        functools.partial(_dsa_dilated_kernel, dil=DSA_DILATIONS[group], group=group),
        out_shape=(_sds((bsz, t, DSA_GW), F32), _sds((bsz, t, DSA_GW), F32)),
        grid=(bsz, blocks_per_group),
        in_specs=[spec(0), spec(1), spec(2)],
        out_specs=(out_spec, out_spec),
        compiler_params=_cp("parallel", "parallel"),
        name=f"dsa_prompt_g{group}",
    )(qkv, qkv, qkv)


def dsa_prompt_group(qkv, group):
    bsz, t, n = qkv.shape
    dil = DSA_DILATIONS[group]
    if dil > 1:
        return dsa_dilated_group(qkv, group)
    s_len = t // dil
    nblk = n // DSA_GW
    view = qkv.reshape(bsz, s_len, dil * n)
    tq = Q_BLOCK

    def spec(part, prev):
        if prev:
            return pl.BlockSpec((1, tq, DSA_GW),
                                lambda b, r, i: (b, jnp.maximum(i - 1, 0), r * nblk + part * DSA_GROUPS + group))
        return pl.BlockSpec((1, tq, DSA_GW), lambda b, r, i: (b, i, r * nblk + part * DSA_GROUPS + group))

    out_spec = pl.BlockSpec((1, tq, DSA_GW), lambda b, r, i: (b, i, r))
    o, lse = pl.pallas_call(
        functools.partial(_dsa_prompt_kernel, dil=dil, group=group),
        out_shape=(_sds((bsz, s_len, dil * DSA_GW), F32), _sds((bsz, s_len, dil * DSA_GW), F32)),
        grid=(bsz, dil, s_len // tq),
        in_specs=[spec(0, False), spec(1, True), spec(1, False), spec(2, True), spec(2, False)],
        out_specs=(out_spec, out_spec),
        compiler_params=_cp("parallel", "parallel", "arbitrary"),
        name=f"dsa_prompt_g{group}",
    )(view, view, view, view, view)
    return o.reshape(bsz, t, DSA_GW), lse.reshape(bsz, t, DSA_GW)


def _dsa_sample_kernel(q_ref, kc_ref, vc_ref, kn_ref, vn_ref, o_ref, lse_ref, *, cache_len, dil, window, group):
    t = q_ref.shape[1]
    pad = jnp.zeros((LANES - t, DSA_GW), F32)
    q = q_ref[0].astype(BF16)
    k = jnp.concatenate([kc_ref[0], kn_ref[0], pad], axis=0).astype(BF16)
    v = jnp.concatenate([vc_ref[0], vn_ref[0], pad], axis=0).astype(BF16)
    rows = cache_len + LANES
    qi = lax.broadcasted_iota(jnp.int32, (t, rows), 0)
    col = lax.broadcasted_iota(jnp.int32, (t, rows), 1)
    dist = cache_len + qi - col
    valid = (dist >= 0) & (dist <= window) & ((dist & (dil - 1)) == 0)
    distf = dist.astype(F32)
    for h in range(DSA_HPG):
        sl = slice(h * DSA_HEAD_DIM, (h + 1) * DSA_HEAD_DIM)
        s = lax.dot_general(q[:, sl], k[:, sl], (((1,), (1,)), ((), ())), preferred_element_type=F32)
        s = s * (DSA_HEAD_DIM ** -0.5) - _alibi_slope(group * DSA_HPG + h) * distf
        s = jnp.where(valid, s, NEG)
        m = jnp.max(s, axis=-1, keepdims=True)
        p = jnp.exp(s - m)
        l = jnp.sum(p, axis=-1, keepdims=True)
        o = jnp.dot(p.astype(BF16), v[:, sl], preferred_element_type=F32) / l
        o_ref[0, :, sl] = o
        lse_ref[0, :, sl] = jnp.broadcast_to(m + jnp.log(l), (t, DSA_HEAD_DIM))


def dsa_sample_group(qkv, cache, group):
    bsz, t, _ = qkv.shape
    cache_len = cache.shape[1]
    cview = cache.reshape(bsz, cache_len, 2 * DSA_GW)
    new_spec = lambda part: pl.BlockSpec((1, t, DSA_GW), lambda b: (b, 0, part * DSA_GROUPS + group))
    cache_spec = lambda part: pl.BlockSpec((1, cache_len, DSA_GW), lambda b: (b, 0, part))
    out_spec = pl.BlockSpec((1, t, DSA_GW), lambda b: (b, 0, 0))
    return pl.pallas_call(
        functools.partial(_dsa_sample_kernel, cache_len=cache_len, dil=DSA_DILATIONS[group],
                          window=DSA_WINDOWS[group], group=group),
        out_shape=(_sds((bsz, t, DSA_GW), F32), _sds((bsz, t, DSA_GW), F32)),
        grid=(bsz,),
        in_specs=[new_spec(0), cache_spec(0), cache_spec(1), new_spec(1), new_spec(2)],
        out_specs=(out_spec, out_spec),
        compiler_params=_cp("parallel"),
        name=f"dsa_sample_g{group}",
    )(qkv, cview, cview, qkv, qkv)


def _dsa_combine_kernel(o0, o1, o2, l0, l1, l2, out_ref):
    ls = (l0[...], l1[...], l2[...])
    m = jnp.maximum(jnp.maximum(ls[0], ls[1]), ls[2])
    es = [jnp.exp(x - m) for x in ls]
    den = es[0] + es[1] + es[2]
    num = es[0] * o0[...] + es[1] * o1[...] + es[2] * o2[...]
    out_ref[...] = (num / den).astype(out_ref.dtype)


def dsa_combine(outs, lses):
    m, n = outs[0].shape
    tm = min(m, 1024)
    spec = pl.BlockSpec((tm, n), lambda i: (i, 0))
    return pl.pallas_call(
        _dsa_combine_kernel, out_shape=_sds((m, n), BF16), grid=(m // tm,),
        in_specs=[spec] * 6, out_specs=spec, compiler_params=_cp("parallel"), name="dsa_combine",
    )(*outs, *lses)


def _split_bf16(x):
    hi = x.astype(BF16)
    lo = (x - hi.astype(F32)).astype(BF16)
    return hi, lo


def _tri_sum(x, tri):
    hi, lo = _split_bf16(x)
    return (jnp.dot(hi, tri, preferred_element_type=F32) + jnp.dot(lo, tri, preferred_element_type=F32))


def _cumsum_kernel(x_ref, o_ref, *, tn):
    j = pl.program_id(0)
    t = x_ref.shape[1]
    src = lax.broadcasted_iota(jnp.int32, (t, tn), 0)
    dst = lax.broadcasted_iota(jnp.int32, (t, tn), 1) + j * tn
    o_ref[...] = _tri_sum(x_ref[...], jnp.where(src <= dst, 1.0, 0.0).astype(BF16))


def cumsum_lanes(x):
    r, t = x.shape
    tn = min(t, 512)
    return pl.pallas_call(
        functools.partial(_cumsum_kernel, tn=tn), out_shape=_sds((r, t), F32), grid=(t // tn,),
        in_specs=[pl.BlockSpec((r, t), lambda j: (0, 0))],
        out_specs=pl.BlockSpec((r, tn), lambda j: (0, j)),
        compiler_params=_cp("arbitrary"), name="fox_cumsum",
    )(x)


def _fox_prompt_kernel(q_ref, k_ref, v_ref, fk_ref, o_ref, *, scale, n_heads):
    i = pl.program_id(2)
    tq = q_ref.shape[1]
    hd = q_ref.shape[2] // n_heads
    nt = (((1,), (1,)), ((), ()))
    for ib in range(k_ref.shape[1] // tq):
        @pl.when(i == ib)
        def _(ib=ib):
            n_off = ib * tq
            diag = slice(n_off, n_off + tq)
            causal = (lax.broadcasted_iota(jnp.int32, (tq, tq), 1)
                      <= lax.broadcasted_iota(jnp.int32, (tq, tq), 0))
            for hh in range(n_heads):
                ls = slice(hh * hd, (hh + 1) * hd)
                q = q_ref[0, :, ls]
                fk = fk_ref[hh]
                s_d = lax.dot_general(q, k_ref[0, diag, ls], nt, preferred_element_type=F32) * scale - fk[:, diag]
                s_d = jnp.where(causal, s_d, NEG)
                m = jnp.max(s_d, axis=-1, keepdims=True)
                if ib:
                    s_o = (lax.dot_general(q, k_ref[0, :n_off, ls], nt, preferred_element_type=F32) * scale
                           - fk[:, :n_off])
                    m = jnp.maximum(m, jnp.max(s_o, axis=-1, keepdims=True))
                p_d = jnp.exp(s_d - m)
                l = jnp.sum(p_d, axis=-1, keepdims=True)
                o = jnp.dot(p_d.astype(BF16), v_ref[0, diag, ls], preferred_element_type=F32)
                if ib:
                    p_o = jnp.exp(s_o - m)
                    l = l + jnp.sum(p_o, axis=-1, keepdims=True)
                    o = o + jnp.dot(p_o.astype(BF16), v_ref[0, :n_off, ls], preferred_element_type=F32)
                o_ref[0, :, ls] = (o / l).astype(o_ref.dtype)


def fox_prompt_attention(qkv, f_cum, *, tq=256, n_heads=4):
    bsz, t, n = qkv.shape
    d = n // 3
    hw = n_heads * (d // FOX_HEADS)
    nhb = FOX_HEADS // n_heads
    return pl.pallas_call(
        functools.partial(_fox_prompt_kernel, scale=(d // FOX_HEADS) ** -0.5, n_heads=n_heads),
        out_shape=_sds((bsz, t, d), BF16),
        grid=(bsz, nhb, t // tq),
        in_specs=[pl.BlockSpec((1, tq, hw), lambda b, h, i: (b, i, h)),
                  pl.BlockSpec((1, t, hw), lambda b, h, i: (b, 0, nhb + h)),
                  pl.BlockSpec((1, t, hw), lambda b, h, i: (b, 0, 2 * nhb + h)),
                  pl.BlockSpec((n_heads, 1, t), lambda b, h, i: (b * nhb + h, 0, 0))],
        out_specs=pl.BlockSpec((1, tq, hw), lambda b, h, i: (b, i, h)),
        compiler_params=_cp("parallel", "parallel", "arbitrary"),
        name="fox_prompt_attn",
    )(qkv, qkv, qkv, f_cum)


def _fox_sample_kernel(pt_ref, q_ref, nkv_ref, nlf_ref, *rest, scale, t_new, heads, pps):
    pool_ref, lf_refs = rest[0], rest[1:1 + pps]
    o_ref, ring, ring_sem, pad_sc, m_sc, l_sc, acc_sc, suf_sc = rest[1 + pps:]
    b, j = pl.program_id(0), pl.program_id(1)
    n_steps = pl.num_programs(1) - 1
    total_steps = pl.num_programs(0) * n_steps
    page = nlf_ref.shape[2]
    hd = q_ref.shape[2] // heads
    r = heads * t_new
    key_rows = 2 * heads
    pitch = pad_sc.shape[1] // page
    src = lax.broadcasted_iota(jnp.int32, (page, page), 0)
    dst = lax.broadcasted_iota(jnp.int32, (page, page), 1)

    def page_copies(s, slot):
        bs, js = s // n_steps, s % n_steps
        n_pages = n_steps * pps
        return [pltpu.make_async_copy(pool_ref.at[pt_ref[bs * n_pages + n_pages - 1 - js * pps - u]],
                                      ring.at[slot, u], ring_sem.at[slot, u]) for u in range(pps)]

    @pl.when((b == 0) & (j == 0))
    def _():
        for s0 in range(FOX_RING_SLOTS - 1):
            for cp in page_copies(s0, s0):
                cp.start()

    def attend(kv_ref, slot, bias, valid, first):
        for key in range(page):
            pad_sc[slot, key * pitch:key * pitch + key_rows, :] = kv_ref[key * key_rows:(key + 1) * key_rows, :]
        rows_ref, row_pitch = pad_sc.at[slot], pitch
        s_rows = []
        for h in range(heads):
            k_h = rows_ref[pl.ds(h, page, stride=row_pitch), :].astype(BF16)
            s_h = lax.dot_general(q_ref[0, :, h * hd:(h + 1) * hd], k_h, (((1,), (1,)), ((), ())),
                                  preferred_element_type=F32)
            s_rows.append(s_h * scale + bias[h:h + 1, :])
        s = jnp.concatenate(s_rows, axis=0)
        if valid is not None:
            s = jnp.where(valid, s, NEG)
        m_cur = jnp.max(s, axis=-1, keepdims=True)
        m_new = m_cur if first else jnp.maximum(m_sc[...], m_cur)
        p = jnp.exp(s - m_new)
        pv = jnp.concatenate(
            [jnp.dot(p[h * t_new:(h + 1) * t_new].astype(BF16),
                     rows_ref[pl.ds(heads + h, page, stride=row_pitch), :].astype(BF16),
                     preferred_element_type=F32) for h in range(heads)], axis=0)
        l_cur = jnp.sum(p, axis=-1, keepdims=True)
        if first:
            l_sc[...] = l_cur
            acc_sc[...] = pv
        else:
            c = jnp.exp(m_sc[...] - m_new)
            l_sc[...] = l_sc[...] * c + l_cur
            acc_sc[...] = acc_sc[...] * c + pv
        m_sc[...] = m_new

    @pl.when(j == 0)
    def _():
        fq = _tri_sum(nlf_ref[0], jnp.where(src <= dst, 1.0, 0.0).astype(BF16))
        tq = lax.broadcasted_iota(jnp.int32, (r, page), 0) % t_new
        tk = lax.broadcasted_iota(jnp.int32, (r, page), 1)
        attend(nkv_ref.at[0], 0, -fq, tk <= tq, True)
        suf_sc[...] = jnp.zeros_like(suf_sc)

    @pl.when(j > 0)
    def _():
        s = b * n_steps + j - 1
        ahead = s + FOX_RING_SLOTS - 1

        @pl.when(ahead < total_steps)
        def _():
            for cp in page_copies(ahead, ahead % FOX_RING_SLOTS):
                cp.start()

        slot = s % FOX_RING_SLOTS
        for cp in page_copies(s, slot):
            cp.wait()
        later_mask = jnp.where(src > dst, 1.0, 0.0).astype(BF16)
        for u in range(pps):
            lf = lf_refs[u][0]
            suf = suf_sc[...]
            attend(ring.at[slot, u], u, _tri_sum(lf, later_mask) + suf, None, False)
            suf_sc[...] = suf + jnp.sum(lf, axis=-1, keepdims=True)

    @pl.when(j == pl.num_programs(1) - 1)
    def _():
        inv = 1.0 / l_sc[...]
        for h in range(heads):
            rs = slice(h * t_new, (h + 1) * t_new)
            o_ref[0, :, h * hd:(h + 1) * hd] = (acc_sc[rs, :] * inv[rs]).astype(o_ref.dtype)


def fox_sample_attention(qkv, kv_new, logf, kv_pool, logf_pool, page_table):
    bsz, t, n = qkv.shape
    d = n // 3
    heads = FOX_HEADS
    hd = d // heads
    n_pool, page = kv_pool.shape[0], kv_pool.shape[1]
    n_pages = page_table.shape[1]
    pps = FOX_PAGES_PER_STEP
    n_steps = n_pages // pps
    key_rows = 2 * heads
    rows = page * key_rows
    pool = kv_pool.reshape(n_pool, rows, hd)
    pool_lf = logf_pool.transpose(0, 2, 1)
    new_kv = jnp.pad(kv_new.reshape(bsz, t, key_rows, hd), ((0, 0), (0, page - t), (0, 0), (0, 0)))
    new_kv = new_kv.reshape(bsz, rows, hd)
    new_lf = jnp.pad(logf.transpose(0, 2, 1), ((0, 0), (0, 0), (0, page - t)))

    def page_idx(u):
        def index_map(b, j, pt_ref):
            return (pt_ref[b * n_pages + n_pages - 1 - (jnp.maximum(j, 1) - 1) * pps - u], 0, 0)
        return index_map

    own = lambda b, j, pt_ref: (b, 0, 0)
    grid_spec = pltpu.PrefetchScalarGridSpec(
        num_scalar_prefetch=1,
        grid=(bsz, n_steps + 1),
        in_specs=[pl.BlockSpec((1, t, d), own),
                  pl.BlockSpec((1, rows, hd), own),
                  pl.BlockSpec((1, heads, page), own),
                  pl.BlockSpec(memory_space=pl.ANY)]
                 + [pl.BlockSpec((1, heads, page), page_idx(u)) for u in range(pps)],
        out_specs=pl.BlockSpec((1, t, d), own),
        scratch_shapes=[pltpu.VMEM((FOX_RING_SLOTS, pps, rows, hd), F32),
                        pltpu.SemaphoreType.DMA((FOX_RING_SLOTS, pps)),
                        pltpu.VMEM((pps, page * (key_rows + FOX_PITCH_PAD), hd), F32),
                        pltpu.VMEM((heads * t, 1), F32), pltpu.VMEM((heads * t, 1), F32),
                        pltpu.VMEM((heads * t, hd), F32), pltpu.VMEM((heads, 1), F32)],
    )
    assert bsz * n_steps >= FOX_RING_SLOTS - 1
    return pl.pallas_call(
        functools.partial(_fox_sample_kernel, scale=hd ** -0.5, t_new=t, heads=heads, pps=pps),
        out_shape=_sds((bsz, t, d), BF16),
        grid_spec=grid_spec,
        compiler_params=_cp("arbitrary", "arbitrary"),
        name="fox_sample_attn",
    )(page_table.reshape(-1), qkv, new_kv, new_lf, pool, *([pool_lf] * pps))


def _flat(x):
    return x.reshape(-1, x.shape[-1])


def _last_rows(hist, new, n):
    t = new.shape[1]
    if t >= n:
        return new[:, t - n:]
    return jnp.concatenate([hist.astype(new.dtype)[:, t:], new], axis=1)


def _conv_tiles(bsz, t):
    return dict(tm=256, nb=1) if t >= 256 else dict(tm=t, nb=bsz)


def _mixer_a(x, gain, hist, w_in, conv_w, w_out, shape):
    bsz, t, d = shape
    m = bsz * t
    tm = min(1024, m)
    gb, u = matmul(x, w_in, tm=tm, tn=512, parts=3, epilogue=_ep_gate_split, norm_gain=gain,
                   out_shapes=(_sds((m, d), F32), _sds((m, d), F32)),
                   out_specs=(_tile_spec(tm, 512), _tile_spec(tm, 512)), name="a_in")
    u3 = u.reshape(bsz, t, d)
    a = causal_conv_fused(u3, hist, conv_w, taps=A_CONV, gate=gb.reshape(bsz, t, d), tc=512,
                          **_conv_tiles(bsz, t))
    x = matmul_residual(_flat(a), w_out, x, name="a_out")
    return x, _last_rows(hist, u3, A_CONV - 1)


def _dsa_kv_rows(qkv3, group, n_last):
    qkv3 = qkv3[:, qkv3.shape[1] - n_last:]
    bsz, t, _ = qkv3.shape
    k0 = DSA_GROUPS * DSA_GW + group * DSA_GW
    v0 = 2 * DSA_GROUPS * DSA_GW + group * DSA_GW
    k = qkv3[:, :, k0:k0 + DSA_GW].reshape(bsz, t, DSA_HPG, DSA_HEAD_DIM)
    v = qkv3[:, :, v0:v0 + DSA_GW].reshape(bsz, t, DSA_HPG, DSA_HEAD_DIM)
    return jnp.stack([k, v], axis=2)


def _mixer_b(x, gain, caches, w_qkv, qn, kn, w_out, shape):
    bsz, t, d = shape
    (qkv,) = matmul_qk_norm(x, gain, w_qkv, qn, kn, head_dim=DSA_HEAD_DIM, name="b_qkv")
    qkv3 = qkv.reshape(bsz, t, -1)
    if caches is None:
        outs, lses = zip(*[dsa_prompt_group(qkv3, g) for g in range(DSA_GROUPS)])
        bufs = [_dsa_kv_rows(qkv3, g, min(DSA_WINDOWS[g], t)) for g in range(DSA_GROUPS)]
    else:
        outs, lses = zip(*[dsa_sample_group(qkv3, caches[g], g) for g in range(DSA_GROUPS)])
        bufs = [jnp.concatenate([caches[g][:, t:], _dsa_kv_rows(qkv3, g, t)], axis=1) for g in range(DSA_GROUPS)]
    o = dsa_combine([_flat(a) for a in outs], [_flat(a) for a in lses])
    x = matmul_residual(o, w_out, x, name="b_out")
    return x, bufs


def _mixer_c(x, gain, hist, w_in, conv_w, ln_g, ln_b, w_out, shape):
    bsz, t, d = shape
    m = bsz * t
    tm = min(1024, m)
    u = matmul(x, w_in, tm=tm, tn=512, parts=2, epilogue=_ep_glu, norm_gain=gain,
               out_shapes=_sds((m, d), F32), out_specs=_tile_spec(tm, 512), name="c_in")
    u3 = u.reshape(bsz, t, d)
    tiles = _conv_tiles(bsz, t)
    a = causal_conv_fused(u3, hist, conv_w, taps=C_CONV, ln=(ln_g, ln_b), tm=min(tiles["tm"], 128), tc=d,
                          nb=tiles["nb"])
    x = matmul_residual(_flat(a), w_out, x, name="c_out")
    return x, _last_rows(hist, u3, C_CONV - 1)


def _mixer_d(x, gain, paged, w_qkv, w_f, b_f, qn, kn, w_out, shape):
    bsz, t, d = shape
    m = bsz * t
    tm = min(1024, m)
    qkv, kv = matmul_qk_norm(x, gain, w_qkv, qn, kn, head_dim=d // FOX_HEADS, n_cols=3 * d, split_kv=True,
                             name="d_qkv")
    logf = matmul(x, w_f, tm=tm, tn=LANES, epilogue=_ep_logf, norm_gain=gain, out_shapes=_sds((m, LANES), F32),
                  out_specs=_tile_spec(tm, LANES), extras=(b_f,),
                  extra_specs=(pl.BlockSpec((1, LANES), lambda i, j: (0, 0)),), name="d_logf")
    logf = logf[:, :FOX_HEADS].reshape(bsz, t, FOX_HEADS)
    qkv3 = qkv.reshape(bsz, t, 3 * d)
    kv3 = kv.reshape(bsz, t, 2 * d)
    if paged is None:
        f_cum = cumsum_lanes(logf.transpose(0, 2, 1).reshape(bsz * FOX_HEADS, t))
        o = fox_prompt_attention(qkv3, f_cum.reshape(bsz * FOX_HEADS, 1, t))
    else:
        o = fox_sample_attention(qkv3, kv3, logf, *paged)
    x = matmul_residual(_flat(o), w_out, x, name="d_out")
    return x, kv3.reshape(bsz, t, 2, FOX_HEADS, d // FOX_HEADS), logf


def kernel(x_prompt, x_sample, state_conv_a, cache_dsa_g0, cache_dsa_g1, cache_dsa_g2, state_conv_c,
           cache_fox_kv, cache_fox_logf, state_ffn_conv, page_table, p_prompt, p_sample,
           norm_mix, norm_ffn, w_a_in, conv_a_w, w_a_out, w_b_qkv, qn_b, kn_b, w_b_out,
           w_c_in, conv_c_w, ln_c_g, ln_c_b, w_c_out, w_d_qkvf, b_f, qn_d, kn_d, w_d_out,
           w_ffn_up, conv_ffn_w, w_ffn_down, norm_ple, w_ple_gate, w_ple_proj):
    depth = norm_mix.shape[0]
    d = x_prompt.shape[-1]
    d_ff = w_ffn_down.shape[1]
    shapes = (x_prompt.shape, x_sample.shape)
    xs = [_flat(x_prompt), _flat(x_sample)]
    ps = tuple(p.reshape(depth, -1, p.shape[-1]) for p in (p_prompt, p_sample))
    bf = lambda w: w.astype(BF16)
    w_up_bf, w_down_bf, w_gate_bf, w_proj_bf = bf(w_ffn_up), bf(w_ffn_down), bf(w_ple_gate), bf(w_ple_proj)

    w_d_qkv = bf(w_d_qkvf)
    n_f = w_d_qkvf.shape[1] - 3 * d
    w_d_f = bf(jnp.pad(w_d_qkvf[:, 3 * d:], ((0, 0), (0, LANES - n_f))))
    b_f_pad = jnp.pad(b_f.astype(F32), (0, LANES - n_f)).reshape(1, LANES)
    caches = (cache_dsa_g0, cache_dsa_g1, cache_dsa_g2)
    paged = (cache_fox_kv, cache_fox_logf, page_table)

    mixer_out = [dict(), dict()]
    ffn_hist = [[], []]
    for i in range(depth):
        kind = i % 4
        for grp in range(2):
            shape = shapes[grp]
            bsz, t, _ = shape
            x = xs[grp]
            out = mixer_out[grp]
            gain = norm_mix[i]
            if kind == 0:
                hist = jnp.zeros((bsz, A_CONV - 1, d), F32) if grp == 0 else state_conv_a
                x, out["conv_a"] = _mixer_a(x, gain, hist, bf(w_a_in), conv_a_w, bf(w_a_out), shape)
            elif kind == 1:
                x, out["dsa"] = _mixer_b(x, gain, None if grp == 0 else caches, bf(w_b_qkv), qn_b, kn_b,
                                         bf(w_b_out), shape)
            elif kind == 2:
                hist = jnp.zeros((bsz, C_CONV - 1, d), F32) if grp == 0 else state_conv_c
                x, out["conv_c"] = _mixer_c(x, gain, hist, bf(w_c_in), conv_c_w, ln_c_g, ln_c_b, bf(w_c_out), shape)
            else:
                x, out["fox_kv"], out["fox_logf"] = _mixer_d(
                    x, gain, None if grp == 0 else paged, w_d_qkv, w_d_f, b_f_pad, qn_d, kn_d, bf(w_d_out), shape)
            hist = jnp.zeros((bsz, FFN_CONV - 1, d_ff), F32) if grp == 0 else state_ffn_conv[i]
            x, g_last = ffn_fused(x, norm_ffn[i], hist, w_up_bf, conv_ffn_w, w_down_bf, i, shape)
            ffn_hist[grp].append(_last_rows(hist, g_last[:, SUBLANES - min(t, SUBLANES):], FFN_CONV - 1))
            xs[grp] = per_layer_embed(x, norm_ple[i], ps[grp], w_gate_bf, w_proj_bf, i)

    mp, ms = mixer_out
    return (xs[0].reshape(shapes[0]), xs[1].reshape(shapes[1]),
            mp["conv_a"], ms["conv_a"],
            mp["dsa"][0], ms["dsa"][0], mp["dsa"][1], ms["dsa"][1], mp["dsa"][2], ms["dsa"][2],
            mp["conv_c"], ms["conv_c"],
            mp["fox_kv"], ms["fox_kv"], mp["fox_logf"], ms["fox_logf"],
            jnp.stack(ffn_hist[0]), jnp.stack(ffn_hist[1]))
```
